```python
import jax, jax.numpy as jnp
from jax import lax
import numpy as np


D_MODEL = 1024
BATCH = 8
SEQ = 4096
DEPTH = 2

GRID_W = 64
CTX_LEN = 256
W_A = 256
H_A = 4
DH_A = 64
HGRN_CHUNK = 16
W_B = 384
H_B = 4
DH_B = 96
MLSTM_CHUNK = 64
W_C = 384
H_C = 6
DH_C = 64
WIN_ROWS = 8
WIN_COLS = 16
W_MIX = W_A + W_B + W_C
ROPE_BASE = 10000.0
EPS = 1e-6
SPLITS = (('a_q', W_A), ('a_ff', W_A), ('a_fb', W_A), ('a_i', W_A), ('a_z', W_A),
          ('b_q', W_B), ('b_k', W_B), ('b_v', W_B), ('b_o', W_B), ('b_z', W_B), ('b_g', 4 * H_B),
          ('c_q', W_C), ('c_k', W_C), ('c_v', W_C), ('c_z', W_C))
P_IN = 5 * W_A + 5 * W_B + 4 * H_B + 4 * W_C

kernel_name = 'hybrid_hgrn2_mlstm_natten_block'


def rms(x, g):
    x32 = x.astype(jnp.float32)
    return x32 * lax.rsqrt(jnp.mean(x32 * x32, axis=-1, keepdims=True) + EPS) * g.astype(jnp.float32)


def heads(a, h):
    return a.reshape(*a.shape[:-1], h, a.shape[-1] // h)


def head_rms(o, g):
    o = o * lax.rsqrt(jnp.mean(o * o, axis=-1, keepdims=True) + EPS)
    return o.reshape(*o.shape[:-2], -1) * g.astype(jnp.float32)


def split_proj(p):
    out = {}
    off = 0
    for name, w in SPLITS:
        out[name] = p[..., off:off + w]
        off += w
    return out


def rope_2d(x):
    T, d = x.shape[1], x.shape[-1]
    t = jnp.arange(T)
    half = d // 2

    def rot(xp, pos):
        dp = xp.shape[-1]
        inv = ROPE_BASE ** (-jnp.arange(0, dp, 2, dtype=jnp.float32) / dp)
        ang = pos.astype(jnp.float32)[:, None] * inv[None, :]
        cos = jnp.cos(ang)[None, :, None, :]
        sin = jnp.sin(ang)[None, :, None, :]
        x1, x2 = xp[..., :dp // 2], xp[..., dp // 2:]
        return jnp.concatenate([x1 * cos - x2 * sin, x2 * cos + x1 * sin], axis=-1)

    return jnp.concatenate([rot(x[..., :half], t // GRID_W), rot(x[..., half:], t % GRID_W)], axis=-1)


def gla_chunked(q, k, v, logf, s0):
    B, T, H, dk = q.shape
    dv = v.shape[-1]
    L = HGRN_CHUNK
    N = T // L
    q = q.reshape(B, N, L, H, dk)
    k = k.reshape(B, N, L, H, dk)
    logf = logf.reshape(B, N, L, H, dk)
    v = v.reshape(B, N, L, H, dv)
    b = jnp.cumsum(logf, axis=2)
    mask = np.tril(np.ones((L, L), dtype=bool))[None, None, :, :, None, None]
    decay = jnp.exp(jnp.where(mask, b[:, :, :, None] - b[:, :, None, :], -jnp.inf))
    A = jnp.einsum('bnthk,bnshk,bntshk->bntsh', q, k, decay)
    o_intra = jnp.einsum('bntsh,bnshv->bnthv', A, v)
    bL = b[:, :, -1]
    U = jnp.einsum('bnshk,bnshv->bnhkv', k * jnp.exp(bL[:, :, None] - b), v)

    def step(S, inp):
        a, u = inp
        return a[..., None] * S + u, S

    s_fin, s_prev = lax.scan(step, s0, (jnp.moveaxis(jnp.exp(bL), 1, 0), jnp.moveaxis(U, 1, 0)))
    s_prev = jnp.moveaxis(s_prev, 0, 1)
    o_inter = jnp.einsum('bnthk,bnhkv->bnthv', q * jnp.exp(b), s_prev)
    return (o_intra + o_inter).reshape(B, T, H, dv), s_fin


def mlstm_chunked(q, k, v, ig, lf, state0):
    B, T, H, dk = q.shape
    dv = v.shape[-1]
    L = MLSTM_CHUNK
    N = T // L
    q = q.reshape(B, N, L, H, dk)
    k = k.reshape(B, N, L, H, dk)
    v = v.reshape(B, N, L, H, dv)
    ig = ig.reshape(B, N, L, H)
    lf = lf.reshape(B, N, L, H)
    b = jnp.cumsum(lf, axis=2)
    bL = b[:, :, -1]
    w = bL[:, :, None] + ig - b
    m_loc = jnp.max(w, axis=2)
    ew = jnp.exp(w - m_loc[:, :, None])
    C_loc = jnp.einsum('bnsh,bnshk,bnshv->bnhkv', ew, k, v)
    n_loc = jnp.einsum('bnsh,bnshk->bnhk', ew, k)

    def step(carry, inp):
        C, n, m = carry
        bl, ml, Cl, nl = inp
        m_new = jnp.maximum(bl + m, ml)
        a = jnp.exp(bl + m - m_new)
        e = jnp.exp(ml - m_new)
        C_new = a[..., None, None] * C + e[..., None, None] * Cl
        n_new = a[..., None] * n + e[..., None] * nl
        return (C_new, n_new, m_new), (C, n, m)

    xs = tuple(jnp.moveaxis(a, 1, 0) for a in (bL, m_loc, C_loc, n_loc))
    final, (C_prev, n_prev, m_prev) = lax.scan(step, state0, xs)
    C_prev = jnp.moveaxis(C_prev, 0, 1)
    n_prev = jnp.moveaxis(n_prev, 0, 1)
    m_prev = jnp.moveaxis(m_prev, 0, 1)
    mask = np.tril(np.ones((L, L), dtype=bool))[None, None, :, :, None]
    D = jnp.where(mask, b[:, :, :, None] - b[:, :, None, :] + ig[:, :, None], -jnp.inf)
    inter = b + m_prev[:, :, None]
    m_t = jnp.maximum(jnp.max(D, axis=3), inter)
    S = jnp.einsum('bnthk,bnshk->bntsh', q, k) * jnp.exp(D - m_t[:, :, :, None])
    e_in = jnp.exp(inter - m_t)
    num = jnp.einsum('bntsh,bnshv->bnthv', S, v) + e_in[..., None] * jnp.einsum('bnthk,bnhkv->bnthv', q, C_prev)
    den = jnp.sum(S, axis=3) + e_in * jnp.einsum('bnthk,bnhk->bnth', q, n_prev)
    h = num / jnp.maximum(jnp.abs(den), jnp.exp(-m_t))[..., None]
    return h.reshape(B, T, H, dv), final


def hgrn2_mixer(px, pc, lb, gn, need_ctx):
    lb = lb.reshape(2, H_A, DH_A)

    def direction(p, d, s0, reverse):
        q = heads(jax.nn.silu(p['a_q']), H_A) * (DH_A ** -0.5)
        v = heads(p['a_i'], H_A)
        fl = heads(p[('a_ff', 'a_fb')[d]], H_A)
        lbd = lb[d]
        logf = jnp.logaddexp(jnp.log(lbd), jnp.log1p(-lbd) + jax.nn.log_sigmoid(fl))
        k = (1.0 - lbd) * jax.nn.sigmoid(-fl)
        if reverse:
            q, k, v, logf = (jnp.flip(a, axis=1) for a in (q, k, v, logf))
        o, s = gla_chunked(q, k, v, logf, s0)
        if reverse:
            o = jnp.flip(o, axis=1)
        return o, s

    B = px['a_q'].shape[0]
    s0 = jnp.zeros((B, H_A, DH_A, DH_A), jnp.float32)
    oc_f, sc_f = direction(pc, 0, s0, False)
    oc_b, sc_b = direction(pc, 1, s0, True)
    ol_f, _ = direction(px, 0, sc_f, False)
    ol_b, _ = direction(px, 1, sc_b, True)
    y = head_rms(ol_f + ol_b, gn) * jax.nn.silu(px['a_z'])
    yc = head_rms(oc_f + oc_b, gn) * jax.nn.silu(pc['a_z']) if need_ctx else None
    return y, yc


def mlstm_mixer(px, pc, gate_b, gn, need_ctx):
    def prep(p, rotary):
        q = heads(p['b_q'], H_B) * (DH_B ** -0.5)
        k = heads(p['b_k'], H_B)
        v = heads(p['b_v'], H_B)
        if rotary:
            q, k = rope_2d(q), rope_2d(k)
        g = p['b_g'].reshape(*p['b_g'].shape[:-1], 4, H_B) + gate_b.astype(jnp.float32)
        return q, k, v, g

    def direction(q, k, v, g, d, state0, reverse):
        ig = g[..., d, :]
        lf = jax.nn.log_sigmoid(g[..., 2 + d, :])
        if reverse:
            q, k, v, ig, lf = (jnp.flip(a, axis=1) for a in (q, k, v, ig, lf))
        h, st = mlstm_chunked(q, k, v, ig, lf, state0)
        if reverse:
            h = jnp.flip(h, axis=1)
        return h, st

    B = px['b_q'].shape[0]
    state0 = (jnp.zeros((B, H_B, DH_B, DH_B), jnp.float32), jnp.zeros((B, H_B, DH_B), jnp.float32),
              jnp.zeros((B, H_B), jnp.float32))
    qc, kc, vc, gc = prep(pc, False)
    ql, kl, vl, gl = prep(px, True)
    hc_f, st_f = direction(qc, kc, vc, gc, 0, state0, False)
    hc_b, st_b = direction(qc, kc, vc, gc, 1, state0, True)
    hl_f, _ = direction(ql, kl, vl, gl, 0, st_f, False)
    hl_b, _ = direction(ql, kl, vl, gl, 1, st_b, True)
    y = jax.nn.sigmoid(px['b_o']) * head_rms(hl_f + hl_b, gn) * jax.nn.silu(px['b_z'])
    yc = (jax.nn.sigmoid(pc['b_o']) * head_rms(hc_f + hc_b, gn) * jax.nn.silu(pc['b_z'])) if need_ctx else None
    return y, yc


def na_mixer(px, pc, rpb, need_ctx):
    scale = DH_C ** -0.5
    q = heads(px['c_q'], H_C) * scale
    k = heads(px['c_k'], H_C)
    v = heads(px['c_v'], H_C)
    qc = heads(pc['c_q'], H_C) * scale
    kc = heads(pc['c_k'], H_C)
    vc = heads(pc['c_v'], H_C)
    B, T = q.shape[0], q.shape[1]
    rows = T // GRID_W
    win_r = min(WIN_ROWS, rows)
    qg = q.reshape(B, rows, GRID_W, H_C, DH_C)
    kg = k.reshape(B, rows, GRID_W, H_C, DH_C)
    vg = v.reshape(B, rows, GRID_W, H_C, DH_C)
    col = np.arange(GRID_W)
    cs = np.clip(col - WIN_COLS // 2, 0, GRID_W - WIN_COLS)
    band = (col[None, :] >= cs[:, None]) & (col[None, :] < cs[:, None] + WIN_COLS)
    dc_idx = np.clip(col[None, :] - col[:, None] + WIN_COLS - 1, 0, 2 * WIN_COLS - 2)
    rpb_c = rpb.astype(jnp.float32)[:, :, dc_idx]
    band_b = band[:, None, :]

    def row_fn(r):
        rs = jnp.clip(r - win_r // 2, 0, rows - win_r)
        q_r = lax.dynamic_index_in_dim(qg, r, axis=1, keepdims=False)
        k_b = lax.dynamic_slice_in_dim(kg, rs, win_r, axis=1)
        v_b = lax.dynamic_slice_in_dim(vg, rs, win_r, axis=1)
        dr = rs - r + jnp.arange(win_r) + WIN_ROWS - 1
        bias = jnp.transpose(jnp.take(rpb_c, dr, axis=1), (0, 2, 1, 3))
        s_w = jnp.einsum('bchd,bjkhd->bhcjk', q_r, k_b) + bias[None]
        s_w = jnp.where(band_b, s_w, -jnp.inf)
        s_c = jnp.einsum('bchd,bnhd->bhcn', q_r, kc)
        logits = jnp.concatenate([s_w.reshape(B, H_C, GRID_W, win_r * GRID_W), s_c], axis=-1).astype(jnp.float32)
        p = jax.nn.softmax(logits, axis=-1)
        p_w = p[..., :win_r * GRID_W].reshape(B, H_C, GRID_W, win_r, GRID_W)
        p_c = p[..., win_r * GRID_W:]
        return jnp.einsum('bhcjk,bjkhd->bchd', p_w, v_b) + jnp.einsum('bhcn,bnhd->bchd', p_c, vc)

    o = lax.map(row_fn, jnp.arange(rows))
    o = jnp.moveaxis(o, 0, 1).reshape(B, T, W_C)
    y = o * jax.nn.silu(px['c_z'])
    yc = None
    if need_ctx:
        s = jnp.einsum('bnhd,bmhd->bhnm', qc, kc).astype(jnp.float32)
        oc = jnp.einsum('bhnm,bmhd->bnhd', jax.nn.softmax(s, axis=-1), vc)
        yc = oc.reshape(B, oc.shape[1], W_C) * jax.nn.silu(pc['c_z'])
    return y, yc


def setup_inputs(seed: int = 0) -> dict:
    key = jax.random.key(seed)
    ks = jax.random.split(key, 16)
    nrm = jax.random.normal
    f32 = jnp.float32
    gate_b = jnp.concatenate([0.1 * nrm(ks[12], (DEPTH, 2, H_B), f32),
                              3.0 + 0.5 * nrm(ks[13], (DEPTH, 2, H_B), f32)], axis=1)
    return {
        'x': nrm(ks[0], (BATCH, SEQ, D_MODEL), f32),
        'c': nrm(ks[1], (BATCH, D_MODEL), f32),
        'ctx': nrm(ks[2], (BATCH, CTX_LEN, D_MODEL), f32),
        'c_ctx': nrm(ks[3], (D_MODEL,), f32),
        'w_mod': 0.5 * D_MODEL ** -0.5 * nrm(ks[4], (DEPTH, D_MODEL, 3 * D_MODEL), f32),
        'b_mod': 0.02 * nrm(ks[5], (DEPTH, 3 * D_MODEL), f32),
        'g_pre': 1.0 + 0.02 * nrm(ks[6], (DEPTH, D_MODEL), f32),
        'g_post': 1.0 + 0.02 * nrm(ks[7], (DEPTH, D_MODEL), f32),
        'w_in': D_MODEL ** -0.5 * nrm(ks[8], (DEPTH, D_MODEL, P_IN), f32),
        'w_out': W_MIX ** -0.5 * nrm(ks[9], (DEPTH, W_MIX, D_MODEL), f32),
        'hgrn_lb': 0.5 * nrm(ks[10], (DEPTH, 2, W_A), f32),
        'hgrn_gn': 1.0 + 0.02 * nrm(ks[11], (DEPTH, W_A), f32),
        'mlstm_gate_b': gate_b,
        'mlstm_gn': 1.0 + 0.02 * nrm(ks[14], (DEPTH, W_B), f32),
        'na_rpb': 0.02 * nrm(ks[15], (DEPTH, H_C, 2 * WIN_ROWS - 1, 2 * WIN_COLS - 1), f32),
    }


def reference(x, c, ctx, c_ctx, w_mod, b_mod, g_pre, g_post, w_in, w_out, hgrn_lb, hgrn_gn, mlstm_gate_b, mlstm_gn, na_rpb):
    lb_cum = jnp.cumsum(jax.nn.softmax(hgrn_lb.astype(jnp.float32), axis=0), axis=0)
    lb_all = lb_cum - lb_cum[0]
    for l in range(DEPTH):
        need_ctx = l < DEPTH - 1
        mod = jax.nn.silu(c.astype(jnp.float32)) @ w_mod[l] + b_mod[l]
        mod_c = jax.nn.silu(c_ctx.astype(jnp.float32)) @ w_mod[l] + b_mod[l]
        sh, sc, gt = jnp.split(mod, 3, axis=-1)
        shc, scc, gtc = jnp.split(mod_c, 3, axis=-1)
        hx = rms(x, g_pre[l]) * (1.0 + sc[:, None]) + sh[:, None]
        hc = rms(ctx, g_pre[l]) * (1.0 + scc) + shc
        px = split_proj(hx @ w_in[l])
        pc = split_proj(hc @ w_in[l])
        ya, yac = hgrn2_mixer(px, pc, lb_all[l], hgrn_gn[l], need_ctx)
        yb, ybc = mlstm_mixer(px, pc, mlstm_gate_b[l], mlstm_gn[l], need_ctx)
        yc, ycc = na_mixer(px, pc, na_rpb[l], need_ctx)
        ux = jnp.concatenate([ya, yb, yc], axis=-1) @ w_out[l]
        x = x + (gt[:, None] * rms(ux, g_post[l])).astype(x.dtype)
        if need_ctx:
            uc = jnp.concatenate([yac, ybc, ycc], axis=-1) @ w_out[l]
            ctx = ctx + (gtc * rms(uc, g_post[l])).astype(ctx.dtype)
    return x
```

```python
import functools

import numpy as np
import jax
import jax.numpy as jnp
from jax import lax
from jax.experimental import pallas as pl
from jax.experimental.pallas import tpu as pltpu

F32 = jnp.float32
BF16 = jnp.bfloat16

LANES = 128
GRID_W = 64
W_A, H_A, DH_A = 256, 4, 64
W_B, H_B, DH_B = 384, 4, 96
W_C, H_C, DH_C = 384, 6, 64
WIN_ROWS, WIN_COLS = 8, 16
ROPE_BASE = 10000.0
EPS = 1e-6
NEG = -1e30
CHUNK = 128
N_LEVELS = 7
OFF_B = 5 * W_A
OFF_G = OFF_B + 5 * W_B
OFF_C = OFF_G + 4 * H_B
P_IN = OFF_C + 4 * W_C
ONE_LANE = DH_B
VMEM_LIMIT = 56 * 1024 * 1024


def _cparams(*sem):
    return pltpu.CompilerParams(dimension_semantics=sem, vmem_limit_bytes=VMEM_LIMIT)


def _dot(a, b):
    return jnp.dot(a, b, preferred_element_type=F32)


def _dot_nt(a, b):
    return lax.dot_general(a, b, (((1,), (1,)), ((), ())), preferred_element_type=F32)


def _dot_tn(a, b):
    return lax.dot_general(a, b, (((0,), (0,)), ((), ())), preferred_element_type=F32)


def _split3(x):
    hi = x.astype(BF16)
    r = x - hi.astype(F32)
    mid = r.astype(BF16)
    lo = (r - mid.astype(F32)).astype(BF16)
    return hi, mid, lo


def _exact_left(t01, x):
    hi, mid, lo = _split3(x)
    return _dot(t01, hi) + _dot(t01, mid) + _dot(t01, lo)


def _exact_right(x, t01):
    hi, mid, lo = _split3(x)
    return _dot(hi, t01) + _dot(mid, t01) + _dot(lo, t01)


def _sigmoid(x):
    return 1.0 / (1.0 + jnp.exp(-x))


def _silu(x):
    return x * _sigmoid(x)


def _log_sigmoid(x):
    return jnp.minimum(x, 0.0) - jnp.log1p(jnp.exp(-jnp.abs(x)))


def _mod_kernel(c_ref, w_ref, b_ref, o_ref):
    s = _silu(c_ref[...])
    o_ref[...] = _dot(s.astype(BF16), w_ref[...].astype(BF16)) + b_ref[...]


def _modulation(cc, w, b):
    rows, d = cc.shape
    n = w.shape[1]
    tn = d
    assert n % tn == 0
    return pl.pallas_call(
        _mod_kernel,
        grid=(n // tn,),
        in_specs=[pl.BlockSpec((rows, d), lambda j: (0, 0)),
                  pl.BlockSpec((d, tn), lambda j: (0, j)),
                  pl.BlockSpec((1, tn), lambda j: (0, j))],
        out_specs=pl.BlockSpec((rows, tn), lambda j: (0, j)),
        out_shape=jax.ShapeDtypeStruct((rows, n), F32),
        compiler_params=_cparams("arbitrary"),
        name="modulation",
    )(cc, w, b.reshape(1, n))


def _inproj_kernel(*refs, rotary):
    if rotary:
        (x_ref, sh_ref, sc_ref, g_ref, wa_ref, wb_ref, wc_ref, wg_ref, wgt_ref, gbr_ref, gbc_ref,
         cos_ref, sin_ref, pa_ref, pb_ref, pc_ref, pg_ref, pgt_ref) = refs
    else:
        (x_ref, sh_ref, sc_ref, g_ref, wa_ref, wb_ref, wc_ref, wg_ref, wgt_ref, gbr_ref, gbc_ref,
         pa_ref, pb_ref, pc_ref, pg_ref, pgt_ref) = refs
    x = x_ref[0]
    ms = jnp.mean(x * x, axis=-1, keepdims=True)
    h = x * lax.rsqrt(ms + EPS) * g_ref[...]
    h = h * (1.0 + sc_ref[0]) + sh_ref[0]
    hb = h.astype(BF16)
    pa_ref[0] = _dot(hb, wa_ref[...])
    pc_ref[0] = _dot(hb, wc_ref[...])
    pg_ref[0] = _dot(hb, wg_ref[...]) + gbr_ref[...]
    pgt_ref[0] = _dot_nt(wgt_ref[...], hb) + gbc_ref[...]
    hw = H_B * LANES
    scale = DH_B ** -0.5
    for part in range(5):
        p = _dot(hb, wb_ref[:, part * hw:(part + 1) * hw])
        if part == 0:
            p = p * scale
        if rotary and part < 2:
            cos = cos_ref[...]
            sin = sin_ref[...]
            for hd in range(H_B):
                ph = p[:, hd * LANES:(hd + 1) * LANES]
                pb_ref[0, :, part * hw + hd * LANES:part * hw + (hd + 1) * LANES] = (
                    ph * cos + pltpu.roll(ph, LANES // 2, 1) * sin)
        else:
            pb_ref[0, :, part * hw:(part + 1) * hw] = p


def _in_projection(x, sh, sc, g, wts, rope, tm):
    b, t, d = x.shape
    wa, wb, wc, wg, wgt, gbr, gbc = wts
    bm = sh.shape[0]
    mod_map = (lambda i, j: (i, 0, 0)) if bm > 1 else (lambda i, j: (0, 0, 0))
    const = lambda i, j: (0, 0)
    in_specs = [pl.BlockSpec((1, tm, d), lambda i, j: (i, j, 0)),
                pl.BlockSpec((1, 1, d), mod_map),
                pl.BlockSpec((1, 1, d), mod_map),
                pl.BlockSpec((1, d), const),
                pl.BlockSpec(wa.shape, const),
                pl.BlockSpec(wb.shape, const),
                pl.BlockSpec(wc.shape, const),
                pl.BlockSpec(wg.shape, const),
                pl.BlockSpec(wgt.shape, const),
                pl.BlockSpec(gbr.shape, const),
                pl.BlockSpec(gbc.shape, const)]
    args = [x, sh, sc, g.reshape(1, d), wa, wb, wc, wg, wgt, gbr, gbc]
    if rope is not None:
        in_specs += [pl.BlockSpec((tm, LANES), lambda i, j: (j, 0))] * 2
        args += list(rope)
    na, nb, nc, ng = wa.shape[1], wb.shape[1], wc.shape[1], wg.shape[1]
    out_specs = [pl.BlockSpec((1, tm, na), lambda i, j: (i, j, 0)),
                 pl.BlockSpec((1, tm, nb), lambda i, j: (i, j, 0)),
                 pl.BlockSpec((1, tm, nc), lambda i, j: (i, j, 0)),
                 pl.BlockSpec((1, tm, ng), lambda i, j: (i, j, 0)),
                 pl.BlockSpec((1, wgt.shape[0], tm), lambda i, j: (i, 0, j))]
    out_shape = [jax.ShapeDtypeStruct((b, t, na), F32),
                 jax.ShapeDtypeStruct((b, t, nb), F32),
                 jax.ShapeDtypeStruct((b, t, nc), F32),
                 jax.ShapeDtypeStruct((b, t, ng), F32),
                 jax.ShapeDtypeStruct((b, wgt.shape[0], t), F32)]
    return pl.pallas_call(
        functools.partial(_inproj_kernel, rotary=rope is not None),
        grid=(b, t // tm),
        in_specs=in_specs,
        out_specs=out_specs,
        out_shape=out_shape,
        compiler_params=_cparams("arbitrary", "arbitrary"),
        name="in_projection_rope" if rope is not None else "in_projection",
    )(*args)


def _hgrn_tables():
    L, nl = CHUNK, N_LEVELS
    idx = np.arange(L)
    t = idx[:, None]
    u = idx[None, :]
    stacks, levels = [], []
    for rev in (False, True):
        mats = []
        for lv in range(nl):
            h = 1 << lv
            mid = (t // (2 * h)) * 2 * h + h
            if not rev:
                m = ((t // h) % 2 == 1) & (u >= mid) & (u <= t)
            else:
                m = ((t // h) % 2 == 0) & (u >= t) & (u <= mid - 1)
            mats.append(m)
        for lv in range(nl):
            h = 1 << lv
            mid = (t // (2 * h)) * 2 * h + h
            if not rev:
                m = ((t // h) % 2 == 0) & (u > t) & (u <= mid - 1)
            else:
                m = ((t // h) % 2 == 1) & (u >= mid) & (u < t)
            mats.append(m)
        mats.append((u <= t) if not rev else (u >= t))
        mats.append((u > t) if not rev else (u < t))
        stacks.append(np.concatenate(mats, axis=0).astype(np.float32))
        x = t ^ u
        lvl = np.where(x > 0, np.floor(np.log2(np.maximum(x, 1))), float(nl))
        valid = (u <= t) if not rev else (u >= t)
        levels.append(np.where(valid, lvl, -1.0).astype(np.float32))
    return jnp.asarray(np.stack(stacks), dtype=BF16), jnp.asarray(np.stack(levels), dtype=F32)


def _tri_tables():
    L = CHUNK
    idx = np.arange(L)
    t = idx[:, None]
    u = idx[None, :]
    fwd = (u <= t).astype(np.float32)
    bwd = (u >= t).astype(np.float32)
    tri = np.stack([fwd, bwd])
    trit = np.stack([fwd.T, bwd.T])
    return jnp.asarray(tri, dtype=BF16), jnp.asarray(trit, dtype=BF16)


def _hgrn_kernel(*refs, need_ctx, n_lat, n_ctx):
    (q_ref, ff_ref, fb_ref, i_ref, z_ref, qc_ref, ffc_ref, fbc_ref, ic_ref, zc_ref,
     lb_ref, gn_ref, tab_ref, lvl_ref) = refs[:14]
    if need_ctx:
        y_ref, yc_ref, of_ref, ofc_ref = refs[14:]
    else:
        y_ref, of_ref = refs[14:]
        yc_ref = ofc_ref = None
    L, nl = CHUNK, N_LEVELS
    lane = lax.broadcasted_iota(jnp.int32, (1, LANES), 1)
    hm0 = lane < DH_A
    row_h = lax.broadcasted_iota(jnp.int32, (LANES, LANES), 0) // DH_A
    col_h = lax.broadcasted_iota(jnp.int32, (LANES, LANES), 1) // DH_A
    same_head = row_h == col_h
    gn = gn_ref[0]

    def chunk(aq, fl, ai, st, d):
        lbd = lb_ref[d, 0]
        a0 = jnp.log(lbd)
        b0 = jnp.log1p(-lbd) + _log_sigmoid(fl)
        logf = jnp.maximum(a0, b0) + jnp.log1p(jnp.exp(-jnp.abs(a0 - b0)))
        k = (1.0 - lbd) * _sigmoid(-fl)
        q = _silu(aq) * (DH_A ** -0.5)
        e = _exact_left(tab_ref[d], logf)
        lvl = lvl_ref[d]
        v0 = jnp.where(hm0, ai, 0.0).astype(BF16)
        v1 = jnp.where(hm0, 0.0, ai).astype(BF16)
        a_h0 = jnp.zeros((L, L), F32)
        a_h1 = jnp.zeros((L, L), F32)
        for lv in range(nl + 1):
            if lv < nl:
                qt = q * jnp.exp(e[lv * L:(lv + 1) * L])
                kt = (k * jnp.exp(e[(nl + lv) * L:(nl + lv + 1) * L])).astype(BF16)
            else:
                qt = q
                kt = k.astype(BF16)
            sel = lvl == float(lv)
            x0 = _dot_nt(jnp.where(hm0, qt, 0.0).astype(BF16), kt)
            x1 = _dot_nt(jnp.where(hm0, 0.0, qt).astype(BF16), kt)
            a_h0 = jnp.where(sel, x0, a_h0)
            a_h1 = jnp.where(sel, x1, a_h1)
        bcum = e[2 * nl * L:(2 * nl + 1) * L]
        rem = e[(2 * nl + 1) * L:(2 * nl + 2) * L]
        o = _dot(a_h0.astype(BF16), v0) + _dot(a_h1.astype(BF16), v1)
        o = o + _dot_nt((q * jnp.exp(bcum)).astype(BF16), st.astype(BF16))
        ut = _dot_tn(ai.astype(BF16), (k * jnp.exp(rem)).astype(BF16))
        b_tot = bcum[L - 1:L] if d == 0 else bcum[0:1]
        st_new = st * jnp.exp(b_tot) + jnp.where(same_head, ut, 0.0)
        return o, st_new

    def finish(o, z):
        sq = o * o
        ms0 = jnp.sum(jnp.where(hm0, sq, 0.0), axis=-1, keepdims=True)
        ms1 = jnp.sum(jnp.where(hm0, 0.0, sq), axis=-1, keepdims=True)
        ms = jnp.where(hm0, ms0, ms1) * (1.0 / DH_A)
        return o * lax.rsqrt(ms + EPS) * gn * _silu(z)

    def run(d, st, n, aq_ref, f_ref, ai_ref, zz_ref, o_scr, out_ref, emit):
        def body(i, st):
            c = i if d == 0 else n - 1 - i
            rows = pl.ds(pl.multiple_of(c * L, L), L)
            o, st = chunk(aq_ref[0, rows, :], f_ref[0, rows, :], ai_ref[0, rows, :], st, d)
            if emit:
                if d == 0:
                    o_scr[rows, :] = o
                else:
                    out_ref[0, rows, :] = finish(o + o_scr[rows, :], zz_ref[0, rows, :])
            return st
        return lax.fori_loop(0, n, body, st)

    st0 = jnp.zeros((LANES, LANES), F32)
    st = run(0, st0, n_ctx, qc_ref, ffc_ref, ic_ref, zc_ref, ofc_ref, yc_ref, need_ctx)
    run(0, st, n_lat, q_ref, ff_ref, i_ref, z_ref, of_ref, y_ref, True)
    st = run(1, st0, n_ctx, qc_ref, fbc_ref, ic_ref, zc_ref, ofc_ref, yc_ref, need_ctx)
    run(1, st, n_lat, q_ref, fb_ref, i_ref, z_ref, of_ref, y_ref, True)


def _hgrn_mixer(pa, pac, lb, gn, tabs, need_ctx):
    b, t, _ = pa.shape
    tc = pac.shape[1]
    tab, lvl = tabs
    npair = W_A // LANES
    lat = lambda part: pl.BlockSpec((1, t, LANES), lambda i, p, part=part: (i, 0, part * npair + p))
    ctx = lambda part: pl.BlockSpec((1, tc, LANES), lambda i, p, part=part: (i, 0, part * npair + p))
    in_specs = ([lat(0), lat(1), lat(2), lat(3), lat(4), ctx(0), ctx(1), ctx(2), ctx(3), ctx(4),
                 pl.BlockSpec((2, 1, 1, LANES), lambda i, p: (0, p, 0, 0)),
                 pl.BlockSpec((1, 1, LANES), lambda i, p: (p, 0, 0)),
                 pl.BlockSpec(tab.shape, lambda i, p: (0, 0, 0)),
                 pl.BlockSpec(lvl.shape, lambda i, p: (0, 0, 0))])
    out_specs = [pl.BlockSpec((1, t, LANES), lambda i, p: (i, 0, p))]
    out_shape = [jax.ShapeDtypeStruct((b, t, W_A), F32)]
    scratch = [pltpu.VMEM((t, LANES), F32)]
    if need_ctx:
        out_specs.append(pl.BlockSpec((1, tc, LANES), lambda i, p: (i, 0, p)))
        out_shape.append(jax.ShapeDtypeStruct((b, tc, W_A), F32))
        scratch.append(pltpu.VMEM((tc, LANES), F32))
    res = pl.pallas_call(
        functools.partial(_hgrn_kernel, need_ctx=need_ctx, n_lat=t // CHUNK, n_ctx=tc // CHUNK),
        grid=(b, npair),
        in_specs=in_specs,
        out_specs=out_specs,
        out_shape=out_shape,
        scratch_shapes=scratch,
        compiler_params=_cparams("arbitrary", "arbitrary"),
        name="hgrn2_mixer_ctx" if need_ctx else "hgrn2_mixer",
    )(pa, pa, pa, pa, pa, pac, pac, pac, pac, pac,
      lb.reshape(2, npair, 1, LANES), gn.reshape(npair, 1, LANES), tab, lvl)
    return (res[0], res[1]) if need_ctx else (res[0], None)


def _mlstm_kernel(*refs, need_ctx, n_lat, n_ctx):
    (q_ref, k_ref, v_ref, o_ref, z_ref, qc_ref, kc_ref, vc_ref, oc_ref, zc_ref,
     g_ref, gt_ref, gc_ref, gtc_ref, gn_ref, tri_ref, trit_ref) = refs[:17]
    if need_ctx:
        y_ref, yc_ref, hf_ref, hfc_ref = refs[17:]
    else:
        y_ref, hf_ref = refs[17:]
        yc_ref = hfc_ref = None
    L = CHUNK
    lane = lax.broadcasted_iota(jnp.int32, (1, LANES), 1)
    real = lane < DH_B
    rows_i = lax.broadcasted_iota(jnp.int32, (L, L), 0)
    cols_i = lax.broadcasted_iota(jnp.int32, (L, L), 1)
    gn = gn_ref[0]

    def chunk(q, k, v, gcol, grow, state, d, emit):
        c_prev, m_prev = state
        ig_col = gcol[:, d:d + 1]
        b_col = _exact_left(tri_ref[d], _log_sigmoid(gcol))[:, 2 + d:3 + d]
        ig_row = grow[d:d + 1, :]
        b_row = _exact_right(_log_sigmoid(grow), trit_ref[d])[2 + d:3 + d, :]
        b_tot = b_col[L - 1:L] if d == 0 else b_col[0:1]
        qb = q.astype(BF16)
        kb = k.astype(BF16)
        va = jnp.where(lane == ONE_LANE, 1.0, v).astype(BF16)
        h = None
        if emit:
            valid = (cols_i <= rows_i) if d == 0 else (cols_i >= rows_i)
            dm = jnp.where(valid, b_col - b_row + ig_row, NEG)
            inter = b_col + m_prev
            m_t = jnp.maximum(jnp.max(dm, axis=-1, keepdims=True), inter)
            p = _dot_nt(qb, kb) * jnp.exp(dm - m_t)
            e_in = jnp.exp(inter - m_t)
            qc = _dot(qb, c_prev.astype(BF16))
            num = _dot(p.astype(BF16), va) + e_in * qc
            den = jnp.sum(p, axis=-1, keepdims=True) + e_in * qc[:, ONE_LANE:ONE_LANE + 1]
            h = num / jnp.maximum(jnp.abs(den), jnp.exp(-m_t))
            h = jnp.where(real, h, 0.0)
        w = b_tot + ig_col - b_col
        m_new = jnp.maximum(b_tot + m_prev, jnp.max(w, axis=0, keepdims=True))
        kw = (k * jnp.exp(w - m_new)).astype(BF16)
        c_new = jnp.exp(b_tot + m_prev - m_new) * c_prev + _dot_tn(kw, va)
        return h, (c_new, m_new)

    def finish(h, og, z):
        ms = jnp.sum(h * h, axis=-1, keepdims=True) * (1.0 / DH_B)
        return _sigmoid(og) * (h * lax.rsqrt(ms + EPS) * gn) * _silu(z)

    def run(d, state, n, qq, kk, vv, oo, zz, gg, ggt, h_scr, out_ref, emit):
        def body(i, state):
            c = i if d == 0 else n - 1 - i
            start = pl.multiple_of(c * L, L)
            rows = pl.ds(start, L)
            h, state = chunk(qq[0, rows, :], kk[0, rows, :], vv[0, rows, :], gg[0, rows, :],
                             ggt[0, :, rows], state, d, emit)
            if emit:
                if d == 0:
                    h_scr[rows, :] = h
                else:
                    out_ref[0, rows, :] = finish(h + h_scr[rows, :], oo[0, rows, :], zz[0, rows, :])
            return state
        return lax.fori_loop(0, n, body, state)

    s0 = (jnp.zeros((LANES, LANES), F32), jnp.zeros((1, 1), F32))
    lat = (q_ref, k_ref, v_ref, o_ref, z_ref, g_ref, gt_ref, hf_ref, y_ref)
    ctx = (qc_ref, kc_ref, vc_ref, oc_ref, zc_ref, gc_ref, gtc_ref, hfc_ref, yc_ref)
    s = run(0, s0, n_ctx, *ctx, need_ctx)
    run(0, s, n_lat, *lat, True)
    s = run(1, s0, n_ctx, *ctx, need_ctx)
    run(1, s, n_lat, *lat, True)


def _mlstm_mixer(pb, pg, pgt, pbc, pgc, pgtc, gn, tris, need_ctx):
    b, t, _ = pb.shape
    tc = pbc.shape[1]
    tri, trit = tris
    lat = lambda part: pl.BlockSpec((1, t, LANES), lambda i, h, part=part: (i, 0, part * H_B + h))
    ctx = lambda part: pl.BlockSpec((1, tc, LANES), lambda i, h, part=part: (i, 0, part * H_B + h))
    in_specs = ([lat(p) for p in range(5)] + [ctx(p) for p in range(5)] +
                [pl.BlockSpec((1, t, LANES), lambda i, h: (i, 0, h)),
                 pl.BlockSpec((1, 8, t), lambda i, h: (i, h, 0)),
                 pl.BlockSpec((1, tc, LANES), lambda i, h: (i, 0, h)),
                 pl.BlockSpec((1, 8, tc), lambda i, h: (i, h, 0)),
                 pl.BlockSpec((1, 1, LANES), lambda i, h: (h, 0, 0)),
                 pl.BlockSpec(tri.shape, lambda i, h: (0, 0, 0)),
                 pl.BlockSpec(trit.shape, lambda i, h: (0, 0, 0))])
    out_specs = [pl.BlockSpec((1, t, LANES), lambda i, h: (i, 0, h))]
    out_shape = [jax.ShapeDtypeStruct((b, t, H_B * LANES), F32)]
    scratch = [pltpu.VMEM((t, LANES), F32)]
    if need_ctx:
        out_specs.append(pl.BlockSpec((1, tc, LANES), lambda i, h: (i, 0, h)))
        out_shape.append(jax.ShapeDtypeStruct((b, tc, H_B * LANES), F32))
        scratch.append(pltpu.VMEM((tc, LANES), F32))
    res = pl.pallas_call(
        functools.partial(_mlstm_kernel, need_ctx=need_ctx, n_lat=t // CHUNK, n_ctx=tc // CHUNK),
        grid=(b, H_B),
        in_specs=in_specs,
        out_specs=out_specs,
        out_shape=out_shape,
        scratch_shapes=scratch,
        compiler_params=_cparams("arbitrary", "arbitrary"),
        name="mlstm_mixer_ctx" if need_ctx else "mlstm_mixer",
    )(pb, pb, pb, pb, pb, pbc, pbc, pbc, pbc, pbc, pg, pgt, pgc, pgtc, gn, tri, trit)
    return (res[0], res[1]) if need_ctx else (res[0], None)


def _natten_kernel(*refs, need_ctx, n_rows):
    if need_ctx:
        (q_ref, k_ref, v_ref, z_ref, kc_ref, vc_ref, bias_ref, qc_ref, zc_ref, y_ref, yc_ref,
         k0_ref, k1_ref, v0_ref, v1_ref, kc0_ref, kc1_ref, vc0_ref, vc1_ref) = refs
    else:
        (q_ref, k_ref, v_ref, z_ref, kc_ref, vc_ref, bias_ref, y_ref,
         k0_ref, k1_ref, v0_ref, v1_ref, kc0_ref, kc1_ref, vc0_ref, vc1_ref) = refs
    lane = lax.broadcasted_iota(jnp.int32, (1, LANES), 1)
    hm0 = lane < DH_C
    scale = DH_C ** -0.5
    win = WIN_ROWS * GRID_W

    def split_heads(src, d0, d1):
        blk = CHUNK

        def body(i, carry):
            rows = pl.ds(pl.multiple_of(i * blk, blk), blk)
            a = src[0, rows, :]
            d0[rows, :] = jnp.where(hm0, a, 0.0).astype(BF16)
            d1[rows, :] = jnp.where(hm0, 0.0, a).astype(BF16)
            return carry

        lax.fori_loop(0, src.shape[1] // blk, body, 0)

    split_heads(k_ref, k0_ref, k1_ref)
    split_heads(v_ref, v0_ref, v1_ref)
    split_heads(kc_ref, kc0_ref, kc1_ref)
    split_heads(vc_ref, vc0_ref, vc1_ref)

    def attend(qb, parts):
        logits = []
        for kk, _, bias in parts:
            s = _dot_nt(qb, kk)
            logits.append(s if bias is None else s + bias)
        m = logits[0].max(axis=-1, keepdims=True)
        for s in logits[1:]:
            m = jnp.maximum(m, s.max(axis=-1, keepdims=True))
        den = 0.0
        acc = 0.0
        for s, (_, vv, _) in zip(logits, parts):
            p = jnp.exp(s - m)
            den = den + jnp.sum(p, axis=-1, keepdims=True)
            acc = acc + _dot(p.astype(BF16), vv)
        return acc / den

    def row(r, carry):
        rs = jnp.clip(r - WIN_ROWS // 2, 0, n_rows - WIN_ROWS)
        dr0 = rs - r + (WIN_ROWS - 1)
        qrows = pl.ds(pl.multiple_of(r * GRID_W, GRID_W), GRID_W)
        wrows = pl.ds(pl.multiple_of(rs * GRID_W, GRID_W), win)
        qb = (q_ref[0, qrows, :] * scale).astype(BF16)
        o = attend(qb, [(k0_ref[wrows, :], v0_ref[wrows, :], bias_ref[dr0, 0]),
                        (kc0_ref[...], vc0_ref[...], None)])
        o = o + attend(qb, [(k1_ref[wrows, :], v1_ref[wrows, :], bias_ref[dr0, 1]),
                            (kc1_ref[...], vc1_ref[...], None)])
        y_ref[0, qrows, :] = o * _silu(z_ref[0, qrows, :])
        return carry

    lax.fori_loop(0, n_rows, row, 0)

    if need_ctx:
        qb = (qc_ref[0] * scale).astype(BF16)
        o = attend(qb, [(kc0_ref[...], vc0_ref[...], None)])
        o = o + attend(qb, [(kc1_ref[...], vc1_ref[...], None)])
        yc_ref[0] = o * _silu(zc_ref[0])


def _natten_mixer(pc, pcc, bias, need_ctx):
    b, t, _ = pc.shape
    tc = pcc.shape[1]
    npair = W_C // LANES
    n_rows = t // GRID_W
    lat = lambda part: pl.BlockSpec((1, t, LANES), lambda i, p, part=part: (i, 0, part * npair + p))
    ctx = lambda part: pl.BlockSpec((1, tc, LANES), lambda i, p, part=part: (i, 0, part * npair + p))
    in_specs = [lat(0), lat(1), lat(2), lat(3), ctx(1), ctx(2),
                pl.BlockSpec((WIN_ROWS, 2, GRID_W, WIN_ROWS * GRID_W), lambda i, p: (0, p, 0, 0))]
    args = [pc, pc, pc, pc, pcc, pcc, bias]
    out_specs = [pl.BlockSpec((1, t, LANES), lambda i, p: (i, 0, p))]
    out_shape = [jax.ShapeDtypeStruct((b, t, W_C), F32)]
    if need_ctx:
        in_specs += [ctx(0), ctx(3)]
        args += [pcc, pcc]
        out_specs.append(pl.BlockSpec((1, tc, LANES), lambda i, p: (i, 0, p)))
        out_shape.append(jax.ShapeDtypeStruct((b, tc, W_C), F32))
    scratch = [pltpu.VMEM((t, LANES), BF16)] * 4 + [pltpu.VMEM((tc, LANES), BF16)] * 4
    res = pl.pallas_call(
        functools.partial(_natten_kernel, need_ctx=need_ctx, n_rows=n_rows),
        grid=(b, npair),
        in_specs=in_specs,
        out_specs=out_specs,
        out_shape=out_shape,
        scratch_shapes=scratch,
        compiler_params=_cparams("arbitrary", "arbitrary"),
        name="natten_mixer_ctx" if need_ctx else "natten_mixer",
    )(*args)
    return (res[0], res[1]) if need_ctx else (res[0], None)


def _natten_bias(rpb):
    col = np.arange(GRID_W)
    cs = np.clip(col - WIN_COLS // 2, 0, GRID_W - WIN_COLS)
    band = (col[None, :] >= cs[:, None]) & (col[None, :] < cs[:, None] + WIN_COLS)
    dc = np.clip(col[None, :] - col[:, None] + WIN_COLS - 1, 0, 2 * WIN_COLS - 2)
    tz = jnp.where(band[None, None], rpb.astype(F32)[:, :, dc], NEG)
    dr = np.arange(WIN_ROWS)[:, None] + np.arange(WIN_ROWS)[None, :]
    tab = tz[:, dr]
    tab = jnp.transpose(tab, (1, 0, 3, 2, 4))
    return tab.reshape(WIN_ROWS, H_C, GRID_W, WIN_ROWS * GRID_W)


def _outproj_kernel(ya_ref, yb_ref, yc_ref, x_ref, gt_ref, g_ref, wa_ref, wb_ref, wc_ref, o_ref):
    u = (_dot(ya_ref[0].astype(BF16), wa_ref[...]) + _dot(yb_ref[0].astype(BF16), wb_ref[...])
         + _dot(yc_ref[0].astype(BF16), wc_ref[...]))
    ms = jnp.mean(u * u, axis=-1, keepdims=True)
    o_ref[0] = x_ref[0] + gt_ref[0] * (u * lax.rsqrt(ms + EPS) * g_ref[...])


def _out_projection(ya, yb, yc, x, gt, g, wts, tm):
    b, t, d = x.shape
    wa, wb, wc = wts
    bm = gt.shape[0]
    mod_map = (lambda i, j: (i, 0, 0)) if bm > 1 else (lambda i, j: (0, 0, 0))
    const = lambda i, j: (0, 0)
    tile = lambda n: pl.BlockSpec((1, tm, n), lambda i, j: (i, j, 0))
    return pl.pallas_call(
        _outproj_kernel,
        grid=(b, t // tm),
        in_specs=[tile(ya.shape[2]), tile(yb.shape[2]), tile(yc.shape[2]), tile(d),
                  pl.BlockSpec((1, 1, d), mod_map),
                  pl.BlockSpec((1, d), const),
                  pl.BlockSpec(wa.shape, const), pl.BlockSpec(wb.shape, const), pl.BlockSpec(wc.shape, const)],
        out_specs=tile(d),
        out_shape=jax.ShapeDtypeStruct((b, t, d), F32),
        compiler_params=_cparams("arbitrary", "arbitrary"),
        name="out_projection",
    )(ya, yb, yc, x, gt, g.reshape(1, d), wa, wb, wc)


def _rope_lane(dd):
    q = DH_B // 4
    dd = np.asarray(dd)
    return np.where(dd < q, dd, np.where(dd < 2 * q, LANES // 2 + dd - q,
                    np.where(dd < 3 * q, dd - q, LANES // 2 + dd - 2 * q)))


def _column_maps():
    hw = H_B * LANES
    src = np.full((5 * hw,), P_IN, dtype=np.int32)
    dd = np.arange(DH_B)
    for part in range(5):
        lanes = _rope_lane(dd) if part < 2 else dd
        for h in range(H_B):
            src[part * hw + h * LANES + lanes] = OFF_B + part * W_B + h * DH_B + dd
    gsrc = np.full((H_B * LANES,), P_IN, dtype=np.int32)
    gtsrc = np.full((H_B * 8,), P_IN, dtype=np.int32)
    for h in range(H_B):
        for j in range(4):
            gsrc[h * LANES + j] = OFF_G + j * H_B + h
            gtsrc[h * 8 + j] = OFF_G + j * H_B + h
    return src, gsrc, gtsrc


def _rope_tables(t):
    pos = np.arange(t)
    q = DH_B // 4
    inv = ROPE_BASE ** (-jnp.arange(0, 2 * q, 2, dtype=F32) / (2 * q))
    ang_r = jnp.asarray(pos // GRID_W, F32)[:, None] * inv[None, :]
    ang_c = jnp.asarray(pos % GRID_W, F32)[:, None] * inv[None, :]
    zeros = jnp.zeros((t, LANES // 2 - 2 * q), F32)
    cos_half = jnp.concatenate([jnp.cos(ang_r), jnp.cos(ang_c), zeros], axis=1)
    sin_half = jnp.concatenate([jnp.sin(ang_r), jnp.sin(ang_c), zeros], axis=1)
    return (jnp.concatenate([cos_half, cos_half], axis=1),
            jnp.concatenate([-sin_half, sin_half], axis=1))


def _pad_heads(v):
    v = v.reshape(v.shape[:-1] + (H_B, DH_B))
    v = jnp.pad(v, [(0, 0)] * (v.ndim - 1) + [(0, LANES - DH_B)])
    return v.reshape(v.shape[:-2] + (H_B * LANES,))


def kernel(x, c, ctx, c_ctx, w_mod, b_mod, g_pre, g_post, w_in, w_out, hgrn_lb, hgrn_gn, mlstm_gate_b, mlstm_gn, na_rpb):
    depth = w_in.shape[0]
    b, t, d = x.shape
    tc = ctx.shape[1]
    assert t % CHUNK == 0 and tc % CHUNK == 0 and t // GRID_W >= WIN_ROWS and t % GRID_W == 0
    assert w_in.shape[2] == P_IN and (1 << N_LEVELS) == CHUNK

    lb_cum = jnp.cumsum(jax.nn.softmax(hgrn_lb.astype(F32), axis=0), axis=0)
    lb_all = lb_cum - lb_cum[0]
    src, gsrc, gtsrc = _column_maps()
    hgrn_tabs = _hgrn_tables()
    tris = _tri_tables()
    rope = _rope_tables(t)
    mod_rows = 16
    cc = jnp.concatenate([c.astype(F32), c_ctx.astype(F32)[None],
                          jnp.zeros((mod_rows - b - 1, d), F32)], axis=0)
    tm = min(256, t)
    tmc = min(256, tc)

    for l in range(depth):
        need_ctx = l < depth - 1
        mod = _modulation(cc, w_mod[l], b_mod[l])
        sh, sc, gt = (mod[:b, i * d:(i + 1) * d].reshape(b, 1, d) for i in range(3))
        shc, scc, gtc = (mod[b:b + 1, i * d:(i + 1) * d].reshape(1, 1, d) for i in range(3))

        w = w_in[l]
        w_ext = jnp.concatenate([w, jnp.zeros((d, 1), w.dtype)], axis=1)
        gb = mlstm_gate_b[l].astype(F32)
        gbr = jnp.pad(gb.T, ((0, 0), (0, LANES - 4))).reshape(1, H_B * LANES)
        gbc = jnp.pad(gb.T, ((0, 0), (0, 4))).reshape(H_B * 8, 1)
        wts = (w[:, :OFF_B].astype(BF16), w_ext[:, src].astype(BF16), w[:, OFF_C:].astype(BF16),
               w_ext[:, gsrc].astype(BF16), w_ext[:, gtsrc].T.astype(BF16), gbr, gbc)

        pa, pb, pc, pg, pgt = _in_projection(x, sh, sc, g_pre[l], wts, rope, tm)
        pac, pbc, pcc, pgc, pgtc = _in_projection(ctx, shc, scc, g_pre[l], wts, None, tmc)

        ya, yac = _hgrn_mixer(pa, pac, lb_all[l], hgrn_gn[l], hgrn_tabs, need_ctx)
        gnb = _pad_heads(mlstm_gn[l].astype(F32)).reshape(H_B, 1, LANES)
        yb, ybc = _mlstm_mixer(pb, pg, pgt, pbc, pgc, pgtc, gnb, tris, need_ctx)
        yc, ycc = _natten_mixer(pc, pcc, _natten_bias(na_rpb[l]), need_ctx)

        wo = w_out[l]
        wo_b = jnp.pad(wo[W_A:W_A + W_B].reshape(H_B, DH_B, d), ((0, 0), (0, LANES - DH_B), (0, 0)))
        wo_parts = (wo[:W_A].astype(BF16), wo_b.reshape(H_B * LANES, d).astype(BF16),
                    wo[W_A + W_B:].astype(BF16))
        x = _out_projection(ya, yb, yc, x, gt, g_post[l], wo_parts, tm)
        if need_ctx:
            ctx = _out_projection(yac, ybc, ycc, ctx, gtc, g_post[l], wo_parts, tmc)
    return x
```

```python
import functools

import numpy as np
import jax
import jax.numpy as jnp
from jax import lax
from jax.experimental import pallas as pl
from jax.experimental.pallas import tpu as pltpu

F32 = jnp.float32
BF16 = jnp.bfloat16

LANES = 128
GRID_W = 64
W_A, H_A, DH_A = 256, 4, 64
W_B, H_B, DH_B = 384, 4, 96
W_C, H_C, DH_C = 384, 6, 64
WIN_ROWS, WIN_COLS = 8, 16
ROPE_BASE = 10000.0
EPS = 1e-6
NEG = -1e30
CHUNK = 128
N_LEVELS = 7
NA_ROW_UNROLL = 4
OFF_B = 5 * W_A
OFF_G = OFF_B + 5 * W_B
OFF_C = OFF_G + 4 * H_B
P_IN = OFF_C + 4 * W_C
ONE_LANE = DH_B
VMEM_LIMIT = 56 * 1024 * 1024


def _cparams(*sem):
    return pltpu.CompilerParams(dimension_semantics=sem, vmem_limit_bytes=VMEM_LIMIT)


def _dot(a, b):
    return jnp.dot(a, b, preferred_element_type=F32)


def _dot_nt(a, b):
    return lax.dot_general(a, b, (((1,), (1,)), ((), ())), preferred_element_type=F32)


def _dot_tn(a, b):
    return lax.dot_general(a, b, (((0,), (0,)), ((), ())), preferred_element_type=F32)


def _split3(x):
    hi = x.astype(BF16)
    r = x - hi.astype(F32)
    mid = r.astype(BF16)
    lo = (r - mid.astype(F32)).astype(BF16)
    return hi, mid, lo


def _exact_left(t01, x):
    hi, mid, lo = _split3(x)
    return _dot(t01, hi) + _dot(t01, mid) + _dot(t01, lo)


def _exact_right(x, t01):
    hi, mid, lo = _split3(x)
    return _dot(hi, t01) + _dot(mid, t01) + _dot(lo, t01)


def _sigmoid(x):
    return 1.0 / (1.0 + jnp.exp(-x))


def _silu(x):
    return x * _sigmoid(x)


def _log_sigmoid(x):
    return jnp.minimum(x, 0.0) - jnp.log1p(jnp.exp(-jnp.abs(x)))


def _mod_kernel(c_ref, w_ref, b_ref, o_ref):
    s = _silu(c_ref[...])
    o_ref[...] = _dot(s.astype(BF16), w_ref[...].astype(BF16)) + b_ref[...]


def _modulation(cc, w, b):
    rows, d = cc.shape
    n = w.shape[1]
    tn = d
    assert n % tn == 0
    return pl.pallas_call(
        _mod_kernel,
        grid=(n // tn,),
        in_specs=[pl.BlockSpec((rows, d), lambda j: (0, 0)),
                  pl.BlockSpec((d, tn), lambda j: (0, j)),
                  pl.BlockSpec((1, tn), lambda j: (0, j))],
        out_specs=pl.BlockSpec((rows, tn), lambda j: (0, j)),
        out_shape=jax.ShapeDtypeStruct((rows, n), F32),
        compiler_params=_cparams("arbitrary"),
        name="modulation",
    )(cc, w, b.reshape(1, n))


def _inproj_kernel(*refs, rotary):
    if rotary:
        (x_ref, sh_ref, sc_ref, g_ref, wa_ref, wb_ref, wc_ref, wg_ref, wgt_ref, gbr_ref, gbc_ref,
         cos_ref, sin_ref, pa_ref, pb_ref, pc_ref, pg_ref, pgt_ref) = refs
    else:
        (x_ref, sh_ref, sc_ref, g_ref, wa_ref, wb_ref, wc_ref, wg_ref, wgt_ref, gbr_ref, gbc_ref,
         pa_ref, pb_ref, pc_ref, pg_ref, pgt_ref) = refs
    x = x_ref[0]
    ms = jnp.mean(x * x, axis=-1, keepdims=True)
    h = x * lax.rsqrt(ms + EPS) * g_ref[...]
    h = h * (1.0 + sc_ref[0]) + sh_ref[0]
    hb = h.astype(BF16)
    pa_ref[0] = _dot(hb, wa_ref[...])
    pc_ref[0] = _dot(hb, wc_ref[...])
    pg_ref[0] = _dot(hb, wg_ref[...]) + gbr_ref[...]
    pgt_ref[0] = _dot_nt(wgt_ref[...], hb) + gbc_ref[...]
    hw = H_B * LANES
    scale = DH_B ** -0.5
    for part in range(5):
        p = _dot(hb, wb_ref[:, part * hw:(part + 1) * hw])
        if part == 0:
            p = p * scale
        if rotary and part < 2:
            cos = cos_ref[...]
            sin = sin_ref[...]
            for hd in range(H_B):
                ph = p[:, hd * LANES:(hd + 1) * LANES]
                pb_ref[0, :, part * hw + hd * LANES:part * hw + (hd + 1) * LANES] = (
                    ph * cos + pltpu.roll(ph, LANES // 2, 1) * sin)
        else:
            pb_ref[0, :, part * hw:(part + 1) * hw] = p


def _in_projection(x, sh, sc, g, wts, rope, tm):
    b, t, d = x.shape
    wa, wb, wc, wg, wgt, gbr, gbc = wts
    bm = sh.shape[0]
    mod_map = (lambda i, j: (i, 0, 0)) if bm > 1 else (lambda i, j: (0, 0, 0))
    const = lambda i, j: (0, 0)
    in_specs = [pl.BlockSpec((1, tm, d), lambda i, j: (i, j, 0)),
                pl.BlockSpec((1, 1, d), mod_map),
                pl.BlockSpec((1, 1, d), mod_map),
                pl.BlockSpec((1, d), const),
                pl.BlockSpec(wa.shape, const),
                pl.BlockSpec(wb.shape, const),
                pl.BlockSpec(wc.shape, const),
                pl.BlockSpec(wg.shape, const),
                pl.BlockSpec(wgt.shape, const),
                pl.BlockSpec(gbr.shape, const),
                pl.BlockSpec(gbc.shape, const)]
    args = [x, sh, sc, g.reshape(1, d), wa, wb, wc, wg, wgt, gbr, gbc]
    if rope is not None:
        in_specs += [pl.BlockSpec((tm, LANES), lambda i, j: (j, 0))] * 2
        args += list(rope)
    na, nb, nc, ng = wa.shape[1], wb.shape[1], wc.shape[1], wg.shape[1]
    out_specs = [pl.BlockSpec((1, tm, na), lambda i, j: (i, j, 0)),
                 pl.BlockSpec((1, tm, nb), lambda i, j: (i, j, 0)),
                 pl.BlockSpec((1, tm, nc), lambda i, j: (i, j, 0)),
                 pl.BlockSpec((1, tm, ng), lambda i, j: (i, j, 0)),
                 pl.BlockSpec((1, wgt.shape[0], tm), lambda i, j: (i, 0, j))]
    out_shape = [jax.ShapeDtypeStruct((b, t, na), F32),
                 jax.ShapeDtypeStruct((b, t, nb), F32),
                 jax.ShapeDtypeStruct((b, t, nc), F32),
                 jax.ShapeDtypeStruct((b, t, ng), F32),
                 jax.ShapeDtypeStruct((b, wgt.shape[0], t), F32)]
    return pl.pallas_call(
        functools.partial(_inproj_kernel, rotary=rope is not None),
        grid=(b, t // tm),
        in_specs=in_specs,
        out_specs=out_specs,
        out_shape=out_shape,
        compiler_params=_cparams("arbitrary", "arbitrary"),
        name="in_projection_rope" if rope is not None else "in_projection",
    )(*args)


def _hgrn_tables():
    L, nl = CHUNK, N_LEVELS
    idx = np.arange(L)
    t = idx[:, None]
    u = idx[None, :]
    stacks, levels = [], []
    for rev in (False, True):
        mats = []
        for lv in range(nl):
            h = 1 << lv
            mid = (t // (2 * h)) * 2 * h + h
            if not rev:
                m = ((t // h) % 2 == 1) & (u >= mid) & (u <= t)
            else:
                m = ((t // h) % 2 == 0) & (u >= t) & (u <= mid - 1)
            mats.append(m)
        for lv in range(nl):
            h = 1 << lv
            mid = (t // (2 * h)) * 2 * h + h
            if not rev:
                m = ((t // h) % 2 == 0) & (u > t) & (u <= mid - 1)
            else:
                m = ((t // h) % 2 == 1) & (u >= mid) & (u < t)
            mats.append(m)
        mats.append((u <= t) if not rev else (u >= t))
        mats.append((u > t) if not rev else (u < t))
        stacks.append(np.concatenate(mats, axis=0).astype(np.float32))
        x = t ^ u
        lvl = np.where(x > 0, np.floor(np.log2(np.maximum(x, 1))), float(nl))
        valid = (u <= t) if not rev else (u >= t)
        lvl = np.where(valid, lvl, -1.0).astype(np.float32)
        levels.append(np.concatenate([lvl, lvl], axis=1))
    return jnp.asarray(np.stack(stacks), dtype=BF16), jnp.asarray(np.stack(levels), dtype=F32)


def _tri_tables():
    L = CHUNK
    idx = np.arange(L)
    t = idx[:, None]
    u = idx[None, :]
    fwd = (u <= t).astype(np.float32)
    bwd = (u >= t).astype(np.float32)
    tri = np.stack([fwd, bwd])
    trit = np.stack([fwd.T, bwd.T])
    return jnp.asarray(tri, dtype=BF16), jnp.asarray(trit, dtype=BF16)


def _hgrn_kernel(*refs, need_ctx, n_lat, n_ctx):
    (q_ref, ff_ref, fb_ref, i_ref, z_ref, qc_ref, ffc_ref, fbc_ref, ic_ref, zc_ref,
     lb_ref, gn_ref, tab_ref, lvl_ref) = refs[:14]
    if need_ctx:
        y_ref, yc_ref, of_ref, ofc_ref = refs[14:]
    else:
        y_ref, of_ref = refs[14:]
        yc_ref = ofc_ref = None
    L, nl = CHUNK, N_LEVELS
    lane = lax.broadcasted_iota(jnp.int32, (1, LANES), 1)
    hm0 = lane < DH_A
    row_h = lax.broadcasted_iota(jnp.int32, (LANES, LANES), 0) // DH_A
    col_h = lax.broadcasted_iota(jnp.int32, (LANES, LANES), 1) // DH_A
    same_head = row_h == col_h
    gn = gn_ref[0]

    def split(a):
        return jnp.concatenate([jnp.where(hm0, a, 0.0), jnp.where(hm0, 0.0, a)], axis=0).astype(BF16)

    def chunk(aq, fl, ai, st, d):
        lbd = lb_ref[d, 0]
        a0 = jnp.log(lbd)
        b0 = jnp.log1p(-lbd) + _log_sigmoid(fl)
        logf = jnp.maximum(a0, b0) + jnp.log1p(jnp.exp(-jnp.abs(a0 - b0)))
        k = (1.0 - lbd) * _sigmoid(-fl)
        q = _silu(aq) * (DH_A ** -0.5)
        e = _exact_left(tab_ref[d], logf)
        lvl = lvl_ref[d]
        a = jnp.zeros((L, 2 * L), F32)
        for lv in range(nl + 1):
            if lv < nl:
                qt = q * jnp.exp(e[lv * L:(lv + 1) * L])
                kt = k * jnp.exp(e[(nl + lv) * L:(nl + lv + 1) * L])
            else:
                qt, kt = q, k
            a = jnp.where(lvl == float(lv), _dot_nt(qt.astype(BF16), split(kt)), a)
        bcum = e[2 * nl * L:(2 * nl + 1) * L]
        rem = e[(2 * nl + 1) * L:(2 * nl + 2) * L]
        o = _dot(a.astype(BF16), split(ai))
        o = o + _dot_nt((q * jnp.exp(bcum)).astype(BF16), st.astype(BF16))
        ut = _dot_tn(ai.astype(BF16), (k * jnp.exp(rem)).astype(BF16))
        b_tot = bcum[L - 1:L] if d == 0 else bcum[0:1]
        st_new = st * jnp.exp(b_tot) + jnp.where(same_head, ut, 0.0)
        return o, st_new

    def finish(o, z):
        sq = o * o
        ms0 = jnp.sum(jnp.where(hm0, sq, 0.0), axis=-1, keepdims=True)
        ms1 = jnp.sum(jnp.where(hm0, 0.0, sq), axis=-1, keepdims=True)
        ms = jnp.where(hm0, ms0, ms1) * (1.0 / DH_A)
        return o * lax.rsqrt(ms + EPS) * gn * _silu(z)

    def scan(states, n, aq_ref, f_refs, ai_ref, zz_ref, o_scr, out_ref, emit):
        def body(i, states):
            new = []
            for d in range(2):
                c = i if d == 0 else n - 1 - i
                rows = pl.ds(pl.multiple_of(c * L, L), L)
                o, st = chunk(aq_ref[0, rows, :], f_refs[d][0, rows, :], ai_ref[0, rows, :], states[d], d)
                if emit and d == 0:
                    o_scr[rows, :] = o
                elif emit:
                    out_ref[0, rows, :] = o
                new.append(st)
            return tuple(new)
        states = lax.fori_loop(0, n, body, states)
        if emit:
            def fin(i, carry):
                rows = pl.ds(pl.multiple_of(i * L, L), L)
                out_ref[0, rows, :] = finish(o_scr[rows, :] + out_ref[0, rows, :], zz_ref[0, rows, :])
                return carry
            lax.fori_loop(0, n, fin, 0)
        return states

    st0 = jnp.zeros((LANES, LANES), F32)
    states = scan((st0, st0), n_ctx, qc_ref, (ffc_ref, fbc_ref), ic_ref, zc_ref, ofc_ref, yc_ref, need_ctx)
    scan(states, n_lat, q_ref, (ff_ref, fb_ref), i_ref, z_ref, of_ref, y_ref, True)


def _hgrn_mixer(pa, pac, lb, gn, tabs, need_ctx):
    b, t, _ = pa.shape
    tc = pac.shape[1]
    tab, lvl = tabs
    npair = W_A // LANES
    lat = lambda part: pl.BlockSpec((1, t, LANES), lambda i, p, part=part: (i, 0, part * npair + p))
    ctx = lambda part: pl.BlockSpec((1, tc, LANES), lambda i, p, part=part: (i, 0, part * npair + p))
    in_specs = ([lat(0), lat(1), lat(2), lat(3), lat(4), ctx(0), ctx(1), ctx(2), ctx(3), ctx(4),
                 pl.BlockSpec((2, 1, 1, LANES), lambda i, p: (0, p, 0, 0)),
                 pl.BlockSpec((1, 1, LANES), lambda i, p: (p, 0, 0)),
                 pl.BlockSpec(tab.shape, lambda i, p: (0, 0, 0)),
                 pl.BlockSpec(lvl.shape, lambda i, p: (0, 0, 0))])
    out_specs = [pl.BlockSpec((1, t, LANES), lambda i, p: (i, 0, p))]
    out_shape = [jax.ShapeDtypeStruct((b, t, W_A), F32)]
    scratch = [pltpu.VMEM((t, LANES), F32)]
    if need_ctx:
        out_specs.append(pl.BlockSpec((1, tc, LANES), lambda i, p: (i, 0, p)))
        out_shape.append(jax.ShapeDtypeStruct((b, tc, W_A), F32))
        scratch.append(pltpu.VMEM((tc, LANES), F32))
    res = pl.pallas_call(
        functools.partial(_hgrn_kernel, need_ctx=need_ctx, n_lat=t // CHUNK, n_ctx=tc // CHUNK),
        grid=(b, npair),
        in_specs=in_specs,
        out_specs=out_specs,
        out_shape=out_shape,
        scratch_shapes=scratch,
        compiler_params=_cparams("arbitrary", "arbitrary"),
        name="hgrn2_mixer_ctx" if need_ctx else "hgrn2_mixer",
    )(pa, pa, pa, pa, pa, pac, pac, pac, pac, pac,
      lb.reshape(2, npair, 1, LANES), gn.reshape(npair, 1, LANES), tab, lvl)
    return (res[0], res[1]) if need_ctx else (res[0], None)


def _mlstm_kernel(*refs, need_ctx, n_lat, n_ctx):
    (q_ref, k_ref, v_ref, o_ref, z_ref, qc_ref, kc_ref, vc_ref, oc_ref, zc_ref,
     g_ref, gt_ref, gc_ref, gtc_ref, gn_ref, tri_ref, trit_ref) = refs[:17]
    if need_ctx:
        y_ref, yc_ref, hf_ref, hfc_ref = refs[17:]
    else:
        y_ref, hf_ref = refs[17:]
        yc_ref = hfc_ref = None
    L = CHUNK
    lane = lax.broadcasted_iota(jnp.int32, (1, LANES), 1)
    real = lane < DH_B
    rows_i = lax.broadcasted_iota(jnp.int32, (L, L), 0)
    cols_i = lax.broadcasted_iota(jnp.int32, (L, L), 1)
    gn = gn_ref[0]

    def chunk(q, k, v, gcol, grow, state, d, emit):
        c_prev, m_prev = state
        ig_col = gcol[:, d:d + 1]
        b_col = _exact_left(tri_ref[d], _log_sigmoid(gcol))[:, 2 + d:3 + d]
        ig_row = grow[d:d + 1, :]
        b_row = _exact_right(_log_sigmoid(grow), trit_ref[d])[2 + d:3 + d, :]
        b_tot = b_col[L - 1:L] if d == 0 else b_col[0:1]
        qb = q.astype(BF16)
        kb = k.astype(BF16)
        va = jnp.where(lane == ONE_LANE, 1.0, v).astype(BF16)
        h = None
        if emit:
            valid = (cols_i <= rows_i) if d == 0 else (cols_i >= rows_i)
            dm = jnp.where(valid, b_col - b_row + ig_row, NEG)
            inter = b_col + m_prev
            m_t = jnp.maximum(jnp.max(dm, axis=-1, keepdims=True), inter)
            p = _dot_nt(qb, kb) * jnp.exp(dm - m_t)
            e_in = jnp.exp(inter - m_t)
            qc = _dot(qb, c_prev.astype(BF16))
            num = _dot(p.astype(BF16), va) + e_in * qc
            den = jnp.sum(p, axis=-1, keepdims=True) + e_in * qc[:, ONE_LANE:ONE_LANE + 1]
            h = num / jnp.maximum(jnp.abs(den), jnp.exp(-m_t))
            h = jnp.where(real, h, 0.0)
        w = b_tot + ig_col - b_col
        m_new = jnp.maximum(b_tot + m_prev, jnp.max(w, axis=0, keepdims=True))
        kw = (k * jnp.exp(w - m_new)).astype(BF16)
        c_new = jnp.exp(b_tot + m_prev - m_new) * c_prev + _dot_tn(kw, va)
        return h, (c_new, m_new)

    def finish(h, og, z):
        ms = jnp.sum(h * h, axis=-1, keepdims=True) * (1.0 / DH_B)
        return _sigmoid(og) * (h * lax.rsqrt(ms + EPS) * gn) * _silu(z)

    def scan(states, n, qq, kk, vv, oo, zz, gg, ggt, h_scr, out_ref, emit):
        def body(i, states):
            new = []
            for d in range(2):
                c = i if d == 0 else n - 1 - i
                rows = pl.ds(pl.multiple_of(c * L, L), L)
                h, st = chunk(qq[0, rows, :], kk[0, rows, :], vv[0, rows, :], gg[0, rows, :],
                              ggt[0, :, rows], states[d], d, emit)
                if emit and d == 0:
                    h_scr[rows, :] = h
                elif emit:
                    out_ref[0, rows, :] = h
                new.append(st)
            return tuple(new)
        states = lax.fori_loop(0, n, body, states)
        if emit:
            def fin(i, carry):
                rows = pl.ds(pl.multiple_of(i * L, L), L)
                out_ref[0, rows, :] = finish(h_scr[rows, :] + out_ref[0, rows, :],
                                             oo[0, rows, :], zz[0, rows, :])
                return carry
            lax.fori_loop(0, n, fin, 0)
        return states

    s0 = (jnp.zeros((LANES, LANES), F32), jnp.zeros((1, 1), F32))
    states = scan((s0, s0), n_ctx, qc_ref, kc_ref, vc_ref, oc_ref, zc_ref, gc_ref, gtc_ref,
                  hfc_ref, yc_ref, need_ctx)
    scan(states, n_lat, q_ref, k_ref, v_ref, o_ref, z_ref, g_ref, gt_ref, hf_ref, y_ref, True)


def _mlstm_mixer(pb, pg, pgt, pbc, pgc, pgtc, gn, tris, need_ctx):
    b, t, _ = pb.shape
    tc = pbc.shape[1]
    tri, trit = tris
    lat = lambda part: pl.BlockSpec((1, t, LANES), lambda i, h, part=part: (i, 0, part * H_B + h))
    ctx = lambda part: pl.BlockSpec((1, tc, LANES), lambda i, h, part=part: (i, 0, part * H_B + h))
    in_specs = ([lat(p) for p in range(5)] + [ctx(p) for p in range(5)] +
                [pl.BlockSpec((1, t, LANES), lambda i, h: (i, 0, h)),
                 pl.BlockSpec((1, 8, t), lambda i, h: (i, h, 0)),
                 pl.BlockSpec((1, tc, LANES), lambda i, h: (i, 0, h)),
                 pl.BlockSpec((1, 8, tc), lambda i, h: (i, h, 0)),
                 pl.BlockSpec((1, 1, LANES), lambda i, h: (h, 0, 0)),
                 pl.BlockSpec(tri.shape, lambda i, h: (0, 0, 0)),
                 pl.BlockSpec(trit.shape, lambda i, h: (0, 0, 0))])
    out_specs = [pl.BlockSpec((1, t, LANES), lambda i, h: (i, 0, h))]
    out_shape = [jax.ShapeDtypeStruct((b, t, H_B * LANES), F32)]
    scratch = [pltpu.VMEM((t, LANES), F32)]
    if need_ctx:
        out_specs.append(pl.BlockSpec((1, tc, LANES), lambda i, h: (i, 0, h)))
        out_shape.append(jax.ShapeDtypeStruct((b, tc, H_B * LANES), F32))
        scratch.append(pltpu.VMEM((tc, LANES), F32))
    res = pl.pallas_call(
        functools.partial(_mlstm_kernel, need_ctx=need_ctx, n_lat=t // CHUNK, n_ctx=tc // CHUNK),
        grid=(b, H_B),
        in_specs=in_specs,
        out_specs=out_specs,
        out_shape=out_shape,
        scratch_shapes=scratch,
        compiler_params=_cparams("arbitrary", "arbitrary"),
        name="mlstm_mixer_ctx" if need_ctx else "mlstm_mixer",
    )(pb, pb, pb, pb, pb, pbc, pbc, pbc, pbc, pbc, pg, pgt, pgc, pgtc, gn, tri, trit)
    return (res[0], res[1]) if need_ctx else (res[0], None)


def _natten_kernel(*refs, need_ctx, n_rows):
    if need_ctx:
        (q_ref, k_ref, v_ref, z_ref, kc_ref, vc_ref, bias_ref, qc_ref, zc_ref, y_ref, yc_ref,
         kb_ref, vt_ref, kcb_ref, vct_ref) = refs
    else:
        (q_ref, k_ref, v_ref, z_ref, kc_ref, vc_ref, bias_ref, y_ref,
         kb_ref, vt_ref, kcb_ref, vct_ref) = refs
    lane = lax.broadcasted_iota(jnp.int32, (1, LANES), 1)
    hm0 = lane < DH_C
    scale = DH_C ** -0.5
    win = WIN_ROWS * GRID_W
    t = k_ref.shape[1]
    tc = kc_ref.shape[1]
    blk = LANES

    def stage(i, carry):
        rows = pl.ds(pl.multiple_of(i * blk, blk), blk)
        kb_ref[rows, :] = k_ref[0, rows, :].astype(BF16)
        vt_ref[0, :, rows] = v_ref[0, rows, :].T.astype(BF16)
        return carry

    lax.fori_loop(0, t // blk, stage, 0)

    def stage_shifted(i, carry):
        start = pl.multiple_of(i * blk, blk)
        src = pl.ds(pl.multiple_of(start + GRID_W, GRID_W), blk)
        vt_ref[1, :, pl.ds(start, blk)] = v_ref[0, src, :].T.astype(BF16)
        return carry

    lax.fori_loop(0, t // blk - 1, stage_shifted, 0)
    last = v_ref[0, t - blk:t, :].T
    vt_ref[1, :, t - blk:t] = pltpu.roll(last, GRID_W, 1).astype(BF16)

    def stage_ctx(i, carry):
        rows = pl.ds(pl.multiple_of(i * blk, blk), blk)
        kcb_ref[rows, :] = kc_ref[0, rows, :].astype(BF16)
        vct_ref[:, rows] = vc_ref[0, rows, :].T.astype(BF16)
        return carry

    lax.fori_loop(0, tc // blk, stage_ctx, 0)

    def attend(q, keys, vals_t, bias):
        n = q.shape[0]
        q2 = jnp.concatenate([jnp.where(hm0, q, 0.0), jnp.where(hm0, 0.0, q)], axis=0).astype(BF16)
        lt = _dot_nt(keys, q2)
        if bias is not None:
            mb = bias.shape[0]
            lt = jnp.concatenate([lt[:mb] + bias, lt[mb:]], axis=0)
        p = jnp.exp(lt - jnp.max(lt, axis=0, keepdims=True))
        den = jnp.sum(p, axis=0, keepdims=True)
        ot = _dot(vals_t, p.astype(BF16)) / den
        o = ot.T
        return jnp.where(hm0, o[:n], o[n:])

    def row(r, carry):
        rs = jnp.clip(r - WIN_ROWS // 2, 0, n_rows - WIN_ROWS)
        dr0 = rs - r + (WIN_ROWS - 1)
        par = rs % 2
        qrows = pl.ds(pl.multiple_of(r * GRID_W, GRID_W), GRID_W)
        wrows = pl.ds(pl.multiple_of(rs * GRID_W, GRID_W), win)
        wlanes = pl.ds(pl.multiple_of((rs - par) * GRID_W, LANES), win)
        keys = jnp.concatenate([kb_ref[wrows, :], kcb_ref[...]], axis=0)
        vals_t = jnp.concatenate([vt_ref[par, :, wlanes], vct_ref[...]], axis=1)
        o = attend(q_ref[0, qrows, :] * scale, keys, vals_t, bias_ref[dr0, 0])
        y_ref[0, qrows, :] = o * _silu(z_ref[0, qrows, :])
        return carry

    lax.fori_loop(0, n_rows, row, 0, unroll=NA_ROW_UNROLL)

    if need_ctx:
        o = attend(qc_ref[0] * scale, kcb_ref[...], vct_ref[...], None)
        yc_ref[0] = o * _silu(zc_ref[0])


def _natten_mixer(pc, pcc, bias, need_ctx):
    b, t, _ = pc.shape
    tc = pcc.shape[1]
    npair = W_C // LANES
    n_rows = t // GRID_W
    lat = lambda part: pl.BlockSpec((1, t, LANES), lambda i, p, part=part: (i, 0, part * npair + p))
    ctx = lambda part: pl.BlockSpec((1, tc, LANES), lambda i, p, part=part: (i, 0, part * npair + p))
    in_specs = [lat(0), lat(1), lat(2), lat(3), ctx(1), ctx(2),
                pl.BlockSpec((WIN_ROWS, 1, WIN_ROWS * GRID_W, LANES), lambda i, p: (0, p, 0, 0))]
    args = [pc, pc, pc, pc, pcc, pcc, bias]
    out_specs = [pl.BlockSpec((1, t, LANES), lambda i, p: (i, 0, p))]
    out_shape = [jax.ShapeDtypeStruct((b, t, W_C), F32)]
    if need_ctx:
        in_specs += [ctx(0), ctx(3)]
        args += [pcc, pcc]
        out_specs.append(pl.BlockSpec((1, tc, LANES), lambda i, p: (i, 0, p)))
        out_shape.append(jax.ShapeDtypeStruct((b, tc, W_C), F32))
    scratch = [pltpu.VMEM((t, LANES), BF16), pltpu.VMEM((2, LANES, t), BF16),
               pltpu.VMEM((tc, LANES), BF16), pltpu.VMEM((LANES, tc), BF16)]
    res = pl.pallas_call(
        functools.partial(_natten_kernel, need_ctx=need_ctx, n_rows=n_rows),
        grid=(b, npair),
        in_specs=in_specs,
        out_specs=out_specs,
        out_shape=out_shape,
        scratch_shapes=scratch,
        compiler_params=_cparams("arbitrary", "arbitrary"),
        name="natten_mixer_ctx" if need_ctx else "natten_mixer",
    )(*args)
    return (res[0], res[1]) if need_ctx else (res[0], None)


def _natten_bias(rpb):
    col = np.arange(GRID_W)
    cs = np.clip(col - WIN_COLS // 2, 0, GRID_W - WIN_COLS)
    band = (col[None, :] >= cs[:, None]) & (col[None, :] < cs[:, None] + WIN_COLS)
    dc = np.clip(col[None, :] - col[:, None] + WIN_COLS - 1, 0, 2 * WIN_COLS - 2)
    tz = jnp.where(band[None, None], rpb.astype(F32)[:, :, dc], NEG)
    dr = np.arange(WIN_ROWS)[:, None] + np.arange(WIN_ROWS)[None, :]
    tab = tz[:, dr]
    tab = tab.reshape(H_C // 2, 2, WIN_ROWS, WIN_ROWS, GRID_W, GRID_W)
    tab = jnp.transpose(tab, (2, 0, 3, 5, 1, 4))
    return tab.reshape(WIN_ROWS, H_C // 2, WIN_ROWS * GRID_W, 2 * GRID_W)


def _outproj_kernel(ya_ref, yb_ref, yc_ref, x_ref, gt_ref, g_ref, wa_ref, wb_ref, wc_ref, o_ref):
    u = (_dot(ya_ref[0].astype(BF16), wa_ref[...]) + _dot(yb_ref[0].astype(BF16), wb_ref[...])
         + _dot(yc_ref[0].astype(BF16), wc_ref[...]))
    ms = jnp.mean(u * u, axis=-1, keepdims=True)
    o_ref[0] = x_ref[0] + gt_ref[0] * (u * lax.rsqrt(ms + EPS) * g_ref[...])


def _out_projection(ya, yb, yc, x, gt, g, wts, tm):
    b, t, d = x.shape
    wa, wb, wc = wts
    bm = gt.shape[0]
    mod_map = (lambda i, j: (i, 0, 0)) if bm > 1 else (lambda i, j: (0, 0, 0))
    const = lambda i, j: (0, 0)
    tile = lambda n: pl.BlockSpec((1, tm, n), lambda i, j: (i, j, 0))
    return pl.pallas_call(
        _outproj_kernel,
        grid=(b, t // tm),
        in_specs=[tile(ya.shape[2]), tile(yb.shape[2]), tile(yc.shape[2]), tile(d),
                  pl.BlockSpec((1, 1, d), mod_map),
                  pl.BlockSpec((1, d), const),
                  pl.BlockSpec(wa.shape, const), pl.BlockSpec(wb.shape, const), pl.BlockSpec(wc.shape, const)],
        out_specs=tile(d),
        out_shape=jax.ShapeDtypeStruct((b, t, d), F32),
        compiler_params=_cparams("arbitrary", "arbitrary"),
        name="out_projection",
    )(ya, yb, yc, x, gt, g.reshape(1, d), wa, wb, wc)


def _rope_lane(dd):
    q = DH_B // 4
    dd = np.asarray(dd)
    return np.where(dd < q, dd, np.where(dd < 2 * q, LANES // 2 + dd - q,
                    np.where(dd < 3 * q, dd - q, LANES // 2 + dd - 2 * q)))


def _column_maps():
    hw = H_B * LANES
    src = np.full((5 * hw,), P_IN, dtype=np.int32)
    dd = np.arange(DH_B)
    for part in range(5):
        lanes = _rope_lane(dd) if part < 2 else dd
        for h in range(H_B):
            src[part * hw + h * LANES + lanes] = OFF_B + part * W_B + h * DH_B + dd
    gsrc = np.full((H_B * LANES,), P_IN, dtype=np.int32)
    gtsrc = np.full((H_B * 8,), P_IN, dtype=np.int32)
    for h in range(H_B):
        for j in range(4):
            gsrc[h * LANES + j] = OFF_G + j * H_B + h
            gtsrc[h * 8 + j] = OFF_G + j * H_B + h
    return src, gsrc, gtsrc


def _rope_tables(t):
    pos = np.arange(t)
    q = DH_B // 4
    inv = ROPE_BASE ** (-jnp.arange(0, 2 * q, 2, dtype=F32) / (2 * q))
    ang_r = jnp.asarray(pos // GRID_W, F32)[:, None] * inv[None, :]
    ang_c = jnp.asarray(pos % GRID_W, F32)[:, None] * inv[None, :]
    zeros = jnp.zeros((t, LANES // 2 - 2 * q), F32)
    cos_half = jnp.concatenate([jnp.cos(ang_r), jnp.cos(ang_c), zeros], axis=1)
    sin_half = jnp.concatenate([jnp.sin(ang_r), jnp.sin(ang_c), zeros], axis=1)
    return (jnp.concatenate([cos_half, cos_half], axis=1),
            jnp.concatenate([-sin_half, sin_half], axis=1))


def _pad_heads(v):
    v = v.reshape(v.shape[:-1] + (H_B, DH_B))
    v = jnp.pad(v, [(0, 0)] * (v.ndim - 1) + [(0, LANES - DH_B)])
    return v.reshape(v.shape[:-2] + (H_B * LANES,))


def kernel(x, c, ctx, c_ctx, w_mod, b_mod, g_pre, g_post, w_in, w_out, hgrn_lb, hgrn_gn, mlstm_gate_b, mlstm_gn, na_rpb):
    depth = w_in.shape[0]
    b, t, d = x.shape
    tc = ctx.shape[1]
    assert t % CHUNK == 0 and tc % CHUNK == 0 and t // GRID_W >= WIN_ROWS and t % GRID_W == 0
    assert w_in.shape[2] == P_IN and (1 << N_LEVELS) == CHUNK

    lb_cum = jnp.cumsum(jax.nn.softmax(hgrn_lb.astype(F32), axis=0), axis=0)
    lb_all = lb_cum - lb_cum[0]
    src, gsrc, gtsrc = _column_maps()
    hgrn_tabs = _hgrn_tables()
    tris = _tri_tables()
    rope = _rope_tables(t)
    mod_rows = 16
    cc = jnp.concatenate([c.astype(F32), c_ctx.astype(F32)[None],
                          jnp.zeros((mod_rows - b - 1, d), F32)], axis=0)
    tm = min(256, t)
    tmc = min(256, tc)

    for l in range(depth):
        need_ctx = l < depth - 1
        mod = _modulation(cc, w_mod[l], b_mod[l])
        sh, sc, gt = (mod[:b, i * d:(i + 1) * d].reshape(b, 1, d) for i in range(3))
        shc, scc, gtc = (mod[b:b + 1, i * d:(i + 1) * d].reshape(1, 1, d) for i in range(3))

        w = w_in[l]
        w_ext = jnp.concatenate([w, jnp.zeros((d, 1), w.dtype)], axis=1)
        gb = mlstm_gate_b[l].astype(F32)
        gbr = jnp.pad(gb.T, ((0, 0), (0, LANES - 4))).reshape(1, H_B * LANES)
        gbc = jnp.pad(gb.T, ((0, 0), (0, 4))).reshape(H_B * 8, 1)
        wts = (w[:, :OFF_B].astype(BF16), w_ext[:, src].astype(BF16), w[:, OFF_C:].astype(BF16),
               w_ext[:, gsrc].astype(BF16), w_ext[:, gtsrc].T.astype(BF16), gbr, gbc)

        pa, pb, pc, pg, pgt = _in_projection(x, sh, sc, g_pre[l], wts, rope, tm)
        pac, pbc, pcc, pgc, pgtc = _in_projection(ctx, shc, scc, g_pre[l], wts, None, tmc)

        ya, yac = _hgrn_mixer(pa, pac, lb_all[l], hgrn_gn[l], hgrn_tabs, need_ctx)
        gnb = _pad_heads(mlstm_gn[l].astype(F32)).reshape(H_B, 1, LANES)
        yb, ybc = _mlstm_mixer(pb, pg, pgt, pbc, pgc, pgtc, gnb, tris, need_ctx)
        yc, ycc = _natten_mixer(pc, pcc, _natten_bias(na_rpb[l]), need_ctx)

        wo = w_out[l]
        wo_b = jnp.pad(wo[W_A:W_A + W_B].reshape(H_B, DH_B, d), ((0, 0), (0, LANES - DH_B), (0, 0)))
        wo_parts = (wo[:W_A].astype(BF16), wo_b.reshape(H_B * LANES, d).astype(BF16),
                    wo[W_A + W_B:].astype(BF16))
        x = _out_projection(ya, yb, yc, x, gt, g_post[l], wo_parts, tm)
        if need_ctx:
            ctx = _out_projection(yac, ybc, ycc, ctx, gtc, g_post[l], wo_parts, tmc)
    return x
```

```python
import functools
import math

import numpy as np
import jax
import jax.numpy as jnp
from jax import lax
from jax.experimental import pallas as pl
from jax.experimental.pallas import tpu as pltpu

F32 = jnp.float32
BF16 = jnp.bfloat16

LANES = 128
GRID_W = 64
W_A, H_A, DH_A = 256, 4, 64
W_B, H_B, DH_B = 384, 4, 96
W_C, H_C, DH_C = 384, 6, 64
WIN_ROWS, WIN_COLS = 8, 16
ROPE_BASE = 10000.0
EPS = 1e-6
NEG = -1e30
LOG2_E = math.log2(math.e)
CHUNK = 128
N_LEVELS = 7
NA_ROW_UNROLL = 4
CHUNK_UNROLL = 4
OFF_B = 5 * W_A
OFF_G = OFF_B + 5 * W_B
OFF_C = OFF_G + 4 * H_B
P_IN = OFF_C + 4 * W_C
ONE_LANE = DH_B
VMEM_LIMIT = 56 * 1024 * 1024


def _cparams(*sem):
    return pltpu.CompilerParams(dimension_semantics=sem, vmem_limit_bytes=VMEM_LIMIT)


def _dot(a, b):
    return jnp.dot(a, b, preferred_element_type=F32)


def _dot_nt(a, b):
    return lax.dot_general(a, b, (((1,), (1,)), ((), ())), preferred_element_type=F32)


def _dot_tn(a, b):
    return lax.dot_general(a, b, (((0,), (0,)), ((), ())), preferred_element_type=F32)


def _chunk_loop(n, body):
    u = math.gcd(n, CHUNK_UNROLL)

    def step(i, carry):
        for j in range(u):
            body(i * u + j, carry)
        return carry

    return lax.fori_loop(0, n // u, step, 0)


def _grouped_loop(n, body):
    u = math.gcd(n, CHUNK_UNROLL)

    def step(i, carry):
        body([i * u + j for j in range(u)])
        return carry

    return lax.fori_loop(0, n // u, step, 0)


def _split3(x):
    hi = x.astype(BF16)
    r = x - hi.astype(F32)
    mid = r.astype(BF16)
    lo = (r - mid.astype(F32)).astype(BF16)
    return hi, mid, lo


def _exact_left(t01, x):
    n = x.shape[1]
    r = _dot(t01, jnp.concatenate(_split3(x), axis=1))
    return r[:, :n] + r[:, n:2 * n] + r[:, 2 * n:]


def _exact_right(x, t01):
    m = x.shape[0]
    pieces = jnp.concatenate([p.astype(F32) for p in _split3(x)], axis=0).astype(BF16)
    r = _dot(pieces, t01)
    return r[:m] + r[m:2 * m] + r[2 * m:]


def _sigmoid(x):
    return 1.0 / (1.0 + jnp.exp(-x))


def _silu(x):
    return x * _sigmoid(x)


def _log_sigmoid(x):
    return jnp.minimum(x, 0.0) - jnp.log(1.0 + jnp.exp(-jnp.abs(x)))


def _mod_kernel(c_ref, w_ref, b_ref, o_ref):
    s = _silu(c_ref[...])
    o_ref[...] = _dot(s.astype(BF16), w_ref[...].astype(BF16)) + b_ref[...]


def _modulation(cc, w, b):
    rows, d = cc.shape
    n = w.shape[1]
    tn = d
    assert n % tn == 0
    return pl.pallas_call(
        _mod_kernel,
        grid=(n // tn,),
        in_specs=[pl.BlockSpec((rows, d), lambda j: (0, 0)),
                  pl.BlockSpec((d, tn), lambda j: (0, j)),
                  pl.BlockSpec((1, tn), lambda j: (0, j))],
        out_specs=pl.BlockSpec((rows, tn), lambda j: (0, j)),
        out_shape=jax.ShapeDtypeStruct((rows, n), F32),
        compiler_params=_cparams("arbitrary"),
        name="modulation",
    )(cc, w, b.reshape(1, n))


def _inproj_kernel(*refs, rotary):
    if rotary:
        (x_ref, sh_ref, sc_ref, g_ref, wa_ref, wb_ref, wc_ref, wg_ref, wgt_ref, gbr_ref, gbc_ref,
         cos_ref, sin_ref, pa_ref, pb_ref, pc_ref, pg_ref, pgt_ref) = refs
    else:
        (x_ref, sh_ref, sc_ref, g_ref, wa_ref, wb_ref, wc_ref, wg_ref, wgt_ref, gbr_ref, gbc_ref,
         pa_ref, pb_ref, pc_ref, pg_ref, pgt_ref) = refs
    x = x_ref[0]
    ms = jnp.mean(x * x, axis=-1, keepdims=True)
    h = x * lax.rsqrt(ms + EPS) * g_ref[...]
    h = h * (1.0 + sc_ref[0]) + sh_ref[0]
    hb = h.astype(BF16)
    pa_ref[0] = _dot(hb, wa_ref[...])
    pc_ref[0] = _dot(hb, wc_ref[...])
    pg_ref[0] = _dot(hb, wg_ref[...]) + gbr_ref[...]
    pgt_ref[0] = _dot_nt(wgt_ref[...], hb) + gbc_ref[...]
    hw = H_B * LANES
    scale = DH_B ** -0.5
    for part in range(5):
        p = _dot(hb, wb_ref[:, part * hw:(part + 1) * hw])
        if part == 0:
            p = p * scale
        if rotary and part < 2:
            cos = cos_ref[...]
            sin = sin_ref[...]
            for hd in range(H_B):
                ph = p[:, hd * LANES:(hd + 1) * LANES]
                pb_ref[0, :, part * hw + hd * LANES:part * hw + (hd + 1) * LANES] = (
                    ph * cos + pltpu.roll(ph, LANES // 2, 1) * sin)
        else:
            pb_ref[0, :, part * hw:(part + 1) * hw] = p


def _in_projection(x, sh, sc, g, wts, rope, tm):
    b, t, d = x.shape
    wa, wb, wc, wg, wgt, gbr, gbc = wts
    bm = sh.shape[0]
    mod_map = (lambda i, j: (i, 0, 0)) if bm > 1 else (lambda i, j: (0, 0, 0))
    const = lambda i, j: (0, 0)
    in_specs = [pl.BlockSpec((1, tm, d), lambda i, j: (i, j, 0)),
                pl.BlockSpec((1, 1, d), mod_map),
                pl.BlockSpec((1, 1, d), mod_map),
                pl.BlockSpec((1, d), const),
                pl.BlockSpec(wa.shape, const),
                pl.BlockSpec(wb.shape, const),
                pl.BlockSpec(wc.shape, const),
                pl.BlockSpec(wg.shape, const),
                pl.BlockSpec(wgt.shape, const),
                pl.BlockSpec(gbr.shape, const),
                pl.BlockSpec(gbc.shape, const)]
    args = [x, sh, sc, g.reshape(1, d), wa, wb, wc, wg, wgt, gbr, gbc]
    if rope is not None:
        in_specs += [pl.BlockSpec((tm, LANES), lambda i, j: (j, 0))] * 2
        args += list(rope)
    na, nb, nc, ng = wa.shape[1], wb.shape[1], wc.shape[1], wg.shape[1]
    out_specs = [pl.BlockSpec((1, tm, na), lambda i, j: (i, j, 0)),
                 pl.BlockSpec((1, tm, nb), lambda i, j: (i, j, 0)),
                 pl.BlockSpec((1, tm, nc), lambda i, j: (i, j, 0)),
                 pl.BlockSpec((1, tm, ng), lambda i, j: (i, j, 0)),
                 pl.BlockSpec((1, wgt.shape[0], tm), lambda i, j: (i, 0, j))]
    out_shape = [jax.ShapeDtypeStruct((b, t, na), F32),
                 jax.ShapeDtypeStruct((b, t, nb), F32),
                 jax.ShapeDtypeStruct((b, t, nc), F32),
                 jax.ShapeDtypeStruct((b, t, ng), F32),
                 jax.ShapeDtypeStruct((b, wgt.shape[0], t), F32)]
    return pl.pallas_call(
        functools.partial(_inproj_kernel, rotary=rope is not None),
        grid=(b, t // tm),
        in_specs=in_specs,
        out_specs=out_specs,
        out_shape=out_shape,
        compiler_params=_cparams("arbitrary", "arbitrary"),
        name="in_projection_rope" if rope is not None else "in_projection",
    )(*args)


def _hgrn_tables():
    L, nl = CHUNK, N_LEVELS
    idx = np.arange(L)
    t = idx[:, None]
    u = idx[None, :]
    tabq, tabk, tabr, levels = [], [], [], []
    for rev in (False, True):
        ks = []
        for lv in range(nl):
            h = 1 << lv
            mid = (t // (2 * h)) * 2 * h + h
            if not rev:
                m = ((t // h) % 2 == 0) & (u > t) & (u <= mid - 1)
            else:
                m = ((t // h) % 2 == 1) & (u >= mid) & (u < t)
            ks.append(m.T)
        tabq.append(((u <= t) if not rev else (u >= t)).astype(np.float32))
        tabk.append(np.concatenate(ks, axis=1).astype(np.float32))
        tabr.append(((u > t) if not rev else (u < t)).T.astype(np.float32))
        x = t ^ u
        lvl = np.where(x > 0, np.floor(np.log2(np.maximum(x, 1))), float(nl))
        valid = (u <= t) if not rev else (u >= t)
        lvl = np.where(valid, lvl, -1.0).astype(np.float32)
        levels.append(np.concatenate([lvl, lvl], axis=1))
    as_bf16 = lambda a: jnp.asarray(np.stack(a), dtype=BF16)
    return as_bf16(tabq), as_bf16(tabk), as_bf16(tabr), jnp.asarray(np.stack(levels), dtype=F32)


def _tri_tables():
    L = CHUNK
    idx = np.arange(L)
    t = idx[:, None]
    u = idx[None, :]
    fwd = (u <= t).astype(np.float32)
    bwd = (u >= t).astype(np.float32)
    return (jnp.asarray(np.concatenate([fwd, bwd], axis=0), dtype=BF16),
            jnp.asarray(np.concatenate([fwd.T, bwd.T], axis=1), dtype=BF16))


def _hgrn_kernel(*refs, need_ctx, n_lat, n_ctx):
    (q_ref, ff_ref, fb_ref, i_ref, z_ref, qc_ref, ffc_ref, fbc_ref, ic_ref, zc_ref,
     lb_ref, lbt_ref, gn_ref, tabq_ref, tabk_ref, tabr_ref, lvl_ref) = refs[:17]
    if need_ctx:
        y_ref, yc_ref, u_ref, dec_ref = refs[17:]
    else:
        y_ref, u_ref, dec_ref = refs[17:]
        yc_ref = None
    L, nl = CHUNK, N_LEVELS
    lane = lax.broadcasted_iota(jnp.int32, (1, LANES), 1)
    hm0 = lane < DH_A
    rm0 = lax.broadcasted_iota(jnp.int32, (LANES, 1), 0) < DH_A
    row_h = lax.broadcasted_iota(jnp.int32, (LANES, LANES), 0) // DH_A
    col_h = lax.broadcasted_iota(jnp.int32, (LANES, LANES), 1) // DH_A
    same_head = row_h == col_h
    gn = gn_ref[0]
    segs = ((0, n_ctx, qc_ref, (ffc_ref, fbc_ref), ic_ref, zc_ref, yc_ref),
            (n_ctx, n_lat, q_ref, (ff_ref, fb_ref), i_ref, z_ref, y_ref))

    def decay(fl, lbd):
        a0 = jnp.log(lbd)
        b0 = jnp.log(1.0 - lbd) + _log_sigmoid(fl)
        logf = jnp.maximum(a0, b0) + jnp.log(1.0 + jnp.exp(-jnp.abs(a0 - b0)))
        return logf * LOG2_E, (1.0 - lbd) * _sigmoid(-fl)

    def rows_of(c):
        return pl.ds(pl.multiple_of(c * L, L), L)

    tok = lax.broadcasted_iota(jnp.int32, (L, 1), 0)

    def query_exponents(logf, bcum, d):
        if d == 0:
            nb, pair_end = pltpu.roll(logf, 1, 0), (tok % 4) == 3
        else:
            nb, pair_end = pltpu.roll(logf, L - 1, 0), (tok % 4) == 0
        eqs = [logf, logf + jnp.where(pair_end, nb, 0.0)]
        for lv in range(2, nl):
            h = 1 << lv
            r = h - 1 if d == 0 else h
            g = jnp.concatenate([jnp.broadcast_to(bcum[blk + r:blk + r + 1], (2 * h, LANES))
                                 for blk in range(0, L, 2 * h)], axis=0)
            eqs.append(bcum - g)
        return eqs

    for base, n, _, f_refs, ai_ref, _, _ in segs:
        def increments(cs, base=base, f_refs=f_refs, ai_ref=ai_ref):
            jd = [(j, d) for j in range(len(cs)) for d in range(2)]
            flt = {(j, d): f_refs[d][0, rows_of(cs[j]), :].T for j, d in jd}
            lk = {(j, d): decay(flt[(j, d)], lbt_ref[d, 0]) for j, d in jd}
            remt = {(j, d): _exact_right(lk[(j, d)][0], tabr_ref[d]) for j, d in jd}
            vb = [ai_ref[0, rows_of(c), :].astype(BF16) for c in cs]
            kr = {x: (lk[x][1] * jnp.exp2(remt[x])).astype(BF16) for x in jd}
            u = {(j, d): _dot(kr[(j, d)], vb[j]) for j, d in jd}
            for j, d in jd:
                edge = 0 if d == 0 else L - 1
                b_tot = remt[(j, d)][:, edge:edge + 1] + lk[(j, d)][0][:, edge:edge + 1]
                u_ref[d, base + cs[j]] = jnp.where(same_head, u[(j, d)], 0.0)
                dec_ref[d, base + cs[j]] = jnp.broadcast_to(jnp.exp2(b_tot), (LANES, LANES))
        _grouped_loop(n, increments)

    def advance(d, idx, s):
        u = u_ref[d, idx]
        u_ref[d, idx] = s
        return s * dec_ref[d, idx] + u

    states = (jnp.zeros((LANES, LANES), F32),) * 2
    for base, n, *_ in segs:
        def scan_step(i, states, base=base, n=n):
            return (advance(0, base + i, states[0]), advance(1, base + n - 1 - i, states[1]))
        states = lax.fori_loop(0, n, scan_step, states)

    def finish(o, z):
        sq = o * o
        ms0 = jnp.sum(jnp.where(hm0, sq, 0.0), axis=-1, keepdims=True)
        ms1 = jnp.sum(jnp.where(hm0, 0.0, sq), axis=-1, keepdims=True)
        ms = jnp.where(hm0, ms0, ms1) * (1.0 / DH_A)
        return o * lax.rsqrt(ms + EPS) * gn * _silu(z)

    for base, n, aq_ref, f_refs, ai_ref, zz_ref, out_ref in segs:
        if out_ref is None:
            continue

        def outputs(cs, base=base, aq_ref=aq_ref, f_refs=f_refs, ai_ref=ai_ref, zz_ref=zz_ref,
                    out_ref=out_ref):
            js = range(len(cs))
            jd = [(j, d) for j in js for d in range(2)]
            q = [_silu(aq_ref[0, rows_of(c), :]) * (DH_A ** -0.5) for c in cs]
            fl = {(j, d): f_refs[d][0, rows_of(cs[j]), :] for j, d in jd}
            logf = {(j, d): decay(fl[(j, d)], lb_ref[d, 0])[0] for j, d in jd}
            lkt = {(j, d): decay(fl[(j, d)].T, lbt_ref[d, 0]) for j, d in jd}
            bcum = {(j, d): _exact_left(tabq_ref[d], logf[(j, d)]) for j, d in jd}
            eq = {(j, d): query_exponents(logf[(j, d)], bcum[(j, d)], d) for j, d in jd}
            ek = {(j, d): _exact_right(lkt[(j, d)][0], tabk_ref[d]) for j, d in jd}
            a = {x: jnp.zeros((L, 2 * L), F32) for x in jd}
            zero = jnp.zeros((DH_A, L), BF16)
            for lv in range(nl + 1):
                at_level = [lvl_ref[d] == float(lv) for d in range(2)]
                for j, d in jd:
                    kt = lkt[(j, d)][1]
                    if lv < nl:
                        qt = q[j] * jnp.exp2(eq[(j, d)][lv])
                        ktl = (kt * jnp.exp2(ek[(j, d)][:, lv * L:(lv + 1) * L])).astype(BF16)
                    else:
                        qt, ktl = q[j], kt.astype(BF16)
                    w = jnp.concatenate([jnp.concatenate([ktl[:DH_A], zero], axis=0),
                                         jnp.concatenate([zero, ktl[DH_A:]], axis=0)], axis=1)
                    a[(j, d)] = jnp.where(at_level[d], _dot(qt.astype(BF16), w), a[(j, d)])
            ai = [ai_ref[0, rows_of(c), :] for c in cs]
            vs = [jnp.concatenate([jnp.where(hm0, x, 0.0), jnp.where(hm0, 0.0, x)], axis=0).astype(BF16)
                  for x in ai]
            intra = {(j, d): _dot(a[(j, d)].astype(BF16), vs[j]) for j, d in jd}
            inter = {(j, d): _dot((q[j] * jnp.exp2(bcum[(j, d)])).astype(BF16),
                                  u_ref[d, base + cs[j]].astype(BF16)) for j, d in jd}
            for j, c in enumerate(cs):
                o = intra[(j, 0)] + inter[(j, 0)] + intra[(j, 1)] + inter[(j, 1)]
                out_ref[0, rows_of(c), :] = finish(o, zz_ref[0, rows_of(c), :])
        _grouped_loop(n, outputs)


def _hgrn_mixer(pa, pac, lb, gn, tabs, need_ctx):
    b, t, _ = pa.shape
    tc = pac.shape[1]
    npair = W_A // LANES
    n_chunks = (t + tc) // CHUNK
    lat = lambda part: pl.BlockSpec((1, t, LANES), lambda i, p, part=part: (i, 0, part * npair + p))
    ctx = lambda part: pl.BlockSpec((1, tc, LANES), lambda i, p, part=part: (i, 0, part * npair + p))
    whole = lambda a: pl.BlockSpec(a.shape, lambda i, p: (0, 0, 0))
    in_specs = ([lat(0), lat(1), lat(2), lat(3), lat(4), ctx(0), ctx(1), ctx(2), ctx(3), ctx(4),
                 pl.BlockSpec((2, 1, 1, LANES), lambda i, p: (0, p, 0, 0)),
                 pl.BlockSpec((2, 1, LANES, 1), lambda i, p: (0, p, 0, 0)),
                 pl.BlockSpec((1, 1, LANES), lambda i, p: (p, 0, 0))] + [whole(a) for a in tabs])
    out_specs = [pl.BlockSpec((1, t, LANES), lambda i, p: (i, 0, p))]
    out_shape = [jax.ShapeDtypeStruct((b, t, W_A), F32)]
    scratch = [pltpu.VMEM((2, n_chunks, LANES, LANES), F32)] * 2
    if need_ctx:
        out_specs.append(pl.BlockSpec((1, tc, LANES), lambda i, p: (i, 0, p)))
        out_shape.append(jax.ShapeDtypeStruct((b, tc, W_A), F32))
    res = pl.pallas_call(
        functools.partial(_hgrn_kernel, need_ctx=need_ctx, n_lat=t // CHUNK, n_ctx=tc // CHUNK),
        grid=(b, npair),
        in_specs=in_specs,
        out_specs=out_specs,
        out_shape=out_shape,
        scratch_shapes=scratch,
        compiler_params=_cparams("arbitrary", "arbitrary"),
        name="hgrn2_mixer_ctx" if need_ctx else "hgrn2_mixer",
    )(pa, pa, pa, pa, pa, pac, pac, pac, pac, pac,
      lb.reshape(2, npair, 1, LANES), lb.reshape(2, npair, LANES, 1), gn.reshape(npair, 1, LANES), *tabs)
    return (res[0], res[1]) if need_ctx else (res[0], None)


def _mlstm_kernel(*refs, need_ctx, n_lat, n_ctx):
    (q_ref, k_ref, v_ref, o_ref, z_ref, qc_ref, kc_ref, vc_ref, oc_ref, zc_ref,
     g_ref, gt_ref, gc_ref, gtc_ref, gn_ref, tri_ref, trit_ref) = refs[:17]
    if need_ctx:
        y_ref, yc_ref, c_ref, st_ref, vt_ref = refs[17:]
    else:
        y_ref, c_ref, st_ref, vt_ref = refs[17:]
        yc_ref = None
    L = CHUNK
    sub = lax.broadcasted_iota(jnp.int32, (LANES, 1), 0)
    rows_i = lax.broadcasted_iota(jnp.int32, (L, L), 0)
    cols_i = lax.broadcasted_iota(jnp.int32, (L, L), 1)
    gn = gn_ref[0]
    segs = ((0, n_ctx, qc_ref, kc_ref, vc_ref, oc_ref, zc_ref, gc_ref, gtc_ref, yc_ref),
            (n_ctx, n_lat, q_ref, k_ref, v_ref, o_ref, z_ref, g_ref, gt_ref, y_ref))
    B_TOT, M_LOC, M_PREV = 0, 1, 2

    def rows_of(c):
        return pl.ds(pl.multiple_of(c * L, L), L)

    loop = _chunk_loop

    def row_gates(grow):
        er = _exact_right(_log_sigmoid(grow), trit_ref[...])
        out = []
        for d in range(2):
            b_row = er[2 + d:3 + d, d * L:(d + 1) * L]
            out.append((grow[d:d + 1, :], b_row, b_row[:, L - 1:L] if d == 0 else b_row[:, 0:1]))
        return out

    for base, n, _, _, vv, *_ in segs:
        def transpose_values(c, carry, base=base, vv=vv):
            vt_ref[:, rows_of(base + c)] = jnp.where(sub == ONE_LANE, 1.0, vv[0, rows_of(c), :].T)
            return carry
        loop(n, transpose_values)

    for base, n, _, kk, _, _, _, _, ggt, _ in segs:
        def local(cs, base=base, kk=kk, ggt=ggt):
            jd = [(j, d) for j in range(len(cs)) for d in range(2)]
            gates = [row_gates(ggt[0, :, rows_of(c)]) for c in cs]
            w = {(j, d): gates[j][d][2] + gates[j][d][0] - gates[j][d][1] for j, d in jd}
            m_loc = {x: jnp.max(w[x], axis=1, keepdims=True) for x in jd}
            ew = {x: jnp.exp(w[x] - m_loc[x]) for x in jd}
            kb = [kk[0, rows_of(c), :].astype(BF16) for c in cs]
            vat = [vt_ref[:, rows_of(base + c)] for c in cs]
            lhs = {(j, d): (vat[j] * ew[(j, d)]).astype(BF16) for j, d in jd}
            out = {(j, d): _dot(lhs[(j, d)], kb[j]) for j, d in jd}
            for j, d in jd:
                c_ref[d, base + cs[j]] = out[(j, d)]
                st_ref[d, base + cs[j], B_TOT:B_TOT + 1, :] = jnp.broadcast_to(gates[j][d][2], (1, LANES))
                st_ref[d, base + cs[j], M_LOC:M_LOC + 1, :] = jnp.broadcast_to(m_loc[(j, d)], (1, LANES))
        _grouped_loop(n, local)

    def advance(d, idx, state):
        c_prev, m_prev = state
        c_loc = c_ref[d, idx]
        b_tot = st_ref[d, idx, B_TOT:B_TOT + 1, :]
        m_loc = st_ref[d, idx, M_LOC:M_LOC + 1, :]
        c_ref[d, idx] = c_prev
        st_ref[d, idx, M_PREV:M_PREV + 1, :] = m_prev
        m_new = jnp.maximum(b_tot + m_prev, m_loc)
        return (jnp.exp(b_tot + m_prev - m_new) * c_prev + jnp.exp(m_loc - m_new) * c_loc, m_new)

    s0 = (jnp.zeros((LANES, LANES), F32), jnp.zeros((1, LANES), F32))
    states = (s0, s0)
    for base, n, *_ in segs:
        def scan_step(i, states, base=base, n=n):
            return (advance(0, base + i, states[0]), advance(1, base + n - 1 - i, states[1]))
        states = lax.fori_loop(0, n, scan_step, states)

    for base, n, qq, kk, _, oo, zz, gg, ggt, out_ref in segs:
        if out_ref is None:
            continue

        def outputs(cs, base=base, qq=qq, kk=kk, oo=oo, zz=zz, gg=gg, ggt=ggt, out_ref=out_ref):
            js = range(len(cs))
            jd = [(j, d) for j in js for d in range(2)]
            qb = [qq[0, rows_of(c), :].astype(BF16) for c in cs]
            sq = [_dot_nt(jnp.concatenate([kk[0, rows_of(c), :].astype(BF16),
                                           c_ref[0, base + c].astype(BF16),
                                           c_ref[1, base + c].astype(BF16)], axis=0), qb[j])
                  for j, c in enumerate(cs)]
            gcol = [gg[0, rows_of(c), :] for c in cs]
            b_cols = [_exact_left(tri_ref[...], _log_sigmoid(g)) for g in gcol]
            gates = [row_gates(ggt[0, :, rows_of(c)]) for c in cs]
            r_col = {(j, d): gcol[j][:, d:d + 1] - b_cols[j][d * L:(d + 1) * L, 2 + d:3 + d]
                     for j, d in jd}
            valid = [rows_i <= cols_i, rows_i >= cols_i]
            dm = {(j, d): jnp.where(valid[d], gates[j][d][1] + r_col[(j, d)], NEG) for j, d in jd}
            inter = {(j, d): gates[j][d][1] + st_ref[d, base + cs[j], M_PREV:M_PREV + 1, 0:1] for j, d in jd}
            m_t = {x: jnp.maximum(jnp.max(dm[x], axis=0, keepdims=True), inter[x]) for x in jd}
            p = {(j, d): sq[j][:L] * jnp.exp(dm[(j, d)] - m_t[(j, d)]) for j, d in jd}
            e_in = {x: jnp.exp(inter[x] - m_t[x]) for x in jd}
            vat = [vt_ref[:, rows_of(base + c)].astype(BF16) for c in cs]
            pv = {(j, d): _dot(vat[j], p[(j, d)].astype(BF16)) for j, d in jd}
            hts = []
            for j in js:
                ht = jnp.zeros((LANES, L), F32)
                for d in range(2):
                    x = (j, d)
                    qct = sq[j][(1 + d) * L:(2 + d) * L]
                    num = pv[x] + e_in[x] * qct
                    den = jnp.sum(p[x], axis=0, keepdims=True) + e_in[x] * qct[ONE_LANE:ONE_LANE + 1, :]
                    ht = ht + num / jnp.maximum(jnp.abs(den), jnp.exp(-m_t[x]))
                hts.append(jnp.where(sub < DH_B, ht, 0.0))
            ms = [jnp.sum(ht * ht, axis=0, keepdims=True) * (1.0 / DH_B) for ht in hts]
            h = [(ht * lax.rsqrt(m + EPS)).T for ht, m in zip(hts, ms)]
            for j, c in enumerate(cs):
                rows = rows_of(c)
                out_ref[0, rows, :] = _sigmoid(oo[0, rows, :]) * (h[j] * gn) * _silu(zz[0, rows, :])
        _grouped_loop(n, outputs)


def _mlstm_mixer(pb, pg, pgt, pbc, pgc, pgtc, gn, tris, need_ctx):
    b, t, _ = pb.shape
    tc = pbc.shape[1]
    tri, trit = tris
    lat = lambda part: pl.BlockSpec((1, t, LANES), lambda i, h, part=part: (i, 0, part * H_B + h))
    ctx = lambda part: pl.BlockSpec((1, tc, LANES), lambda i, h, part=part: (i, 0, part * H_B + h))
    in_specs = ([lat(p) for p in range(5)] + [ctx(p) for p in range(5)] +
                [pl.BlockSpec((1, t, LANES), lambda i, h: (i, 0, h)),
                 pl.BlockSpec((1, 8, t), lambda i, h: (i, h, 0)),
                 pl.BlockSpec((1, tc, LANES), lambda i, h: (i, 0, h)),
                 pl.BlockSpec((1, 8, tc), lambda i, h: (i, h, 0)),
                 pl.BlockSpec((1, 1, LANES), lambda i, h: (h, 0, 0)),
                 pl.BlockSpec(tri.shape, lambda i, h: (0, 0)),
                 pl.BlockSpec(trit.shape, lambda i, h: (0, 0))])
    out_specs = [pl.BlockSpec((1, t, LANES), lambda i, h: (i, 0, h))]
    out_shape = [jax.ShapeDtypeStruct((b, t, H_B * LANES), F32)]
    n_chunks = (t + tc) // CHUNK
    scratch = [pltpu.VMEM((2, n_chunks, LANES, LANES), F32),
               pltpu.VMEM((2, n_chunks, 8, LANES), F32),
               pltpu.VMEM((LANES, t + tc), F32)]
    if need_ctx:
        out_specs.append(pl.BlockSpec((1, tc, LANES), lambda i, h: (i, 0, h)))
        out_shape.append(jax.ShapeDtypeStruct((b, tc, H_B * LANES), F32))
    res = pl.pallas_call(
        functools.partial(_mlstm_kernel, need_ctx=need_ctx, n_lat=t // CHUNK, n_ctx=tc // CHUNK),
        grid=(b, H_B),
        in_specs=in_specs,
        out_specs=out_specs,
        out_shape=out_shape,
        scratch_shapes=scratch,
        compiler_params=_cparams("arbitrary", "arbitrary"),
        name="mlstm_mixer_ctx" if need_ctx else "mlstm_mixer",
    )(pb, pb, pb, pb, pb, pbc, pbc, pbc, pbc, pbc, pg, pgt, pgc, pgtc, gn, tri, trit)
    return (res[0], res[1]) if need_ctx else (res[0], None)


def _natten_kernel(*refs, need_ctx, n_rows):
    if need_ctx:
        (q_ref, k_ref, v_ref, z_ref, kc_ref, vc_ref, bias_ref, qc_ref, zc_ref, y_ref, yc_ref,
         kb_ref, vt_ref, kcb_ref, vct_ref) = refs
    else:
        (q_ref, k_ref, v_ref, z_ref, kc_ref, vc_ref, bias_ref, y_ref,
         kb_ref, vt_ref, kcb_ref, vct_ref) = refs
    lane = lax.broadcasted_iota(jnp.int32, (1, LANES), 1)
    hm0 = lane < DH_C
    scale = DH_C ** -0.5
    win = WIN_ROWS * GRID_W
    t = k_ref.shape[1]
    tc = kc_ref.shape[1]
    blk = LANES

    nb = t // blk
    group = NA_ROW_UNROLL

    def blocks(i):
        return [pl.ds(pl.multiple_of((i * group + j) * blk, blk), blk) for j in range(group)]

    def stage(i, carry):
        tiles = [v_ref[0, rows, :].T.astype(BF16) for rows in blocks(i)]
        for rows, x in zip(blocks(i), tiles):
            kb_ref[rows, :] = k_ref[0, rows, :].astype(BF16)
            vt_ref[0, :, rows] = x
        return carry

    lax.fori_loop(0, nb // group, stage, 0)

    def shifted(start):
        src = start + GRID_W
        if not isinstance(src, int):
            src = pl.multiple_of(src, GRID_W)
        return v_ref[0, pl.ds(src, blk), :].T.astype(BF16)

    def stage_shifted(i, carry):
        tiles = [shifted(rows.start) for rows in blocks(i)]
        for rows, x in zip(blocks(i), tiles):
            vt_ref[1, :, rows] = x
        return carry

    lax.fori_loop(0, nb // group - 1, stage_shifted, 0)
    tail = [(nb - group + j) * blk for j in range(group - 1)]
    tiles = [shifted(start) for start in tail]
    for start, x in zip(tail, tiles):
        vt_ref[1, :, start:start + blk] = x
    last = v_ref[0, t - blk:t, :].T
    vt_ref[1, :, t - blk:t] = pltpu.roll(last, GRID_W, 1).astype(BF16)

    for i in range(tc // blk):
        kcb_ref[i * blk:(i + 1) * blk, :] = kc_ref[0, i * blk:(i + 1) * blk, :].astype(BF16)
    tiles = [vc_ref[0, i * blk:(i + 1) * blk, :].T.astype(BF16) for i in range(tc // blk)]
    for i, x in enumerate(tiles):
        vct_ref[:, i * blk:(i + 1) * blk] = x

    def attend(qs, keys, vals_t, biases):
        n = qs[0].shape[0]
        q2 = [jnp.concatenate([jnp.where(hm0, q, 0.0), jnp.where(hm0, 0.0, q)], axis=0).astype(BF16)
              for q in qs]
        lt = [_dot_nt(k, q) for k, q in zip(keys, q2)]
        lt = [x if b is None else jnp.concatenate([x[:b.shape[0]] + b, x[b.shape[0]:]], axis=0)
              for x, b in zip(lt, biases)]
        p = [jnp.exp(x - jnp.max(x, axis=0, keepdims=True)) for x in lt]
        den = [jnp.sum(x, axis=0, keepdims=True) for x in p]
        ot = [_dot(v, x.astype(BF16)) for v, x in zip(vals_t, p)]
        o = [(x / d).T for x, d in zip(ot, den)]
        return [jnp.where(hm0, x[:n], x[n:]) for x in o]

    def row_group(i, carry):
        rows = [i * NA_ROW_UNROLL + j for j in range(NA_ROW_UNROLL)]
        rs = [jnp.clip(r - WIN_ROWS // 2, 0, n_rows - WIN_ROWS) for r in rows]
        qrows = [pl.ds(pl.multiple_of(r * GRID_W, GRID_W), GRID_W) for r in rows]
        keys, vals_t, biases = [], [], []
        for r, s in zip(rows, rs):
            par = s % 2
            wrows = pl.ds(pl.multiple_of(s * GRID_W, GRID_W), win)
            wlanes = pl.ds(pl.multiple_of((s - par) * GRID_W, LANES), win)
            keys.append(jnp.concatenate([kb_ref[wrows, :], kcb_ref[...]], axis=0))
            vals_t.append(jnp.concatenate([vt_ref[par, :, wlanes], vct_ref[...]], axis=1))
            biases.append(bias_ref[s - r + (WIN_ROWS - 1), 0])
        outs = attend([q_ref[0, qr, :] * scale for qr in qrows], keys, vals_t, biases)
        for qr, o in zip(qrows, outs):
            y_ref[0, qr, :] = o * _silu(z_ref[0, qr, :])
        return carry

    lax.fori_loop(0, n_rows // NA_ROW_UNROLL, row_group, 0)

    if need_ctx:
        o, = attend([qc_ref[0] * scale], [kcb_ref[...]], [vct_ref[...]], [None])
        yc_ref[0] = o * _silu(zc_ref[0])


def _natten_mixer(pc, pcc, bias, need_ctx):
    b, t, _ = pc.shape
    tc = pcc.shape[1]
    npair = W_C // LANES
    n_rows = t // GRID_W
    lat = lambda part: pl.BlockSpec((1, t, LANES), lambda i, p, part=part: (i, 0, part * npair + p))
    ctx = lambda part: pl.BlockSpec((1, tc, LANES), lambda i, p, part=part: (i, 0, part * npair + p))
    in_specs = [lat(0), lat(1), lat(2), lat(3), ctx(1), ctx(2),
                pl.BlockSpec((WIN_ROWS, 1, WIN_ROWS * GRID_W, LANES), lambda i, p: (0, p, 0, 0))]
    args = [pc, pc, pc, pc, pcc, pcc, bias]
    out_specs = [pl.BlockSpec((1, t, LANES), lambda i, p: (i, 0, p))]
    out_shape = [jax.ShapeDtypeStruct((b, t, W_C), F32)]
    if need_ctx:
        in_specs += [ctx(0), ctx(3)]
        args += [pcc, pcc]
        out_specs.append(pl.BlockSpec((1, tc, LANES), lambda i, p: (i, 0, p)))
        out_shape.append(jax.ShapeDtypeStruct((b, tc, W_C), F32))
    scratch = [pltpu.VMEM((t, LANES), BF16), pltpu.VMEM((2, LANES, t), BF16),
               pltpu.VMEM((tc, LANES), BF16), pltpu.VMEM((LANES, tc), BF16)]
    res = pl.pallas_call(
        functools.partial(_natten_kernel, need_ctx=need_ctx, n_rows=n_rows),
        grid=(b, npair),
        in_specs=in_specs,
        out_specs=out_specs,
        out_shape=out_shape,
        scratch_shapes=scratch,
        compiler_params=_cparams("arbitrary", "arbitrary"),
        name="natten_mixer_ctx" if need_ctx else "natten_mixer",
    )(*args)
    return (res[0], res[1]) if need_ctx else (res[0], None)


def _natten_bias(rpb):
    col = np.arange(GRID_W)
    cs = np.clip(col - WIN_COLS // 2, 0, GRID_W - WIN_COLS)
    band = (col[None, :] >= cs[:, None]) & (col[None, :] < cs[:, None] + WIN_COLS)
    dc = np.clip(col[None, :] - col[:, None] + WIN_COLS - 1, 0, 2 * WIN_COLS - 2)
    tz = jnp.where(band[None, None], rpb.astype(F32)[:, :, dc], NEG)
    dr = np.arange(WIN_ROWS)[:, None] + np.arange(WIN_ROWS)[None, :]
    tab = tz[:, dr]
    tab = tab.reshape(H_C // 2, 2, WIN_ROWS, WIN_ROWS, GRID_W, GRID_W)
    tab = jnp.transpose(tab, (2, 0, 3, 5, 1, 4))
    return tab.reshape(WIN_ROWS, H_C // 2, WIN_ROWS * GRID_W, 2 * GRID_W)


def _outproj_kernel(ya_ref, yb_ref, yc_ref, x_ref, gt_ref, g_ref, wa_ref, wb_ref, wc_ref, o_ref):
    u = (_dot(ya_ref[0].astype(BF16), wa_ref[...]) + _dot(yb_ref[0].astype(BF16), wb_ref[...])
         + _dot(yc_ref[0].astype(BF16), wc_ref[...]))
    ms = jnp.mean(u * u, axis=-1, keepdims=True)
    o_ref[0] = x_ref[0] + gt_ref[0] * (u * lax.rsqrt(ms + EPS) * g_ref[...])


def _out_projection(ya, yb, yc, x, gt, g, wts, tm):
    b, t, d = x.shape
    wa, wb, wc = wts
    bm = gt.shape[0]
    mod_map = (lambda i, j: (i, 0, 0)) if bm > 1 else (lambda i, j: (0, 0, 0))
    const = lambda i, j: (0, 0)
    tile = lambda n: pl.BlockSpec((1, tm, n), lambda i, j: (i, j, 0))
    return pl.pallas_call(
        _outproj_kernel,
        grid=(b, t // tm),
        in_specs=[tile(ya.shape[2]), tile(yb.shape[2]), tile(yc.shape[2]), tile(d),
                  pl.BlockSpec((1, 1, d), mod_map),
                  pl.BlockSpec((1, d), const),
                  pl.BlockSpec(wa.shape, const), pl.BlockSpec(wb.shape, const), pl.BlockSpec(wc.shape, const)],
        out_specs=tile(d),
        out_shape=jax.ShapeDtypeStruct((b, t, d), F32),
        compiler_params=_cparams("arbitrary", "arbitrary"),
        name="out_projection",
    )(ya, yb, yc, x, gt, g.reshape(1, d), wa, wb, wc)


def _rope_lane(dd):
    q = DH_B // 4
    dd = np.asarray(dd)
    return np.where(dd < q, dd, np.where(dd < 2 * q, LANES // 2 + dd - q,
                    np.where(dd < 3 * q, dd - q, LANES // 2 + dd - 2 * q)))


def _column_maps():
    hw = H_B * LANES
    src = np.full((5 * hw,), P_IN, dtype=np.int32)
    dd = np.arange(DH_B)
    for part in range(5):
        lanes = _rope_lane(dd) if part < 2 else dd
        for h in range(H_B):
            src[part * hw + h * LANES + lanes] = OFF_B + part * W_B + h * DH_B + dd
    gsrc = np.full((H_B * LANES,), P_IN, dtype=np.int32)
    gtsrc = np.full((H_B * 8,), P_IN, dtype=np.int32)
    for h in range(H_B):
        for j in range(4):
            gsrc[h * LANES + j] = OFF_G + j * H_B + h
            gtsrc[h * 8 + j] = OFF_G + j * H_B + h
    return src, gsrc, gtsrc


def _rope_tables(t):
    pos = np.arange(t)
    q = DH_B // 4
    inv = ROPE_BASE ** (-jnp.arange(0, 2 * q, 2, dtype=F32) / (2 * q))
    ang_r = jnp.asarray(pos // GRID_W, F32)[:, None] * inv[None, :]
    ang_c = jnp.asarray(pos % GRID_W, F32)[:, None] * inv[None, :]
    zeros = jnp.zeros((t, LANES // 2 - 2 * q), F32)
    cos_half = jnp.concatenate([jnp.cos(ang_r), jnp.cos(ang_c), zeros], axis=1)
    sin_half = jnp.concatenate([jnp.sin(ang_r), jnp.sin(ang_c), zeros], axis=1)
    return (jnp.concatenate([cos_half, cos_half], axis=1),
            jnp.concatenate([-sin_half, sin_half], axis=1))


def _pad_heads(v):
    v = v.reshape(v.shape[:-1] + (H_B, DH_B))
    v = jnp.pad(v, [(0, 0)] * (v.ndim - 1) + [(0, LANES - DH_B)])
    return v.reshape(v.shape[:-2] + (H_B * LANES,))


def kernel(x, c, ctx, c_ctx, w_mod, b_mod, g_pre, g_post, w_in, w_out, hgrn_lb, hgrn_gn, mlstm_gate_b, mlstm_gn, na_rpb):
    depth = w_in.shape[0]
    b, t, d = x.shape
    tc = ctx.shape[1]
    assert t % CHUNK == 0 and tc % CHUNK == 0 and t // GRID_W >= WIN_ROWS and t % GRID_W == 0
    assert w_in.shape[2] == P_IN and (1 << N_LEVELS) == CHUNK

    lb_cum = jnp.cumsum(jax.nn.softmax(hgrn_lb.astype(F32), axis=0), axis=0)
    lb_all = lb_cum - lb_cum[0]
    src, gsrc, gtsrc = _column_maps()
    hgrn_tabs = _hgrn_tables()
    tris = _tri_tables()
    rope = _rope_tables(t)
    mod_rows = 16
    cc = jnp.concatenate([c.astype(F32), c_ctx.astype(F32)[None],
                          jnp.zeros((mod_rows - b - 1, d), F32)], axis=0)
    tm = min(256, t)
    tmc = min(256, tc)

    for l in range(depth):
        need_ctx = l < depth - 1
        mod = _modulation(cc, w_mod[l], b_mod[l])
        sh, sc, gt = (mod[:b, i * d:(i + 1) * d].reshape(b, 1, d) for i in range(3))
        shc, scc, gtc = (mod[b:b + 1, i * d:(i + 1) * d].reshape(1, 1, d) for i in range(3))

        w = w_in[l]
        w_ext = jnp.concatenate([w, jnp.zeros((d, 1), w.dtype)], axis=1)
        gb = mlstm_gate_b[l].astype(F32)
        gbr = jnp.pad(gb.T, ((0, 0), (0, LANES - 4))).reshape(1, H_B * LANES)
        gbc = jnp.pad(gb.T, ((0, 0), (0, 4))).reshape(H_B * 8, 1)
        wts = (w[:, :OFF_B].astype(BF16), w_ext[:, src].astype(BF16), w[:, OFF_C:].astype(BF16),
               w_ext[:, gsrc].astype(BF16), w_ext[:, gtsrc].T.astype(BF16), gbr, gbc)

        pa, pb, pc, pg, pgt = _in_projection(x, sh, sc, g_pre[l], wts, rope, tm)
        pac, pbc, pcc, pgc, pgtc = _in_projection(ctx, shc, scc, g_pre[l], wts, None, tmc)

        ya, yac = _hgrn_mixer(pa, pac, lb_all[l], hgrn_gn[l], hgrn_tabs, need_ctx)
        gnb = _pad_heads(mlstm_gn[l].astype(F32)).reshape(H_B, 1, LANES)
        yb, ybc = _mlstm_mixer(pb, pg, pgt, pbc, pgc, pgtc, gnb, tris, need_ctx)
        yc, ycc = _natten_mixer(pc, pcc, _natten_bias(na_rpb[l]), need_ctx)

        wo = w_out[l]
        wo_b = jnp.pad(wo[W_A:W_A + W_B].reshape(H_B, DH_B, d), ((0, 0), (0, LANES - DH_B), (0, 0)))
        wo_parts = (wo[:W_A].astype(BF16), wo_b.reshape(H_B * LANES, d).astype(BF16),
                    wo[W_A + W_B:].astype(BF16))
        x = _out_projection(ya, yb, yc, x, gt, g_post[l], wo_parts, tm)
        if need_ctx:
            ctx = _out_projection(yac, ybc, ycc, ctx, gtc, g_post[l], wo_parts, tmc)
    return x
```

```python
import functools
import math

import numpy as np
import jax
import jax.numpy as jnp
from jax import lax
from jax.experimental import pallas as pl
from jax.experimental.pallas import tpu as pltpu

F32 = jnp.float32
BF16 = jnp.bfloat16

LANES = 128
GRID_W = 64
W_A, H_A, DH_A = 256, 4, 64
W_B, H_B, DH_B = 384, 4, 96
W_C, H_C, DH_C = 384, 6, 64
WIN_ROWS, WIN_COLS = 8, 16
ROPE_BASE = 10000.0
EPS = 1e-6
NEG = -1e30
LOG2_E = math.log2(math.e)
CHUNK = 128
N_LEVELS = 7
NA_ROW_UNROLL = 4
CHUNK_UNROLL = 4
OFF_B = 5 * W_A
OFF_G = OFF_B + 5 * W_B
OFF_C = OFF_G + 4 * H_B
P_IN = OFF_C + 4 * W_C
ONE_LANE = DH_B
VMEM_LIMIT = 56 * 1024 * 1024


def _cparams(*sem):
    return pltpu.CompilerParams(dimension_semantics=sem, vmem_limit_bytes=VMEM_LIMIT)


def _dot(a, b):
    return jnp.dot(a, b, preferred_element_type=F32)


def _dot_nt(a, b):
    return lax.dot_general(a, b, (((1,), (1,)), ((), ())), preferred_element_type=F32)


def _dot_tn(a, b):
    return lax.dot_general(a, b, (((0,), (0,)), ((), ())), preferred_element_type=F32)


def _chunk_loop(n, body):
    u = math.gcd(n, CHUNK_UNROLL)

    def step(i, carry):
        for j in range(u):
            body(i * u + j, carry)
        return carry

    return lax.fori_loop(0, n // u, step, 0)


def _grouped_loop(n, body):
    u = math.gcd(n, CHUNK_UNROLL)

    def step(i, carry):
        body([i * u + j for j in range(u)])
        return carry

    return lax.fori_loop(0, n // u, step, 0)


def _split3(x):
    hi = x.astype(BF16)
    r = x - hi.astype(F32)
    mid = r.astype(BF16)
    lo = (r - mid.astype(F32)).astype(BF16)
    return hi, mid, lo


def _exact_left(t01, x):
    n = x.shape[1]
    r = _dot(t01, jnp.concatenate(_split3(x), axis=1))
    return r[:, :n] + r[:, n:2 * n] + r[:, 2 * n:]


def _exact_right(x, t01):
    m = x.shape[0]
    pieces = jnp.concatenate([p.astype(F32) for p in _split3(x)], axis=0).astype(BF16)
    r = _dot(pieces, t01)
    return r[:m] + r[m:2 * m] + r[2 * m:]


def _sigmoid(x):
    return 1.0 / (1.0 + jnp.exp(-x))


def _silu(x):
    return x * _sigmoid(x)


def _log_sigmoid(x):
    return jnp.minimum(x, 0.0) - jnp.log(1.0 + jnp.exp(-jnp.abs(x)))


def _mod_kernel(c_ref, w_ref, b_ref, o_ref):
    s = _silu(c_ref[...])
    o_ref[...] = _dot(s.astype(BF16), w_ref[...].astype(BF16)) + b_ref[...]


def _modulation(cc, w, b):
    rows, d = cc.shape
    n = w.shape[1]
    tn = d
    assert n % tn == 0
    return pl.pallas_call(
        _mod_kernel,
        grid=(n // tn,),
        in_specs=[pl.BlockSpec((rows, d), lambda j: (0, 0)),
                  pl.BlockSpec((d, tn), lambda j: (0, j)),
                  pl.BlockSpec((1, tn), lambda j: (0, j))],
        out_specs=pl.BlockSpec((rows, tn), lambda j: (0, j)),
        out_shape=jax.ShapeDtypeStruct((rows, n), F32),
        compiler_params=_cparams("arbitrary"),
        name="modulation",
    )(cc, w, b.reshape(1, n))


def _inproj_kernel(*refs, rotary):
    if rotary:
        (x_ref, sh_ref, sc_ref, g_ref, wa_ref, wb_ref, wc_ref, wg_ref, wgt_ref, gbr_ref, gbc_ref,
         cos_ref, sin_ref, pa_ref, pb_ref, pc_ref, pg_ref, pgt_ref) = refs
    else:
        (x_ref, sh_ref, sc_ref, g_ref, wa_ref, wb_ref, wc_ref, wg_ref, wgt_ref, gbr_ref, gbc_ref,
         pa_ref, pb_ref, pc_ref, pg_ref, pgt_ref) = refs
    x = x_ref[0]
    ms = jnp.mean(x * x, axis=-1, keepdims=True)
    h = x * lax.rsqrt(ms + EPS) * g_ref[...]
    h = h * (1.0 + sc_ref[0]) + sh_ref[0]
    hb = h.astype(BF16)
    pa_ref[0] = _dot(hb, wa_ref[...])
    pc_ref[0] = _dot(hb, wc_ref[...])
    pg_ref[0] = _dot(hb, wg_ref[...]) + gbr_ref[...]
    pgt_ref[0] = _dot_nt(wgt_ref[...], hb) + gbc_ref[...]
    hw = H_B * LANES
    scale = DH_B ** -0.5
    for part in range(5):
        p = _dot(hb, wb_ref[:, part * hw:(part + 1) * hw])
        if part == 0:
            p = p * scale
        if rotary and part < 2:
            cos = cos_ref[...]
            sin = sin_ref[...]
            for hd in range(H_B):
                ph = p[:, hd * LANES:(hd + 1) * LANES]
                pb_ref[0, :, part * hw + hd * LANES:part * hw + (hd + 1) * LANES] = (
                    ph * cos + pltpu.roll(ph, LANES // 2, 1) * sin)
        else:
            pb_ref[0, :, part * hw:(part + 1) * hw] = p


def _in_projection(x, sh, sc, g, wts, rope, tm):
    b, t, d = x.shape
    wa, wb, wc, wg, wgt, gbr, gbc = wts
    bm = sh.shape[0]
    mod_map = (lambda i, j: (i, 0, 0)) if bm > 1 else (lambda i, j: (0, 0, 0))
    const = lambda i, j: (0, 0)
    in_specs = [pl.BlockSpec((1, tm, d), lambda i, j: (i, j, 0)),
                pl.BlockSpec((1, 1, d), mod_map),
                pl.BlockSpec((1, 1, d), mod_map),
                pl.BlockSpec((1, d), const),
                pl.BlockSpec(wa.shape, const),
                pl.BlockSpec(wb.shape, const),
                pl.BlockSpec(wc.shape, const),
                pl.BlockSpec(wg.shape, const),
                pl.BlockSpec(wgt.shape, const),
                pl.BlockSpec(gbr.shape, const),
                pl.BlockSpec(gbc.shape, const)]
    args = [x, sh, sc, g.reshape(1, d), wa, wb, wc, wg, wgt, gbr, gbc]
    if rope is not None:
        in_specs += [pl.BlockSpec((tm, LANES), lambda i, j: (j, 0))] * 2
        args += list(rope)
    na, nb, nc, ng = wa.shape[1], wb.shape[1], wc.shape[1], wg.shape[1]
    out_specs = [pl.BlockSpec((1, tm, na), lambda i, j: (i, j, 0)),
                 pl.BlockSpec((1, tm, nb), lambda i, j: (i, j, 0)),
                 pl.BlockSpec((1, tm, nc), lambda i, j: (i, j, 0)),
                 pl.BlockSpec((1, tm, ng), lambda i, j: (i, j, 0)),
                 pl.BlockSpec((1, wgt.shape[0], tm), lambda i, j: (i, 0, j))]
    out_shape = [jax.ShapeDtypeStruct((b, t, na), F32),
                 jax.ShapeDtypeStruct((b, t, nb), F32),
                 jax.ShapeDtypeStruct((b, t, nc), F32),
                 jax.ShapeDtypeStruct((b, t, ng), F32),
                 jax.ShapeDtypeStruct((b, wgt.shape[0], t), F32)]
    return pl.pallas_call(
        functools.partial(_inproj_kernel, rotary=rope is not None),
        grid=(b, t // tm),
        in_specs=in_specs,
        out_specs=out_specs,
        out_shape=out_shape,
        compiler_params=_cparams("arbitrary", "arbitrary"),
        name="in_projection_rope" if rope is not None else "in_projection",
    )(*args)


def _hgrn_tables():
    L, nl = CHUNK, N_LEVELS
    idx = np.arange(L)
    t = idx[:, None]
    u = idx[None, :]
    tabq, levels = [], []
    for rev in (False, True):
        tabq.append(((u <= t) if not rev else (u >= t)).astype(np.float32))
        x = t ^ u
        lvl = np.where(x > 0, np.floor(np.log2(np.maximum(x, 1))), float(nl))
        valid = (u <= t) if not rev else (u >= t)
        lvl = np.where(valid, lvl, -1.0).astype(np.float32)
        levels.append(np.concatenate([lvl, lvl], axis=1))
    return jnp.asarray(np.stack(tabq), dtype=BF16), jnp.asarray(np.stack(levels), dtype=F32)


def _tri_tables():
    L = CHUNK
    idx = np.arange(L)
    t = idx[:, None]
    u = idx[None, :]
    fwd = (u <= t).astype(np.float32)
    bwd = (u >= t).astype(np.float32)
    return (jnp.asarray(np.concatenate([fwd, bwd], axis=0), dtype=BF16),
            jnp.asarray(np.concatenate([fwd.T, bwd.T], axis=1), dtype=BF16))


def _hgrn_kernel(*refs, need_ctx, n_lat, n_ctx):
    (q_ref, ff_ref, fb_ref, i_ref, z_ref, qc_ref, ffc_ref, fbc_ref, ic_ref, zc_ref,
     lb_ref, gn_ref, tabq_ref, lvl_ref) = refs[:14]
    if need_ctx:
        y_ref, yc_ref, u_ref, dec_ref, gate_ref = refs[14:]
    else:
        y_ref, u_ref, dec_ref, gate_ref = refs[14:]
        yc_ref = None
    L, nl = CHUNK, N_LEVELS
    lane = lax.broadcasted_iota(jnp.int32, (1, LANES), 1)
    hm0 = lane < DH_A
    row_h = lax.broadcasted_iota(jnp.int32, (LANES, LANES), 0) // DH_A
    col_h = lax.broadcasted_iota(jnp.int32, (LANES, LANES), 1) // DH_A
    same_head = row_h == col_h
    gn = gn_ref[0]
    segs = ((0, n_ctx, qc_ref, (ffc_ref, fbc_ref), ic_ref, zc_ref, yc_ref),
            (n_ctx, n_lat, q_ref, (ff_ref, fb_ref), i_ref, z_ref, y_ref))

    def decay(fl, lbd):
        a0 = jnp.log(lbd)
        b0 = jnp.log(1.0 - lbd) + _log_sigmoid(fl)
        logf = jnp.maximum(a0, b0) + jnp.log(1.0 + jnp.exp(-jnp.abs(a0 - b0)))
        return logf * LOG2_E, (1.0 - lbd) * _sigmoid(-fl)

    def rows_of(c):
        return pl.ds(pl.multiple_of(c * L, L), L)

    tok = lax.broadcasted_iota(jnp.int32, (L, 1), 0)

    def level_exponents(logf, bcum, d):
        prev, nxt = pltpu.roll(logf, 1, 0), pltpu.roll(logf, L - 1, 0)
        if d == 0:
            eqs = [logf, logf + jnp.where((tok % 4) == 3, prev, 0.0)]
            eks = [None, jnp.where((tok % 4) == 0, nxt, 0.0)]
        else:
            eqs = [logf, logf + jnp.where((tok % 4) == 0, nxt, 0.0)]
            eks = [None, jnp.where((tok % 4) == 3, prev, 0.0)]
        for lv in range(2, nl):
            h = 1 << lv
            r = h - 1 if d == 0 else h
            g = jnp.concatenate([jnp.broadcast_to(bcum[blk + r:blk + r + 1], (2 * h, LANES))
                                 for blk in range(0, L, 2 * h)], axis=0)
            eqs.append(bcum - g)
            eks.append(g - bcum)
        return eqs, eks

    def chunk_total(bcum, d):
        return bcum[L - 1:L] if d == 0 else bcum[0:1]

    for base, n, _, f_refs, ai_ref, _, _ in segs:
        def increments(cs, base=base, f_refs=f_refs, ai_ref=ai_ref):
            jd = [(j, d) for j in range(len(cs)) for d in range(2)]
            lk = {(j, d): decay(f_refs[d][0, rows_of(cs[j]), :], lb_ref[d, 0]) for j, d in jd}
            bcum = {(j, d): _exact_left(tabq_ref[d], lk[(j, d)][0]) for j, d in jd}
            b_tot = {(j, d): chunk_total(bcum[(j, d)], d) for j, d in jd}
            vb = [ai_ref[0, rows_of(c), :].astype(BF16) for c in cs]
            kr = {x: (lk[x][1] * jnp.exp2(b_tot[x] - bcum[x])).astype(BF16) for x in jd}
            u = {(j, d): _dot_tn(kr[(j, d)], vb[j]) for j, d in jd}
            dec = {x: jnp.broadcast_to(jnp.exp2(b_tot[x]), (LANES, LANES)).T for x in jd}
            for j, d in jd:
                u_ref[d, base + cs[j]] = jnp.where(same_head, u[(j, d)], 0.0)
                dec_ref[d, base + cs[j]] = dec[(j, d)]
                tok_rows = rows_of(base + cs[j])
                gate_ref[d, 0, tok_rows, :] = lk[(j, d)][0]
                gate_ref[d, 1, tok_rows, :] = lk[(j, d)][1]
                gate_ref[d, 2, tok_rows, :] = bcum[(j, d)]
        _grouped_loop(n, increments)

    def advance(d, idx, s):
        u = u_ref[d, idx]
        u_ref[d, idx] = s
        return s * dec_ref[d, idx] + u

    states = (jnp.zeros((LANES, LANES), F32),) * 2
    for base, n, *_ in segs:
        def scan_step(i, states, base=base, n=n):
            return (advance(0, base + i, states[0]), advance(1, base + n - 1 - i, states[1]))
        states = lax.fori_loop(0, n, scan_step, states)

    def finish(o, z):
        sq = o * o
        ms0 = jnp.sum(jnp.where(hm0, sq, 0.0), axis=-1, keepdims=True)
        ms1 = jnp.sum(jnp.where(hm0, 0.0, sq), axis=-1, keepdims=True)
        ms = jnp.where(hm0, ms0, ms1) * (1.0 / DH_A)
        return o * lax.rsqrt(ms + EPS) * gn * _silu(z)

    for base, n, aq_ref, f_refs, ai_ref, zz_ref, out_ref in segs:
        if out_ref is None:
            continue

        def outputs(cs, base=base, aq_ref=aq_ref, f_refs=f_refs, ai_ref=ai_ref, zz_ref=zz_ref,
                    out_ref=out_ref):
            js = range(len(cs))
            jd = [(j, d) for j in js for d in range(2)]
            q = [_silu(aq_ref[0, rows_of(c), :]) * (DH_A ** -0.5) for c in cs]
            lk = {(j, d): (gate_ref[d, 0, rows_of(base + cs[j]), :], gate_ref[d, 1, rows_of(base + cs[j]), :])
                  for j, d in jd}
            bcum = {(j, d): gate_ref[d, 2, rows_of(base + cs[j]), :] for j, d in jd}
            ex = {(j, d): level_exponents(lk[(j, d)][0], bcum[(j, d)], d) for j, d in jd}
            a = {x: jnp.zeros((L, 2 * L), F32) for x in jd}
            zero = jnp.zeros((DH_A, L), BF16)
            for lv in range(nl + 1):
                at_level = [lvl_ref[d] == float(lv) for d in range(2)]
                for j, d in jd:
                    k = lk[(j, d)][1]
                    eqs, eks = ex[(j, d)]
                    if lv < nl:
                        qt = q[j] * jnp.exp2(eqs[lv])
                        kl = k if eks[lv] is None else k * jnp.exp2(eks[lv])
                    else:
                        qt, kl = q[j], k
                    ktl = kl.T.astype(BF16)
                    w = jnp.concatenate([jnp.concatenate([ktl[:DH_A], zero], axis=0),
                                         jnp.concatenate([zero, ktl[DH_A:]], axis=0)], axis=1)
                    a[(j, d)] = jnp.where(at_level[d], _dot(qt.astype(BF16), w), a[(j, d)])
            ai = [ai_ref[0, rows_of(c), :] for c in cs]
            vs = [jnp.concatenate([jnp.where(hm0, x, 0.0), jnp.where(hm0, 0.0, x)], axis=0).astype(BF16)
                  for x in ai]
            intra = {(j, d): _dot(a[(j, d)].astype(BF16), vs[j]) for j, d in jd}
            inter = {(j, d): _dot((q[j] * jnp.exp2(bcum[(j, d)])).astype(BF16),
                                  u_ref[d, base + cs[j]].astype(BF16)) for j, d in jd}
            for j, c in enumerate(cs):
                o = intra[(j, 0)] + inter[(j, 0)] + intra[(j, 1)] + inter[(j, 1)]
                out_ref[0, rows_of(c), :] = finish(o, zz_ref[0, rows_of(c), :])
        _grouped_loop(n, outputs)


def _hgrn_mixer(pa, pac, lb, gn, tabs, need_ctx):
    b, t, _ = pa.shape
    tc = pac.shape[1]
    npair = W_A // LANES
    n_chunks = (t + tc) // CHUNK
    lat = lambda part: pl.BlockSpec((1, t, LANES), lambda i, p, part=part: (i, 0, part * npair + p))
    ctx = lambda part: pl.BlockSpec((1, tc, LANES), lambda i, p, part=part: (i, 0, part * npair + p))
    whole = lambda a: pl.BlockSpec(a.shape, lambda i, p: (0, 0, 0))
    in_specs = ([lat(0), lat(1), lat(2), lat(3), lat(4), ctx(0), ctx(1), ctx(2), ctx(3), ctx(4),
                 pl.BlockSpec((2, 1, 1, LANES), lambda i, p: (0, p, 0, 0)),
                 pl.BlockSpec((1, 1, LANES), lambda i, p: (p, 0, 0))] + [whole(a) for a in tabs])
    out_specs = [pl.BlockSpec((1, t, LANES), lambda i, p: (i, 0, p))]
    out_shape = [jax.ShapeDtypeStruct((b, t, W_A), F32)]
    scratch = [pltpu.VMEM((2, n_chunks, LANES, LANES), F32)] * 2
    scratch.append(pltpu.VMEM((2, 3, t + tc, LANES), F32))
    if need_ctx:
        out_specs.append(pl.BlockSpec((1, tc, LANES), lambda i, p: (i, 0, p)))
        out_shape.append(jax.ShapeDtypeStruct((b, tc, W_A), F32))
    res = pl.pallas_call(
        functools.partial(_hgrn_kernel, need_ctx=need_ctx, n_lat=t // CHUNK, n_ctx=tc // CHUNK),
        grid=(b, npair),
        in_specs=in_specs,
        out_specs=out_specs,
        out_shape=out_shape,
        scratch_shapes=scratch,
        compiler_params=_cparams("arbitrary", "arbitrary"),
        name="hgrn2_mixer_ctx" if need_ctx else "hgrn2_mixer",
    )(pa, pa, pa, pa, pa, pac, pac, pac, pac, pac,
      lb.reshape(2, npair, 1, LANES), gn.reshape(npair, 1, LANES), *tabs)
    return (res[0], res[1]) if need_ctx else (res[0], None)


def _mlstm_kernel(*refs, need_ctx, n_lat, n_ctx):
    (q_ref, k_ref, v_ref, o_ref, z_ref, qc_ref, kc_ref, vc_ref, oc_ref, zc_ref,
     g_ref, gt_ref, gc_ref, gtc_ref, gn_ref, tri_ref, trit_ref) = refs[:17]
    if need_ctx:
        y_ref, yc_ref, c_ref, st_ref, vt_ref = refs[17:]
    else:
        y_ref, c_ref, st_ref, vt_ref = refs[17:]
        yc_ref = None
    L = CHUNK
    sub = lax.broadcasted_iota(jnp.int32, (LANES, 1), 0)
    rows_i = lax.broadcasted_iota(jnp.int32, (L, L), 0)
    cols_i = lax.broadcasted_iota(jnp.int32, (L, L), 1)
    gn = gn_ref[0]
    segs = ((0, n_ctx, qc_ref, kc_ref, vc_ref, oc_ref, zc_ref, gc_ref, gtc_ref, yc_ref),
            (n_ctx, n_lat, q_ref, k_ref, v_ref, o_ref, z_ref, g_ref, gt_ref, y_ref))
    B_TOT, M_LOC, M_PREV = 0, 1, 2

    def rows_of(c):
        return pl.ds(pl.multiple_of(c * L, L), L)

    loop = _chunk_loop

    def row_gates(grow):
        er = _exact_right(_log_sigmoid(grow), trit_ref[...])
        out = []
        for d in range(2):
            b_row = er[2 + d:3 + d, d * L:(d + 1) * L]
            out.append((grow[d:d + 1, :], b_row, b_row[:, L - 1:L] if d == 0 else b_row[:, 0:1]))
        return out

    for base, n, _, _, vv, *_ in segs:
        def transpose_values(c, carry, base=base, vv=vv):
            vt_ref[:, rows_of(base + c)] = jnp.where(sub == ONE_LANE, 1.0, vv[0, rows_of(c), :].T)
            return carry
        loop(n, transpose_values)

    for base, n, _, kk, _, _, _, _, ggt, _ in segs:
        def local(cs, base=base, kk=kk, ggt=ggt):
            jd = [(j, d) for j in range(len(cs)) for d in range(2)]
            gates = [row_gates(ggt[0, :, rows_of(c)]) for c in cs]
            w = {(j, d): gates[j][d][2] + gates[j][d][0] - gates[j][d][1] for j, d in jd}
            m_loc = {x: jnp.max(w[x], axis=1, keepdims=True) for x in jd}
            ew = {x: jnp.exp(w[x] - m_loc[x]) for x in jd}
            kb = [kk[0, rows_of(c), :].astype(BF16) for c in cs]
            vat = [vt_ref[:, rows_of(base + c)] for c in cs]
            lhs = {(j, d): (vat[j] * ew[(j, d)]).astype(BF16) for j, d in jd}
            out = {(j, d): _dot(lhs[(j, d)], kb[j]) for j, d in jd}
            for j, d in jd:
                c_ref[d, base + cs[j]] = out[(j, d)]
                st_ref[d, base + cs[j], B_TOT:B_TOT + 1, :] = jnp.broadcast_to(gates[j][d][2], (1, LANES))
                st_ref[d, base + cs[j], M_LOC:M_LOC + 1, :] = jnp.broadcast_to(m_loc[(j, d)], (1, LANES))
        _grouped_loop(n, local)

    def advance(d, idx, state):
        c_prev, m_prev = state
        c_loc = c_ref[d, idx]
        b_tot = st_ref[d, idx, B_TOT:B_TOT + 1, :]
        m_loc = st_ref[d, idx, M_LOC:M_LOC + 1, :]
        c_ref[d, idx] = c_prev
        st_ref[d, idx, M_PREV:M_PREV + 1, :] = m_prev
        m_new = jnp.maximum(b_tot + m_prev, m_loc)
        return (jnp.exp(b_tot + m_prev - m_new) * c_prev + jnp.exp(m_loc - m_new) * c_loc, m_new)

    s0 = (jnp.zeros((LANES, LANES), F32), jnp.zeros((1, LANES), F32))
    states = (s0, s0)
    for base, n, *_ in segs:
        def scan_step(i, states, base=base, n=n):
            return (advance(0, base + i, states[0]), advance(1, base + n - 1 - i, states[1]))
        states = lax.fori_loop(0, n, scan_step, states)

    for base, n, qq, kk, _, oo, zz, gg, ggt, out_ref in segs:
        if out_ref is None:
            continue

        def outputs(cs, base=base, qq=qq, kk=kk, oo=oo, zz=zz, gg=gg, ggt=ggt, out_ref=out_ref):
            js = range(len(cs))
            jd = [(j, d) for j in js for d in range(2)]
            qb = [qq[0, rows_of(c), :].astype(BF16) for c in cs]
            sq = [_dot_nt(jnp.concatenate([kk[0, rows_of(c), :].astype(BF16),
                                           c_ref[0, base + c].astype(BF16),
                                           c_ref[1, base + c].astype(BF16)], axis=0), qb[j])
                  for j, c in enumerate(cs)]
            gcol = [gg[0, rows_of(c), :] for c in cs]
            b_cols = [_exact_left(tri_ref[...], _log_sigmoid(g)) for g in gcol]
            gates = [row_gates(ggt[0, :, rows_of(c)]) for c in cs]
            r_col = {(j, d): gcol[j][:, d:d + 1] - b_cols[j][d * L:(d + 1) * L, 2 + d:3 + d]
                     for j, d in jd}
            valid = [rows_i <= cols_i, rows_i >= cols_i]
            dm = {(j, d): jnp.where(valid[d], gates[j][d][1] + r_col[(j, d)], NEG) for j, d in jd}
            inter = {(j, d): gates[j][d][1] + st_ref[d, base + cs[j], M_PREV:M_PREV + 1, 0:1] for j, d in jd}
            m_t = {x: jnp.maximum(jnp.max(dm[x], axis=0, keepdims=True), inter[x]) for x in jd}
            p = {(j, d): sq[j][:L] * jnp.exp(dm[(j, d)] - m_t[(j, d)]) for j, d in jd}
            e_in = {x: jnp.exp(inter[x] - m_t[x]) for x in jd}
            vat = [vt_ref[:, rows_of(base + c)].astype(BF16) for c in cs]
            pv = {(j, d): _dot(vat[j], p[(j, d)].astype(BF16)) for j, d in jd}
            hts = []
            for j in js:
                ht = jnp.zeros((LANES, L), F32)
                for d in range(2):
                    x = (j, d)
                    qct = sq[j][(1 + d) * L:(2 + d) * L]
                    num = pv[x] + e_in[x] * qct
                    den = jnp.sum(p[x], axis=0, keepdims=True) + e_in[x] * qct[ONE_LANE:ONE_LANE + 1, :]
                    ht = ht + num / jnp.maximum(jnp.abs(den), jnp.exp(-m_t[x]))
                hts.append(jnp.where(sub < DH_B, ht, 0.0))
            ms = [jnp.sum(ht * ht, axis=0, keepdims=True) * (1.0 / DH_B) for ht in hts]
            h = [(ht * lax.rsqrt(m + EPS)).T for ht, m in zip(hts, ms)]
            for j, c in enumerate(cs):
                rows = rows_of(c)
                out_ref[0, rows, :] = _sigmoid(oo[0, rows, :]) * (h[j] * gn) * _silu(zz[0, rows, :])
        _grouped_loop(n, outputs)


def _mlstm_mixer(pb, pg, pgt, pbc, pgc, pgtc, gn, tris, need_ctx):
    b, t, _ = pb.shape
    tc = pbc.shape[1]
    tri, trit = tris
    lat = lambda part: pl.BlockSpec((1, t, LANES), lambda i, h, part=part: (i, 0, part * H_B + h))
    ctx = lambda part: pl.BlockSpec((1, tc, LANES), lambda i, h, part=part: (i, 0, part * H_B + h))
    in_specs = ([lat(p) for p in range(5)] + [ctx(p) for p in range(5)] +
                [pl.BlockSpec((1, t, LANES), lambda i, h: (i, 0, h)),
                 pl.BlockSpec((1, 8, t), lambda i, h: (i, h, 0)),
                 pl.BlockSpec((1, tc, LANES), lambda i, h: (i, 0, h)),
                 pl.BlockSpec((1, 8, tc), lambda i, h: (i, h, 0)),
                 pl.BlockSpec((1, 1, LANES), lambda i, h: (h, 0, 0)),
                 pl.BlockSpec(tri.shape, lambda i, h: (0, 0)),
                 pl.BlockSpec(trit.shape, lambda i, h: (0, 0))])
    out_specs = [pl.BlockSpec((1, t, LANES), lambda i, h: (i, 0, h))]
    out_shape = [jax.ShapeDtypeStruct((b, t, H_B * LANES), F32)]
    n_chunks = (t + tc) // CHUNK
    scratch = [pltpu.VMEM((2, n_chunks, LANES, LANES), F32),
               pltpu.VMEM((2, n_chunks, 8, LANES), F32),
               pltpu.VMEM((LANES, t + tc), F32)]
    if need_ctx:
        out_specs.append(pl.BlockSpec((1, tc, LANES), lambda i, h: (i, 0, h)))
        out_shape.append(jax.ShapeDtypeStruct((b, tc, H_B * LANES), F32))
    res = pl.pallas_call(
        functools.partial(_mlstm_kernel, need_ctx=need_ctx, n_lat=t // CHUNK, n_ctx=tc // CHUNK),
        grid=(b, H_B),
        in_specs=in_specs,
        out_specs=out_specs,
        out_shape=out_shape,
        scratch_shapes=scratch,
        compiler_params=_cparams("arbitrary", "arbitrary"),
        name="mlstm_mixer_ctx" if need_ctx else "mlstm_mixer",
    )(pb, pb, pb, pb, pb, pbc, pbc, pbc, pbc, pbc, pg, pgt, pgc, pgtc, gn, tri, trit)
    return (res[0], res[1]) if need_ctx else (res[0], None)


def _natten_kernel(*refs, need_ctx, n_rows):
    if need_ctx:
        (q_ref, k_ref, v_ref, z_ref, kc_ref, vc_ref, bias_ref, qc_ref, zc_ref, y_ref, yc_ref,
         kb_ref, vt_ref, kcb_ref, vct_ref) = refs
    else:
        (q_ref, k_ref, v_ref, z_ref, kc_ref, vc_ref, bias_ref, y_ref,
         kb_ref, vt_ref, kcb_ref, vct_ref) = refs
    lane = lax.broadcasted_iota(jnp.int32, (1, LANES), 1)
    hm0 = lane < DH_C
    scale = DH_C ** -0.5
    win = WIN_ROWS * GRID_W
    t = k_ref.shape[1]
    tc = kc_ref.shape[1]
    blk = LANES

    nb = t // blk
    group = NA_ROW_UNROLL

    def blocks(i):
        return [pl.ds(pl.multiple_of((i * group + j) * blk, blk), blk) for j in range(group)]

    def stage(i, carry):
        tiles = [v_ref[0, rows, :].T.astype(BF16) for rows in blocks(i)]
        for rows, x in zip(blocks(i), tiles):
            kb_ref[rows, :] = k_ref[0, rows, :].astype(BF16)
            vt_ref[0, :, rows] = x
        return carry

    lax.fori_loop(0, nb // group, stage, 0)

    def shifted(start):
        src = start + GRID_W
        if not isinstance(src, int):
            src = pl.multiple_of(src, GRID_W)
        return v_ref[0, pl.ds(src, blk), :].T.astype(BF16)

    def stage_shifted(i, carry):
        tiles = [shifted(rows.start) for rows in blocks(i)]
        for rows, x in zip(blocks(i), tiles):
            vt_ref[1, :, rows] = x
        return carry

    lax.fori_loop(0, nb // group - 1, stage_shifted, 0)
    tail = [(nb - group + j) * blk for j in range(group - 1)]
    tiles = [shifted(start) for start in tail]
    for start, x in zip(tail, tiles):
        vt_ref[1, :, start:start + blk] = x
    last = v_ref[0, t - blk:t, :].T
    vt_ref[1, :, t - blk:t] = pltpu.roll(last, GRID_W, 1).astype(BF16)

    for i in range(tc // blk):
        kcb_ref[i * blk:(i + 1) * blk, :] = kc_ref[0, i * blk:(i + 1) * blk, :].astype(BF16)
    tiles = [vc_ref[0, i * blk:(i + 1) * blk, :].T.astype(BF16) for i in range(tc // blk)]
    for i, x in enumerate(tiles):
        vct_ref[:, i * blk:(i + 1) * blk] = x

    def attend(qs, keys, vals_t, biases):
        n = qs[0].shape[0]
        q2 = [jnp.concatenate([jnp.where(hm0, q, 0.0), jnp.where(hm0, 0.0, q)], axis=0).astype(BF16)
              for q in qs]
        lt = [_dot_nt(k, q) for k, q in zip(keys, q2)]
        lt = [x if b is None else jnp.concatenate([x[:b.shape[0]] + b, x[b.shape[0]:]], axis=0)
              for x, b in zip(lt, biases)]
        p = [jnp.exp(x - jnp.max(x, axis=0, keepdims=True)) for x in lt]
        den = [jnp.sum(x, axis=0, keepdims=True) for x in p]
        ot = [_dot(v, x.astype(BF16)) for v, x in zip(vals_t, p)]
        o = [(x / d).T for x, d in zip(ot, den)]
        return [jnp.where(hm0, x[:n], x[n:]) for x in o]

    def row_group(i, carry):
        rows = [i * NA_ROW_UNROLL + j for j in range(NA_ROW_UNROLL)]
        rs = [jnp.clip(r - WIN_ROWS // 2, 0, n_rows - WIN_ROWS) for r in rows]
        qrows = [pl.ds(pl.multiple_of(r * GRID_W, GRID_W), GRID_W) for r in rows]
        keys, vals_t, biases = [], [], []
        for r, s in zip(rows, rs):
            par = s % 2
            wrows = pl.ds(pl.multiple_of(s * GRID_W, GRID_W), win)
            wlanes = pl.ds(pl.multiple_of((s - par) * GRID_W, LANES), win)
            keys.append(jnp.concatenate([kb_ref[wrows, :], kcb_ref[...]], axis=0))
            vals_t.append(jnp.concatenate([vt_ref[par, :, wlanes], vct_ref[...]], axis=1))
            dr0 = s - r + (WIN_ROWS - 1)
            biases.append(jnp.concatenate([bias_ref[dr0 + j, 0] for j in range(WIN_ROWS)], axis=0))
        outs = attend([q_ref[0, qr, :] * scale for qr in qrows], keys, vals_t, biases)
        for qr, o in zip(qrows, outs):
            y_ref[0, qr, :] = o * _silu(z_ref[0, qr, :])
        return carry

    lax.fori_loop(0, n_rows // NA_ROW_UNROLL, row_group, 0)

    if need_ctx:
        o, = attend([qc_ref[0] * scale], [kcb_ref[...]], [vct_ref[...]], [None])
        yc_ref[0] = o * _silu(zc_ref[0])


def _natten_mixer(pc, pcc, bias, need_ctx):
    b, t, _ = pc.shape
    tc = pcc.shape[1]
    npair = W_C // LANES
    n_rows = t // GRID_W
    lat = lambda part: pl.BlockSpec((1, t, LANES), lambda i, p, part=part: (i, 0, part * npair + p))
    ctx = lambda part: pl.BlockSpec((1, tc, LANES), lambda i, p, part=part: (i, 0, part * npair + p))
    in_specs = [lat(0), lat(1), lat(2), lat(3), ctx(1), ctx(2),
                pl.BlockSpec((2 * WIN_ROWS - 1, 1, GRID_W, LANES), lambda i, p: (0, p, 0, 0))]
    args = [pc, pc, pc, pc, pcc, pcc, bias]
    out_specs = [pl.BlockSpec((1, t, LANES), lambda i, p: (i, 0, p))]
    out_shape = [jax.ShapeDtypeStruct((b, t, W_C), F32)]
    if need_ctx:
        in_specs += [ctx(0), ctx(3)]
        args += [pcc, pcc]
        out_specs.append(pl.BlockSpec((1, tc, LANES), lambda i, p: (i, 0, p)))
        out_shape.append(jax.ShapeDtypeStruct((b, tc, W_C), F32))
    scratch = [pltpu.VMEM((t, LANES), BF16), pltpu.VMEM((2, LANES, t), BF16),
               pltpu.VMEM((tc, LANES), BF16), pltpu.VMEM((LANES, tc), BF16)]
    res = pl.pallas_call(
        functools.partial(_natten_kernel, need_ctx=need_ctx, n_rows=n_rows),
        grid=(b, npair),
        in_specs=in_specs,
        out_specs=out_specs,
        out_shape=out_shape,
        scratch_shapes=scratch,
        compiler_params=_cparams("arbitrary", "arbitrary"),
        name="natten_mixer_ctx" if need_ctx else "natten_mixer",
    )(*args)
    return (res[0], res[1]) if need_ctx else (res[0], None)


def _natten_bias(rpb):
    col = np.arange(GRID_W)
    cs = np.clip(col - WIN_COLS // 2, 0, GRID_W - WIN_COLS)
    band = (col[None, :] >= cs[:, None]) & (col[None, :] < cs[:, None] + WIN_COLS)
    dc = np.clip(col[None, :] - col[:, None] + WIN_COLS - 1, 0, 2 * WIN_COLS - 2)
    tz = jnp.where(band[None, None], rpb.astype(F32)[:, :, dc], NEG)
    tab = tz.reshape(H_C // 2, 2, 2 * WIN_ROWS - 1, GRID_W, GRID_W)
    tab = jnp.transpose(tab, (2, 0, 4, 1, 3))
    return tab.reshape(2 * WIN_ROWS - 1, H_C // 2, GRID_W, 2 * GRID_W)


def _outproj_kernel(ya_ref, yb_ref, yc_ref, x_ref, gt_ref, g_ref, wa_ref, wb_ref, wc_ref, o_ref):
    u = (_dot(ya_ref[0].astype(BF16), wa_ref[...]) + _dot(yb_ref[0].astype(BF16), wb_ref[...])
         + _dot(yc_ref[0].astype(BF16), wc_ref[...]))
    ms = jnp.mean(u * u, axis=-1, keepdims=True)
    o_ref[0] = x_ref[0] + gt_ref[0] * (u * lax.rsqrt(ms + EPS) * g_ref[...])


def _out_projection(ya, yb, yc, x, gt, g, wts, tm):
    b, t, d = x.shape
    wa, wb, wc = wts
    bm = gt.shape[0]
    mod_map = (lambda i, j: (i, 0, 0)) if bm > 1 else (lambda i, j: (0, 0, 0))
    const = lambda i, j: (0, 0)
    tile = lambda n: pl.BlockSpec((1, tm, n), lambda i, j: (i, j, 0))
    return pl.pallas_call(
        _outproj_kernel,
        grid=(b, t // tm),
        in_specs=[tile(ya.shape[2]), tile(yb.shape[2]), tile(yc.shape[2]), tile(d),
                  pl.BlockSpec((1, 1, d), mod_map),
                  pl.BlockSpec((1, d), const),
                  pl.BlockSpec(wa.shape, const), pl.BlockSpec(wb.shape, const), pl.BlockSpec(wc.shape, const)],
        out_specs=tile(d),
        out_shape=jax.ShapeDtypeStruct((b, t, d), F32),
        compiler_params=_cparams("arbitrary", "arbitrary"),
        name="out_projection",
    )(ya, yb, yc, x, gt, g.reshape(1, d), wa, wb, wc)


def _mlstm_weights(w):
    d = w.shape[0]
    q = DH_B // 4
    blk = w[:, OFF_B:OFF_G].reshape(d, 5, H_B, DH_B)
    gap = jnp.zeros((d, 2, H_B, LANES // 2 - 2 * q), w.dtype)
    qk = blk[:, :2]
    qk = jnp.concatenate([qk[..., 0:q], qk[..., 2 * q:3 * q], gap, qk[..., q:2 * q], qk[..., 3 * q:], gap], axis=-1)
    rest = jnp.pad(blk[:, 2:], ((0, 0), (0, 0), (0, 0), (0, LANES - DH_B)))
    wb = jnp.concatenate([qk, rest], axis=1).reshape(d, 5 * H_B * LANES)
    g = jnp.swapaxes(w[:, OFF_G:OFF_C].reshape(d, 4, H_B), 1, 2)
    wg = jnp.pad(g, ((0, 0), (0, 0), (0, LANES - 4))).reshape(d, H_B * LANES)
    wgt = jnp.pad(g, ((0, 0), (0, 0), (0, 4))).reshape(d, H_B * 8).T
    return wb.astype(BF16), wg.astype(BF16), wgt.astype(BF16)


def _rope_tables(t):
    pos = np.arange(t)
    q = DH_B // 4
    inv = ROPE_BASE ** (-jnp.arange(0, 2 * q, 2, dtype=F32) / (2 * q))
    ang_r = jnp.asarray(pos // GRID_W, F32)[:, None] * inv[None, :]
    ang_c = jnp.asarray(pos % GRID_W, F32)[:, None] * inv[None, :]
    zeros = jnp.zeros((t, LANES // 2 - 2 * q), F32)
    cos_half = jnp.concatenate([jnp.cos(ang_r), jnp.cos(ang_c), zeros], axis=1)
    sin_half = jnp.concatenate([jnp.sin(ang_r), jnp.sin(ang_c), zeros], axis=1)
    return (jnp.concatenate([cos_half, cos_half], axis=1),
            jnp.concatenate([-sin_half, sin_half], axis=1))


def _pad_heads(v):
    v = v.reshape(v.shape[:-1] + (H_B, DH_B))
    v = jnp.pad(v, [(0, 0)] * (v.ndim - 1) + [(0, LANES - DH_B)])
    return v.reshape(v.shape[:-2] + (H_B * LANES,))


def kernel(x, c, ctx, c_ctx, w_mod, b_mod, g_pre, g_post, w_in, w_out, hgrn_lb, hgrn_gn, mlstm_gate_b, mlstm_gn, na_rpb):
    depth = w_in.shape[0]
    b, t, d = x.shape
    tc = ctx.shape[1]
    assert t % CHUNK == 0 and tc % CHUNK == 0 and t // GRID_W >= WIN_ROWS and t % GRID_W == 0
    assert w_in.shape[2] == P_IN and (1 << N_LEVELS) == CHUNK

    lb_cum = jnp.cumsum(jax.nn.softmax(hgrn_lb.astype(F32), axis=0), axis=0)
    lb_all = lb_cum - lb_cum[0]
    hgrn_tabs = _hgrn_tables()
    tris = _tri_tables()
    rope = _rope_tables(t)
    mod_rows = 16
    cc = jnp.concatenate([c.astype(F32), c_ctx.astype(F32)[None],
                          jnp.zeros((mod_rows - b - 1, d), F32)], axis=0)
    tm = min(256, t)
    tmc = min(256, tc)

    for l in range(depth):
        need_ctx = l < depth - 1
        mod = _modulation(cc, w_mod[l], b_mod[l])
        sh, sc, gt = (mod[:b, i * d:(i + 1) * d].reshape(b, 1, d) for i in range(3))
        shc, scc, gtc = (mod[b:b + 1, i * d:(i + 1) * d].reshape(1, 1, d) for i in range(3))

        w = w_in[l]
        gb = mlstm_gate_b[l].astype(F32)
        gbr = jnp.pad(gb.T, ((0, 0), (0, LANES - 4))).reshape(1, H_B * LANES)
        gbc = jnp.pad(gb.T, ((0, 0), (0, 4))).reshape(H_B * 8, 1)
        wb, wg, wgt = _mlstm_weights(w)
        wts = (w[:, :OFF_B].astype(BF16), wb, w[:, OFF_C:].astype(BF16), wg, wgt, gbr, gbc)

        pa, pb, pc, pg, pgt = _in_projection(x, sh, sc, g_pre[l], wts, rope, tm)
        pac, pbc, pcc, pgc, pgtc = _in_projection(ctx, shc, scc, g_pre[l], wts, None, tmc)

        ya, yac = _hgrn_mixer(pa, pac, lb_all[l], hgrn_gn[l], hgrn_tabs, need_ctx)
        gnb = _pad_heads(mlstm_gn[l].astype(F32)).reshape(H_B, 1, LANES)
        yb, ybc = _mlstm_mixer(pb, pg, pgt, pbc, pgc, pgtc, gnb, tris, need_ctx)
        yc, ycc = _natten_mixer(pc, pcc, _natten_bias(na_rpb[l]), need_ctx)

        wo = w_out[l]
        wo_b = jnp.pad(wo[W_A:W_A + W_B].reshape(H_B, DH_B, d), ((0, 0), (0, LANES - DH_B), (0, 0)))
        wo_parts = (wo[:W_A].astype(BF16), wo_b.reshape(H_B * LANES, d).astype(BF16),
                    wo[W_A + W_B:].astype(BF16))
        x = _out_projection(ya, yb, yc, x, gt, g_post[l], wo_parts, tm)
        if need_ctx:
            ctx = _out_projection(yac, ybc, ycc, ctx, gtc, g_post[l], wo_parts, tmc)
    return x
```

```python
import functools
import math

import numpy as np
import jax
import jax.numpy as jnp
from jax import lax
from jax.experimental import pallas as pl
from jax.experimental.pallas import tpu as pltpu

F32 = jnp.float32
BF16 = jnp.bfloat16

LANES = 128
GRID_W = 64
W_A, H_A, DH_A = 256, 4, 64
W_B, H_B, DH_B = 384, 4, 96
W_C, H_C, DH_C = 384, 6, 64
WIN_ROWS, WIN_COLS = 8, 16
ROPE_BASE = 10000.0
EPS = 1e-6
NEG = -1e30
LOG2_E = math.log2(math.e)
CHUNK = 128
N_LEVELS = 7
NA_ROW_UNROLL = 8
NA_STAGE_GROUP = 4
CHUNK_UNROLL = 4
MLSTM_GROUP = 8
OFF_B = 5 * W_A
OFF_G = OFF_B + 5 * W_B
OFF_C = OFF_G + 4 * H_B
P_IN = OFF_C + 4 * W_C
ONE_LANE = DH_B
VMEM_LIMIT = 56 * 1024 * 1024


def _cparams(*sem):
    return pltpu.CompilerParams(dimension_semantics=sem, vmem_limit_bytes=VMEM_LIMIT)


def _dot(a, b):
    return jnp.dot(a, b, preferred_element_type=F32)


def _dot_nt(a, b):
    return lax.dot_general(a, b, (((1,), (1,)), ((), ())), preferred_element_type=F32)


def _dot_tn(a, b):
    return lax.dot_general(a, b, (((0,), (0,)), ((), ())), preferred_element_type=F32)


def _chunk_loop(n, body):
    u = math.gcd(n, CHUNK_UNROLL)

    def step(i, carry):
        for j in range(u):
            body(i * u + j, carry)
        return carry

    return lax.fori_loop(0, n // u, step, 0)


def _grouped_loop(n, body, group=CHUNK_UNROLL):
    u = math.gcd(n, group)

    def step(i, carry):
        body([i * u + j for j in range(u)])
        return carry

    return lax.fori_loop(0, n // u, step, 0)


def _split3(x):
    hi = x.astype(BF16)
    r = x - hi.astype(F32)
    mid = r.astype(BF16)
    lo = (r - mid.astype(F32)).astype(BF16)
    return hi, mid, lo


def _exact_left(t01, x):
    n = x.shape[1]
    r = _dot(t01, jnp.concatenate(_split3(x), axis=1))
    return r[:, :n] + r[:, n:2 * n] + r[:, 2 * n:]


def _exact_right(x, t01):
    m = x.shape[0]
    pieces = jnp.concatenate([p.astype(F32) for p in _split3(x)], axis=0).astype(BF16)
    r = _dot(pieces, t01)
    return r[:m] + r[m:2 * m] + r[2 * m:]


def _sigmoid(x):
    return 1.0 / (1.0 + jnp.exp(-x))


def _silu(x):
    return x * _sigmoid(x)


def _log_sigmoid(x):
    return jnp.minimum(x, 0.0) - jnp.log(1.0 + jnp.exp(-jnp.abs(x)))


def _mod_kernel(c_ref, w_ref, b_ref, o_ref):
    s = _silu(c_ref[...])
    o_ref[...] = _dot(s.astype(BF16), w_ref[...].astype(BF16)) + b_ref[...]


def _modulation(cc, w, b):
    rows, d = cc.shape
    n = w.shape[1]
    tn = d
    assert n % tn == 0
    return pl.pallas_call(
        _mod_kernel,
        grid=(n // tn,),
        in_specs=[pl.BlockSpec((rows, d), lambda j: (0, 0)),
                  pl.BlockSpec((d, tn), lambda j: (0, j)),
                  pl.BlockSpec((1, tn), lambda j: (0, j))],
        out_specs=pl.BlockSpec((rows, tn), lambda j: (0, j)),
        out_shape=jax.ShapeDtypeStruct((rows, n), F32),
        compiler_params=_cparams("arbitrary"),
        name="modulation",
    )(cc, w, b.reshape(1, n))


def _inproj_kernel(*refs, rotary):
    if rotary:
        (x_ref, sh_ref, sc_ref, g_ref, wa_ref, wb_ref, wc_ref, wgt_ref, gbc_ref,
         cos_ref, sin_ref, pa_ref, pb_ref, pc_ref, pgt_ref) = refs
    else:
        (x_ref, sh_ref, sc_ref, g_ref, wa_ref, wb_ref, wc_ref, wgt_ref, gbc_ref,
         pa_ref, pb_ref, pc_ref, pgt_ref) = refs
    x = x_ref[0]
    ms = jnp.mean(x * x, axis=-1, keepdims=True)
    h = x * lax.rsqrt(ms + EPS) * g_ref[...]
    h = h * (1.0 + sc_ref[0]) + sh_ref[0]
    hb = h.astype(BF16)
    pa_ref[0] = _dot(hb, wa_ref[...])
    pc_ref[0] = _dot(hb, wc_ref[...])
    pgt_ref[0] = _dot_nt(wgt_ref[...], hb) + gbc_ref[...]
    hw = H_B * LANES
    scale = DH_B ** -0.5
    for part in range(5):
        p = _dot(hb, wb_ref[:, part * hw:(part + 1) * hw])
        if part == 0:
            p = p * scale
        if rotary and part < 2:
            cos = cos_ref[...]
            sin = sin_ref[...]
            for hd in range(H_B):
                ph = p[:, hd * LANES:(hd + 1) * LANES]
                pb_ref[0, :, part * hw + hd * LANES:part * hw + (hd + 1) * LANES] = (
                    ph * cos + pltpu.roll(ph, LANES // 2, 1) * sin)
        else:
            pb_ref[0, :, part * hw:(part + 1) * hw] = p


def _in_projection(x, sh, sc, g, wts, rope, tm):
    b, t, d = x.shape
    wa, wb, wc, wgt, gbc = wts
    bm = sh.shape[0]
    mod_map = (lambda i, j: (i, 0, 0)) if bm > 1 else (lambda i, j: (0, 0, 0))
    const = lambda i, j: (0, 0)
    in_specs = [pl.BlockSpec((1, tm, d), lambda i, j: (i, j, 0)),
                pl.BlockSpec((1, 1, d), mod_map),
                pl.BlockSpec((1, 1, d), mod_map),
                pl.BlockSpec((1, d), const),
                pl.BlockSpec(wa.shape, const),
                pl.BlockSpec(wb.shape, const),
                pl.BlockSpec(wc.shape, const),
                pl.BlockSpec(wgt.shape, const),
                pl.BlockSpec(gbc.shape, const)]
    args = [x, sh, sc, g.reshape(1, d), wa, wb, wc, wgt, gbc]
    if rope is not None:
        in_specs += [pl.BlockSpec((tm, LANES), lambda i, j: (j, 0))] * 2
        args += list(rope)
    na, nb, nc = wa.shape[1], wb.shape[1], wc.shape[1]
    out_specs = [pl.BlockSpec((1, tm, na), lambda i, j: (i, j, 0)),
                 pl.BlockSpec((1, tm, nb), lambda i, j: (i, j, 0)),
                 pl.BlockSpec((1, tm, nc), lambda i, j: (i, j, 0)),
                 pl.BlockSpec((1, wgt.shape[0], tm), lambda i, j: (i, 0, j))]
    out_shape = [jax.ShapeDtypeStruct((b, t, na), F32),
                 jax.ShapeDtypeStruct((b, t, nb), F32),
                 jax.ShapeDtypeStruct((b, t, nc), F32),
                 jax.ShapeDtypeStruct((b, wgt.shape[0], t), F32)]
    return pl.pallas_call(
        functools.partial(_inproj_kernel, rotary=rope is not None),
        grid=(b, t // tm),
        in_specs=in_specs,
        out_specs=out_specs,
        out_shape=out_shape,
        compiler_params=_cparams("arbitrary", "arbitrary"),
        name="in_projection_rope" if rope is not None else "in_projection",
    )(*args)


def _hgrn_tables():
    L, nl = CHUNK, N_LEVELS
    idx = np.arange(L)
    t = idx[:, None]
    u = idx[None, :]
    tabq, levels = [], []
    for rev in (False, True):
        tabq.append(((u <= t) if not rev else (u >= t)).astype(np.float32))
        x = t ^ u
        lvl = np.where(x > 0, np.floor(np.log2(np.maximum(x, 1))), float(nl))
        valid = (u <= t) if not rev else (u >= t)
        lvl = np.where(valid, lvl, -1.0).astype(np.float32)
        levels.append(np.concatenate([lvl, lvl], axis=1))
    return jnp.asarray(np.stack(tabq), dtype=BF16), jnp.asarray(np.stack(levels), dtype=F32)


def _tri_tables():
    L = CHUNK
    idx = np.arange(L)
    t = idx[:, None]
    u = idx[None, :]
    fwd = (u <= t).astype(np.float32)
    bwd = (u >= t).astype(np.float32)
    spread = np.zeros((16, 2 * L), np.float32)
    spread[0:3, :L] = 1.0
    spread[3:6, L:] = 1.0
    return (jnp.asarray(np.concatenate([fwd.T, bwd.T], axis=1), dtype=BF16), jnp.asarray(spread, dtype=BF16))


def _hgrn_kernel(*refs, need_ctx, n_lat, n_ctx):
    (q_ref, ff_ref, fb_ref, i_ref, z_ref, qc_ref, ffc_ref, fbc_ref, ic_ref, zc_ref,
     lb_ref, gn_ref, tabq_ref, lvl_ref) = refs[:14]
    if need_ctx:
        y_ref, yc_ref, u_ref, dec_ref, gate_ref = refs[14:]
    else:
        y_ref, u_ref, dec_ref, gate_ref = refs[14:]
        yc_ref = None
    L, nl = CHUNK, N_LEVELS
    lane = lax.broadcasted_iota(jnp.int32, (1, LANES), 1)
    hm0 = lane < DH_A
    row_h = lax.broadcasted_iota(jnp.int32, (LANES, LANES), 0) // DH_A
    col_h = lax.broadcasted_iota(jnp.int32, (LANES, LANES), 1) // DH_A
    same_head = row_h == col_h
    gn = gn_ref[0]
    segs = ((0, n_ctx, qc_ref, (ffc_ref, fbc_ref), ic_ref, zc_ref, yc_ref),
            (n_ctx, n_lat, q_ref, (ff_ref, fb_ref), i_ref, z_ref, y_ref))

    def decay(fl, lbd):
        a0 = jnp.log(lbd)
        b0 = jnp.log(1.0 - lbd) + _log_sigmoid(fl)
        logf = jnp.maximum(a0, b0) + jnp.log(1.0 + jnp.exp(-jnp.abs(a0 - b0)))
        return logf * LOG2_E, (1.0 - lbd) * _sigmoid(-fl)

    def rows_of(c):
        return pl.ds(pl.multiple_of(c * L, L), L)

    tok = lax.broadcasted_iota(jnp.int32, (L, 1), 0)

    def level_exponents(logf, bcum, d):
        prev, nxt = pltpu.roll(logf, 1, 0), pltpu.roll(logf, L - 1, 0)
        if d == 0:
            eqs = [logf, logf + jnp.where((tok % 4) == 3, prev, 0.0)]
            eks = [None, jnp.where((tok % 4) == 0, nxt, 0.0)]
        else:
            eqs = [logf, logf + jnp.where((tok % 4) == 0, nxt, 0.0)]
            eks = [None, jnp.where((tok % 4) == 3, prev, 0.0)]
        for lv in range(2, nl):
            h = 1 << lv
            r = h - 1 if d == 0 else h
            g = jnp.concatenate([jnp.broadcast_to(bcum[blk + r:blk + r + 1], (2 * h, LANES))
                                 for blk in range(0, L, 2 * h)], axis=0)
            eqs.append(bcum - g)
            eks.append(g - bcum)
        return eqs, eks

    def chunk_total(bcum, d):
        return bcum[L - 1:L] if d == 0 else bcum[0:1]

    for base, n, _, f_refs, ai_ref, _, _ in segs:
        def increments(cs, base=base, f_refs=f_refs, ai_ref=ai_ref):
            jd = [(j, d) for j in range(len(cs)) for d in range(2)]
            lk = {(j, d): decay(f_refs[d][0, rows_of(cs[j]), :], lb_ref[d, 0]) for j, d in jd}
            bcum = {(j, d): _exact_left(tabq_ref[d], lk[(j, d)][0]) for j, d in jd}
            b_tot = {(j, d): chunk_total(bcum[(j, d)], d) for j, d in jd}
            vb = [ai_ref[0, rows_of(c), :].astype(BF16) for c in cs]
            kr = {x: (lk[x][1] * jnp.exp2(b_tot[x] - bcum[x])).astype(BF16) for x in jd}
            u = {(j, d): _dot_tn(kr[(j, d)], vb[j]) for j, d in jd}
            dec = {x: jnp.broadcast_to(jnp.exp2(b_tot[x]), (LANES, LANES)).T for x in jd}
            for j, d in jd:
                u_ref[d, base + cs[j]] = jnp.where(same_head, u[(j, d)], 0.0)
                dec_ref[d, base + cs[j]] = dec[(j, d)]
                tok_rows = rows_of(base + cs[j])
                gate_ref[d, 0, tok_rows, :] = lk[(j, d)][0]
                gate_ref[d, 1, tok_rows, :] = lk[(j, d)][1]
                gate_ref[d, 2, tok_rows, :] = bcum[(j, d)]
        _grouped_loop(n, increments)

    def advance(d, idx, s):
        u = u_ref[d, idx]
        u_ref[d, idx] = s
        return s * dec_ref[d, idx] + u

    states = (jnp.zeros((LANES, LANES), F32),) * 2
    for base, n, *_ in segs:
        def scan_step(i, states, base=base, n=n):
            return (advance(0, base + i, states[0]), advance(1, base + n - 1 - i, states[1]))
        states = lax.fori_loop(0, n, scan_step, states)

    def finish(o, z):
        sq = o * o
        ms0 = jnp.sum(jnp.where(hm0, sq, 0.0), axis=-1, keepdims=True)
        ms1 = jnp.sum(jnp.where(hm0, 0.0, sq), axis=-1, keepdims=True)
        ms = jnp.where(hm0, ms0, ms1) * (1.0 / DH_A)
        return o * lax.rsqrt(ms + EPS) * gn * _silu(z)

    for base, n, aq_ref, f_refs, ai_ref, zz_ref, out_ref in segs:
        if out_ref is None:
            continue

        def outputs(cs, base=base, aq_ref=aq_ref, f_refs=f_refs, ai_ref=ai_ref, zz_ref=zz_ref,
                    out_ref=out_ref):
            js = range(len(cs))
            jd = [(j, d) for j in js for d in range(2)]
            q = [_silu(aq_ref[0, rows_of(c), :]) * (DH_A ** -0.5) for c in cs]
            lk = {(j, d): (gate_ref[d, 0, rows_of(base + cs[j]), :], gate_ref[d, 1, rows_of(base + cs[j]), :])
                  for j, d in jd}
            bcum = {(j, d): gate_ref[d, 2, rows_of(base + cs[j]), :] for j, d in jd}
            ex = {(j, d): level_exponents(lk[(j, d)][0], bcum[(j, d)], d) for j, d in jd}
            a = {x: jnp.zeros((L, 2 * L), F32) for x in jd}
            zero = jnp.zeros((DH_A, L), BF16)
            for lv in range(nl + 1):
                at_level = [lvl_ref[d] == float(lv) for d in range(2)]
                for j, d in jd:
                    k = lk[(j, d)][1]
                    eqs, eks = ex[(j, d)]
                    if lv < nl:
                        qt = q[j] * jnp.exp2(eqs[lv])
                        kl = k if eks[lv] is None else k * jnp.exp2(eks[lv])
                    else:
                        qt, kl = q[j], k
                    ktl = kl.T.astype(BF16)
                    w = jnp.concatenate([jnp.concatenate([ktl[:DH_A], zero], axis=0),
                                         jnp.concatenate([zero, ktl[DH_A:]], axis=0)], axis=1)
                    a[(j, d)] = jnp.where(at_level[d], _dot(qt.astype(BF16), w), a[(j, d)])
            ai = [ai_ref[0, rows_of(c), :] for c in cs]
            vs = [jnp.concatenate([jnp.where(hm0, x, 0.0), jnp.where(hm0, 0.0, x)], axis=0).astype(BF16)
                  for x in ai]
            intra = {(j, d): _dot(a[(j, d)].astype(BF16), vs[j]) for j, d in jd}
            inter = {(j, d): _dot((q[j] * jnp.exp2(bcum[(j, d)])).astype(BF16),
                                  u_ref[d, base + cs[j]].astype(BF16)) for j, d in jd}
            for j, c in enumerate(cs):
                o = intra[(j, 0)] + inter[(j, 0)] + intra[(j, 1)] + inter[(j, 1)]
                out_ref[0, rows_of(c), :] = finish(o, zz_ref[0, rows_of(c), :])
        _grouped_loop(n, outputs)


def _hgrn_mixer(pa, pac, lb, gn, tabs, need_ctx):
    b, t, _ = pa.shape
    tc = pac.shape[1]
    npair = W_A // LANES
    n_chunks = (t + tc) // CHUNK
    lat = lambda part: pl.BlockSpec((1, t, LANES), lambda i, p, part=part: (i, 0, part * npair + p))
    ctx = lambda part: pl.BlockSpec((1, tc, LANES), lambda i, p, part=part: (i, 0, part * npair + p))
    whole = lambda a: pl.BlockSpec(a.shape, lambda i, p: (0, 0, 0))
    in_specs = ([lat(0), lat(1), lat(2), lat(3), lat(4), ctx(0), ctx(1), ctx(2), ctx(3), ctx(4),
                 pl.BlockSpec((2, 1, 1, LANES), lambda i, p: (0, p, 0, 0)),
                 pl.BlockSpec((1, 1, LANES), lambda i, p: (p, 0, 0))] + [whole(a) for a in tabs])
    out_specs = [pl.BlockSpec((1, t, LANES), lambda i, p: (i, 0, p))]
    out_shape = [jax.ShapeDtypeStruct((b, t, W_A), F32)]
    scratch = [pltpu.VMEM((2, n_chunks, LANES, LANES), F32)] * 2
    scratch.append(pltpu.VMEM((2, 3, t + tc, LANES), F32))
    if need_ctx:
        out_specs.append(pl.BlockSpec((1, tc, LANES), lambda i, p: (i, 0, p)))
        out_shape.append(jax.ShapeDtypeStruct((b, tc, W_A), F32))
    res = pl.pallas_call(
        functools.partial(_hgrn_kernel, need_ctx=need_ctx, n_lat=t // CHUNK, n_ctx=tc // CHUNK),
        grid=(b, npair),
        in_specs=in_specs,
        out_specs=out_specs,
        out_shape=out_shape,
        scratch_shapes=scratch,
        compiler_params=_cparams("arbitrary", "arbitrary"),
        name="hgrn2_mixer_ctx" if need_ctx else "hgrn2_mixer",
    )(pa, pa, pa, pa, pa, pac, pac, pac, pac, pac,
      lb.reshape(2, npair, 1, LANES), gn.reshape(npair, 1, LANES), *tabs)
    return (res[0], res[1]) if need_ctx else (res[0], None)


def _mlstm_kernel(*refs, need_ctx, n_lat, n_ctx):
    (q_ref, k_ref, v_ref, o_ref, z_ref, qc_ref, kc_ref, vc_ref, oc_ref, zc_ref,
     gt_ref, gtc_ref, gn_ref, trit_ref, ones_ref) = refs[:15]
    if need_ctx:
        y_ref, yc_ref, c_ref, st_ref, vt_ref = refs[15:]
    else:
        y_ref, c_ref, st_ref, vt_ref = refs[15:]
        yc_ref = None
    L = CHUNK
    sub = lax.broadcasted_iota(jnp.int32, (LANES, 1), 0)
    rows_i = lax.broadcasted_iota(jnp.int32, (L, L), 0)
    cols_i = lax.broadcasted_iota(jnp.int32, (L, L), 1)
    gn = gn_ref[0]
    segs = ((0, n_ctx, qc_ref, kc_ref, vc_ref, oc_ref, zc_ref, gtc_ref, yc_ref),
            (n_ctx, n_lat, q_ref, k_ref, v_ref, o_ref, z_ref, gt_ref, y_ref))
    B_TOT, M_LOC, M_PREV = 0, 1, 2

    def rows_of(c):
        return pl.ds(pl.multiple_of(c * L, L), L)

    loop = _chunk_loop

    def row_gates(grow):
        er = _exact_right(_log_sigmoid(grow), trit_ref[...])
        out = []
        for d in range(2):
            b_row = er[2 + d:3 + d, d * L:(d + 1) * L]
            out.append((grow[d:d + 1, :], b_row, b_row[:, L - 1:L] if d == 0 else b_row[:, 0:1]))
        return out

    for base, n, _, _, vv, *_ in segs:
        def transpose_values(c, carry, base=base, vv=vv):
            vt_ref[:, rows_of(base + c)] = jnp.where(sub == ONE_LANE, 1.0, vv[0, rows_of(c), :].T)
            return carry
        loop(n, transpose_values)

    for base, n, _, kk, _, _, _, ggt, _ in segs:
        def local(cs, base=base, kk=kk, ggt=ggt):
            jd = [(j, d) for j in range(len(cs)) for d in range(2)]
            gates = [row_gates(ggt[0, :, rows_of(c)]) for c in cs]
            w = {(j, d): gates[j][d][2] + gates[j][d][0] - gates[j][d][1] for j, d in jd}
            m_loc = {x: jnp.max(w[x], axis=1, keepdims=True) for x in jd}
            ew = {x: jnp.exp(w[x] - m_loc[x]) for x in jd}
            kb = [kk[0, rows_of(c), :].astype(BF16) for c in cs]
            vat = [vt_ref[:, rows_of(base + c)] for c in cs]
            lhs = {(j, d): (vat[j] * ew[(j, d)]).astype(BF16) for j, d in jd}
            out = {(j, d): _dot(lhs[(j, d)], kb[j]) for j, d in jd}
            for j, d in jd:
                c_ref[d, base + cs[j]] = out[(j, d)]
                st_ref[d, base + cs[j], B_TOT:B_TOT + 1, :] = jnp.broadcast_to(gates[j][d][2], (1, LANES))
                st_ref[d, base + cs[j], M_LOC:M_LOC + 1, :] = jnp.broadcast_to(m_loc[(j, d)], (1, LANES))
        _grouped_loop(n, local, MLSTM_GROUP)

    def advance(d, idx, state):
        c_prev, m_prev = state
        c_loc = c_ref[d, idx]
        b_tot = st_ref[d, idx, B_TOT:B_TOT + 1, :]
        m_loc = st_ref[d, idx, M_LOC:M_LOC + 1, :]
        c_ref[d, idx] = c_prev
        st_ref[d, idx, M_PREV:M_PREV + 1, :] = m_prev
        m_new = jnp.maximum(b_tot + m_prev, m_loc)
        return (jnp.exp(b_tot + m_prev - m_new) * c_prev + jnp.exp(m_loc - m_new) * c_loc, m_new)

    s0 = (jnp.zeros((LANES, LANES), F32), jnp.zeros((1, LANES), F32))
    states = (s0, s0)
    for base, n, *_ in segs:
        def scan_step(i, states, base=base, n=n):
            return (advance(0, base + i, states[0]), advance(1, base + n - 1 - i, states[1]))
        states = lax.fori_loop(0, n, scan_step, states)

    for base, n, qq, kk, _, oo, zz, ggt, out_ref in segs:
        if out_ref is None:
            continue

        def outputs(cs, base=base, qq=qq, kk=kk, oo=oo, zz=zz, ggt=ggt, out_ref=out_ref):
            js = range(len(cs))
            jd = [(j, d) for j in js for d in range(2)]
            qb = [qq[0, rows_of(c), :].astype(BF16) for c in cs]
            sq = [_dot_nt(jnp.concatenate([kk[0, rows_of(c), :].astype(BF16),
                                           c_ref[0, base + c].astype(BF16),
                                           c_ref[1, base + c].astype(BF16)], axis=0), qb[j])
                  for j, c in enumerate(cs)]
            gates = [row_gates(ggt[0, :, rows_of(c)]) for c in cs]
            pieces = [jnp.concatenate(
                [p.astype(F32) for d in range(2) for p in _split3(gates[j][d][0] - gates[j][d][1])]
                + [jnp.zeros((10, L), F32)], axis=0).astype(BF16) for j in js]
            r_bc = [_dot_tn(x, ones_ref[...]) for x in pieces]
            valid = [rows_i <= cols_i, rows_i >= cols_i]
            dm = {(j, d): jnp.where(valid[d], gates[j][d][1] + r_bc[j][:, d * L:(d + 1) * L], NEG)
                  for j, d in jd}
            inter = {(j, d): gates[j][d][1] + st_ref[d, base + cs[j], M_PREV:M_PREV + 1, 0:1] for j, d in jd}
            m_t = {x: jnp.maximum(jnp.max(dm[x], axis=0, keepdims=True), inter[x]) for x in jd}
            p = {(j, d): sq[j][:L] * jnp.exp(dm[(j, d)] - m_t[(j, d)]) for j, d in jd}
            e_in = {x: jnp.exp(inter[x] - m_t[x]) for x in jd}
            vat = [vt_ref[:, rows_of(base + c)].astype(BF16) for c in cs]
            pv = {(j, d): _dot(vat[j], p[(j, d)].astype(BF16)) for j, d in jd}
            hts = []
            for j in js:
                ht = jnp.zeros((LANES, L), F32)
                for d in range(2):
                    x = (j, d)
                    qct = sq[j][(1 + d) * L:(2 + d) * L]
                    num = pv[x] + e_in[x] * qct
                    den = jnp.sum(p[x], axis=0, keepdims=True) + e_in[x] * qct[ONE_LANE:ONE_LANE + 1, :]
                    ht = ht + num / jnp.maximum(jnp.abs(den), jnp.exp(-m_t[x]))
                hts.append(jnp.where(sub < DH_B, ht, 0.0))
            ms = [jnp.sum(ht * ht, axis=0, keepdims=True) * (1.0 / DH_B) for ht in hts]
            h = [(ht * lax.rsqrt(m + EPS)).T for ht, m in zip(hts, ms)]
            for j, c in enumerate(cs):
                rows = rows_of(c)
                out_ref[0, rows, :] = _sigmoid(oo[0, rows, :]) * (h[j] * gn) * _silu(zz[0, rows, :])
        _grouped_loop(n, outputs, MLSTM_GROUP)


def _mlstm_mixer(pb, pgt, pbc, pgtc, gn, tris, need_ctx):
    b, t, _ = pb.shape
    tc = pbc.shape[1]
    trit, spread = tris
    lat = lambda part: pl.BlockSpec((1, t, LANES), lambda i, h, part=part: (i, 0, part * H_B + h))
    ctx = lambda part: pl.BlockSpec((1, tc, LANES), lambda i, h, part=part: (i, 0, part * H_B + h))
    in_specs = ([lat(p) for p in range(5)] + [ctx(p) for p in range(5)] +
                [pl.BlockSpec((1, 8, t), lambda i, h: (i, h, 0)),
                 pl.BlockSpec((1, 8, tc), lambda i, h: (i, h, 0)),
                 pl.BlockSpec((1, 1, LANES), lambda i, h: (h, 0, 0)),
                 pl.BlockSpec(trit.shape, lambda i, h: (0, 0)),
                 pl.BlockSpec(spread.shape, lambda i, h: (0, 0))])
    out_specs = [pl.BlockSpec((1, t, LANES), lambda i, h: (i, 0, h))]
    out_shape = [jax.ShapeDtypeStruct((b, t, H_B * LANES), F32)]
    n_chunks = (t + tc) // CHUNK
    scratch = [pltpu.VMEM((2, n_chunks, LANES, LANES), F32),
               pltpu.VMEM((2, n_chunks, 8, LANES), F32),
               pltpu.VMEM((LANES, t + tc), F32)]
    if need_ctx:
        out_specs.append(pl.BlockSpec((1, tc, LANES), lambda i, h: (i, 0, h)))
        out_shape.append(jax.ShapeDtypeStruct((b, tc, H_B * LANES), F32))
    res = pl.pallas_call(
        functools.partial(_mlstm_kernel, need_ctx=need_ctx, n_lat=t // CHUNK, n_ctx=tc // CHUNK),
        grid=(b, H_B),
        in_specs=in_specs,
        out_specs=out_specs,
        out_shape=out_shape,
        scratch_shapes=scratch,
        compiler_params=_cparams("arbitrary", "arbitrary"),
        name="mlstm_mixer_ctx" if need_ctx else "mlstm_mixer",
    )(pb, pb, pb, pb, pb, pbc, pbc, pbc, pbc, pbc, pgt, pgtc, gn, trit, spread)
    return (res[0], res[1]) if need_ctx else (res[0], None)


def _natten_kernel(*refs, need_ctx, n_rows):
    if need_ctx:
        (q_ref, k_ref, v_ref, z_ref, kc_ref, vc_ref, bias_ref, qc_ref, zc_ref, y_ref, yc_ref,
         kb_ref, vt_ref, kcb_ref, vct_ref) = refs
    else:
        (q_ref, k_ref, v_ref, z_ref, kc_ref, vc_ref, bias_ref, y_ref,
         kb_ref, vt_ref, kcb_ref, vct_ref) = refs
    lane = lax.broadcasted_iota(jnp.int32, (1, LANES), 1)
    hm0 = lane < DH_C
    scale = DH_C ** -0.5
    win = WIN_ROWS * GRID_W
    t = k_ref.shape[1]
    tc = kc_ref.shape[1]
    blk = LANES

    nb = t // blk
    group = math.gcd(nb, NA_STAGE_GROUP)
    row_group_size = math.gcd(n_rows, NA_ROW_UNROLL)

    def blocks(i):
        return [pl.ds(pl.multiple_of((i * group + j) * blk, blk), blk) for j in range(group)]

    def stage(i, carry):
        tiles = [v_ref[0, rows, :].T.astype(BF16) for rows in blocks(i)]
        for rows, x in zip(blocks(i), tiles):
            kb_ref[rows, :] = k_ref[0, rows, :].astype(BF16)
            vt_ref[0, :, rows] = x
        return carry

    lax.fori_loop(0, nb // group, stage, 0)

    def shifted(start):
        src = start + GRID_W
        if not isinstance(src, int):
            src = pl.multiple_of(src, GRID_W)
        return v_ref[0, pl.ds(src, blk), :].T.astype(BF16)

    def stage_shifted(i, carry):
        tiles = [shifted(rows.start) for rows in blocks(i)]
        for rows, x in zip(blocks(i), tiles):
            vt_ref[1, :, rows] = x
        return carry

    lax.fori_loop(0, nb // group - 1, stage_shifted, 0)
    tail = [(nb - group + j) * blk for j in range(group - 1)]
    tiles = [shifted(start) for start in tail]
    for start, x in zip(tail, tiles):
        vt_ref[1, :, start:start + blk] = x
    last = v_ref[0, t - blk:t, :].T
    vt_ref[1, :, t - blk:t] = pltpu.roll(last, GRID_W, 1).astype(BF16)

    for i in range(tc // blk):
        kcb_ref[i * blk:(i + 1) * blk, :] = kc_ref[0, i * blk:(i + 1) * blk, :].astype(BF16)
    tiles = [vc_ref[0, i * blk:(i + 1) * blk, :].T.astype(BF16) for i in range(tc // blk)]
    for i, x in enumerate(tiles):
        vct_ref[:, i * blk:(i + 1) * blk] = x

    def attend(qs, keys, vals_t, biases):
        n = qs[0].shape[0]
        q2 = [jnp.concatenate([jnp.where(hm0, q, 0.0), jnp.where(hm0, 0.0, q)], axis=0).astype(BF16)
              for q in qs]
        lt = [_dot_nt(k, q) for k, q in zip(keys, q2)]
        lt = [x if b is None else jnp.concatenate([x[:b.shape[0]] + b, x[b.shape[0]:]], axis=0)
              for x, b in zip(lt, biases)]
        p = [jnp.exp(x - jnp.max(x, axis=0, keepdims=True)) for x in lt]
        den = [jnp.sum(x, axis=0, keepdims=True) for x in p]
        ot = [_dot(v, x.astype(BF16)) for v, x in zip(vals_t, p)]
        o = [(x / d).T for x, d in zip(ot, den)]
        return [jnp.where(hm0, x[:n], x[n:]) for x in o]

    def row_group(i, carry):
        rows = [i * row_group_size + j for j in range(row_group_size)]
        rs = [jnp.clip(r - WIN_ROWS // 2, 0, n_rows - WIN_ROWS) for r in rows]
        qrows = [pl.ds(pl.multiple_of(r * GRID_W, GRID_W), GRID_W) for r in rows]
        keys, vals_t, biases = [], [], []
        for r, s in zip(rows, rs):
            par = s % 2
            wrows = pl.ds(pl.multiple_of(s * GRID_W, GRID_W), win)
            wlanes = pl.ds(pl.multiple_of((s - par) * GRID_W, LANES), win)
            keys.append(jnp.concatenate([kb_ref[wrows, :], kcb_ref[...]], axis=0))
            vals_t.append(jnp.concatenate([vt_ref[par, :, wlanes], vct_ref[...]], axis=1))
            dr0 = s - r + (WIN_ROWS - 1)
            biases.append(jnp.concatenate([bias_ref[dr0 + j, 0] for j in range(WIN_ROWS)], axis=0))
        outs = attend([q_ref[0, qr, :] * scale for qr in qrows], keys, vals_t, biases)
        for qr, o in zip(qrows, outs):
            y_ref[0, qr, :] = o * _silu(z_ref[0, qr, :])
        return carry

    lax.fori_loop(0, n_rows // row_group_size, row_group, 0)

    if need_ctx:
        o, = attend([qc_ref[0] * scale], [kcb_ref[...]], [vct_ref[...]], [None])
        yc_ref[0] = o * _silu(zc_ref[0])


def _natten_mixer(pc, pcc, bias, need_ctx):
    b, t, _ = pc.shape
    tc = pcc.shape[1]
    npair = W_C // LANES
    n_rows = t // GRID_W
    lat = lambda part: pl.BlockSpec((1, t, LANES), lambda i, p, part=part: (i, 0, part * npair + p))
    ctx = lambda part: pl.BlockSpec((1, tc, LANES), lambda i, p, part=part: (i, 0, part * npair + p))
    in_specs = [lat(0), lat(1), lat(2), lat(3), ctx(1), ctx(2),
                pl.BlockSpec((2 * WIN_ROWS - 1, 1, GRID_W, LANES), lambda i, p: (0, p, 0, 0))]
    args = [pc, pc, pc, pc, pcc, pcc, bias]
    out_specs = [pl.BlockSpec((1, t, LANES), lambda i, p: (i, 0, p))]
    out_shape = [jax.ShapeDtypeStruct((b, t, W_C), F32)]
    if need_ctx:
        in_specs += [ctx(0), ctx(3)]
        args += [pcc, pcc]
        out_specs.append(pl.BlockSpec((1, tc, LANES), lambda i, p: (i, 0, p)))
        out_shape.append(jax.ShapeDtypeStruct((b, tc, W_C), F32))
    scratch = [pltpu.VMEM((t, LANES), BF16), pltpu.VMEM((2, LANES, t), BF16),
               pltpu.VMEM((tc, LANES), BF16), pltpu.VMEM((LANES, tc), BF16)]
    res = pl.pallas_call(
        functools.partial(_natten_kernel, need_ctx=need_ctx, n_rows=n_rows),
        grid=(b, npair),
        in_specs=in_specs,
        out_specs=out_specs,
        out_shape=out_shape,
        scratch_shapes=scratch,
        compiler_params=_cparams("arbitrary", "arbitrary"),
        name="natten_mixer_ctx" if need_ctx else "natten_mixer",
    )(*args)
    return (res[0], res[1]) if need_ctx else (res[0], None)


def _natten_bias(rpb):
    col = np.arange(GRID_W)
    cs = np.clip(col - WIN_COLS // 2, 0, GRID_W - WIN_COLS)
    band = (col[None, :] >= cs[:, None]) & (col[None, :] < cs[:, None] + WIN_COLS)
    pad = GRID_W - WIN_COLS
    padded = jnp.pad(rpb.astype(F32), ((0, 0), (0, 0), (pad, pad)))
    toeplitz = jnp.stack([padded[:, :, GRID_W - 1 - c:2 * GRID_W - 1 - c] for c in range(GRID_W)], axis=2)
    tz = jnp.where(band[None, None], toeplitz, NEG)
    tab = tz.reshape(H_C // 2, 2, 2 * WIN_ROWS - 1, GRID_W, GRID_W)
    tab = jnp.transpose(tab, (2, 0, 4, 1, 3))
    return tab.reshape(2 * WIN_ROWS - 1, H_C // 2, GRID_W, 2 * GRID_W)


def _outproj_kernel(ya_ref, yb_ref, yc_ref, x_ref, gt_ref, g_ref, wa_ref, wb_ref, wc_ref, o_ref):
    u = (_dot(ya_ref[0].astype(BF16), wa_ref[...]) + _dot(yb_ref[0].astype(BF16), wb_ref[...])
         + _dot(yc_ref[0].astype(BF16), wc_ref[...]))
    ms = jnp.mean(u * u, axis=-1, keepdims=True)
    o_ref[0] = x_ref[0] + gt_ref[0] * (u * lax.rsqrt(ms + EPS) * g_ref[...])


def _out_projection(ya, yb, yc, x, gt, g, wts, tm):
    b, t, d = x.shape
    wa, wb, wc = wts
    bm = gt.shape[0]
    mod_map = (lambda i, j: (i, 0, 0)) if bm > 1 else (lambda i, j: (0, 0, 0))
    const = lambda i, j: (0, 0)
    tile = lambda n: pl.BlockSpec((1, tm, n), lambda i, j: (i, j, 0))
    return pl.pallas_call(
        _outproj_kernel,
        grid=(b, t // tm),
        in_specs=[tile(ya.shape[2]), tile(yb.shape[2]), tile(yc.shape[2]), tile(d),
                  pl.BlockSpec((1, 1, d), mod_map),
                  pl.BlockSpec((1, d), const),
                  pl.BlockSpec(wa.shape, const), pl.BlockSpec(wb.shape, const), pl.BlockSpec(wc.shape, const)],
        out_specs=tile(d),
        out_shape=jax.ShapeDtypeStruct((b, t, d), F32),
        compiler_params=_cparams("arbitrary", "arbitrary"),
        name="out_projection",
    )(ya, yb, yc, x, gt, g.reshape(1, d), wa, wb, wc)


def _mlstm_weights(w):
    d = w.shape[0]
    q = DH_B // 4
    blk = w[:, OFF_B:OFF_G].reshape(d, 5, H_B, DH_B)
    gap = jnp.zeros((d, 2, H_B, LANES // 2 - 2 * q), w.dtype)
    qk = blk[:, :2]
    qk = jnp.concatenate([qk[..., 0:q], qk[..., 2 * q:3 * q], gap, qk[..., q:2 * q], qk[..., 3 * q:], gap], axis=-1)
    rest = jnp.pad(blk[:, 2:], ((0, 0), (0, 0), (0, 0), (0, LANES - DH_B)))
    wb = jnp.concatenate([qk, rest], axis=1).reshape(d, 5 * H_B * LANES)
    g = jnp.swapaxes(w[:, OFF_G:OFF_C].reshape(d, 4, H_B), 1, 2)
    wgt = jnp.pad(g, ((0, 0), (0, 0), (0, 4))).reshape(d, H_B * 8).T
    return wb.astype(BF16), wgt.astype(BF16)


def _rope_tables(t):
    pos = np.arange(t)
    q = DH_B // 4
    inv = ROPE_BASE ** (-jnp.arange(0, 2 * q, 2, dtype=F32) / (2 * q))
    ang_r = jnp.asarray(pos // GRID_W, F32)[:, None] * inv[None, :]
    ang_c = jnp.asarray(pos % GRID_W, F32)[:, None] * inv[None, :]
    zeros = jnp.zeros((t, LANES // 2 - 2 * q), F32)
    cos_half = jnp.concatenate([jnp.cos(ang_r), jnp.cos(ang_c), zeros], axis=1)
    sin_half = jnp.concatenate([jnp.sin(ang_r), jnp.sin(ang_c), zeros], axis=1)
    return (jnp.concatenate([cos_half, cos_half], axis=1),
            jnp.concatenate([-sin_half, sin_half], axis=1))


def _pad_heads(v):
    v = v.reshape(v.shape[:-1] + (H_B, DH_B))
    v = jnp.pad(v, [(0, 0)] * (v.ndim - 1) + [(0, LANES - DH_B)])
    return v.reshape(v.shape[:-2] + (H_B * LANES,))


def kernel(x, c, ctx, c_ctx, w_mod, b_mod, g_pre, g_post, w_in, w_out, hgrn_lb, hgrn_gn, mlstm_gate_b, mlstm_gn, na_rpb):
    depth = w_in.shape[0]
    b, t, d = x.shape
    tc = ctx.shape[1]
    assert t % CHUNK == 0 and tc % CHUNK == 0 and t // GRID_W >= WIN_ROWS and t % GRID_W == 0
    assert w_in.shape[2] == P_IN and (1 << N_LEVELS) == CHUNK

    lb_cum = jnp.cumsum(jax.nn.softmax(hgrn_lb.astype(F32), axis=0), axis=0)
    lb_all = lb_cum - lb_cum[0]
    hgrn_tabs = _hgrn_tables()
    tris = _tri_tables()
    rope = _rope_tables(t)
    mod_rows = 16
    cc = jnp.concatenate([c.astype(F32), c_ctx.astype(F32)[None],
                          jnp.zeros((mod_rows - b - 1, d), F32)], axis=0)
    tm = min(256, t)
    tmc = min(256, tc)

    for l in range(depth):
        need_ctx = l < depth - 1
        mod = _modulation(cc, w_mod[l], b_mod[l])
        sh, sc, gt = (mod[:b, i * d:(i + 1) * d].reshape(b, 1, d) for i in range(3))
        shc, scc, gtc = (mod[b:b + 1, i * d:(i + 1) * d].reshape(1, 1, d) for i in range(3))

        w = w_in[l]
        gb = mlstm_gate_b[l].astype(F32)
        gbc = jnp.pad(gb.T, ((0, 0), (0, 4))).reshape(H_B * 8, 1)
        wb, wgt = _mlstm_weights(w)
        wts = (w[:, :OFF_B].astype(BF16), wb, w[:, OFF_C:].astype(BF16), wgt, gbc)

        pa, pb, pc, pgt = _in_projection(x, sh, sc, g_pre[l], wts, rope, tm)
        pac, pbc, pcc, pgtc = _in_projection(ctx, shc, scc, g_pre[l], wts, None, tmc)

        ya, yac = _hgrn_mixer(pa, pac, lb_all[l], hgrn_gn[l], hgrn_tabs, need_ctx)
        gnb = _pad_heads(mlstm_gn[l].astype(F32)).reshape(H_B, 1, LANES)
        yb, ybc = _mlstm_mixer(pb, pgt, pbc, pgtc, gnb, tris, need_ctx)
        yc, ycc = _natten_mixer(pc, pcc, _natten_bias(na_rpb[l]), need_ctx)

        wo = w_out[l]
        wo_b = jnp.pad(wo[W_A:W_A + W_B].reshape(H_B, DH_B, d), ((0, 0), (0, LANES - DH_B), (0, 0)))
        wo_parts = (wo[:W_A].astype(BF16), wo_b.reshape(H_B * LANES, d).astype(BF16),
                    wo[W_A + W_B:].astype(BF16))
        x = _out_projection(ya, yb, yc, x, gt, g_post[l], wo_parts, tm)
        if need_ctx:
            ctx = _out_projection(yac, ybc, ycc, ctx, gtc, g_post[l], wo_parts, tmc)
    return x
```

```python
import functools
import math

import numpy as np
import jax
import jax.numpy as jnp
from jax import lax
from jax.experimental import pallas as pl
from jax.experimental.pallas import tpu as pltpu

F32 = jnp.float32
BF16 = jnp.bfloat16

LANES = 128
GRID_W = 64
W_A, H_A, DH_A = 256, 4, 64
W_B, H_B, DH_B = 384, 4, 96
W_C, H_C, DH_C = 384, 6, 64
WIN_ROWS, WIN_COLS = 8, 16
ROPE_BASE = 10000.0
EPS = 1e-6
NEG = -1e30
LOG2_E = math.log2(math.e)
CHUNK = 128
N_LEVELS = 7
NA_ROW_UNROLL = 8
NA_STAGE_GROUP = 4
CHUNK_UNROLL = 4
MLSTM_GROUP = 8
OFF_B = 5 * W_A
OFF_G = OFF_B + 5 * W_B
OFF_C = OFF_G + 4 * H_B
P_IN = OFF_C + 4 * W_C
ONE_LANE = DH_B
VMEM_LIMIT = 56 * 1024 * 1024


def _cparams(*sem):
    return pltpu.CompilerParams(dimension_semantics=sem, vmem_limit_bytes=VMEM_LIMIT)


def _dot(a, b):
    return jnp.dot(a, b, preferred_element_type=F32)


def _dot_nt(a, b):
    return lax.dot_general(a, b, (((1,), (1,)), ((), ())), preferred_element_type=F32)


def _dot_tn(a, b):
    return lax.dot_general(a, b, (((0,), (0,)), ((), ())), preferred_element_type=F32)


def _chunk_loop(n, body):
    u = math.gcd(n, CHUNK_UNROLL)

    def step(i, carry):
        for j in range(u):
            body(i * u + j, carry)
        return carry

    return lax.fori_loop(0, n // u, step, 0)


def _grouped_loop(n, body, group=CHUNK_UNROLL):
    u = math.gcd(n, group)

    def step(i, carry):
        body([i * u + j for j in range(u)])
        return carry

    return lax.fori_loop(0, n // u, step, 0)


def _split3(x):
    hi = x.astype(BF16)
    r = x - hi.astype(F32)
    mid = r.astype(BF16)
    lo = (r - mid.astype(F32)).astype(BF16)
    return hi, mid, lo


def _exact_left(t01, x):
    n = x.shape[1]
    r = _dot(t01, jnp.concatenate(_split3(x), axis=1))
    return r[:, :n] + r[:, n:2 * n] + r[:, 2 * n:]


def _exact_right(x, t01):
    m = x.shape[0]
    pieces = jnp.concatenate([p.astype(F32) for p in _split3(x)], axis=0).astype(BF16)
    r = _dot(pieces, t01)
    return r[:m] + r[m:2 * m] + r[2 * m:]


def _sigmoid(x):
    return 1.0 / (1.0 + jnp.exp(-x))


def _silu(x):
    return x * _sigmoid(x)


def _log_sigmoid(x):
    return jnp.minimum(x, 0.0) - jnp.log(1.0 + jnp.exp(-jnp.abs(x)))


def _mod_kernel(c_ref, w_ref, b_ref, o_ref):
    s = _silu(c_ref[...])
    o_ref[...] = _dot(s.astype(BF16), w_ref[...].astype(BF16)) + b_ref[...]


def _modulation(cc, w, b):
    rows, d = cc.shape
    n = w.shape[1]
    tn = d
    assert n % tn == 0
    return pl.pallas_call(
        _mod_kernel,
        grid=(n // tn,),
        in_specs=[pl.BlockSpec((rows, d), lambda j: (0, 0)),
                  pl.BlockSpec((d, tn), lambda j: (0, j)),
                  pl.BlockSpec((1, tn), lambda j: (0, j))],
        out_specs=pl.BlockSpec((rows, tn), lambda j: (0, j)),
        out_shape=jax.ShapeDtypeStruct((rows, n), F32),
        compiler_params=_cparams("arbitrary"),
        name="modulation",
    )(cc, w, b.reshape(1, n))


def _inproj_kernel(*refs, rotary):
    if rotary:
        (x_ref, sh_ref, sc_ref, g_ref, wa_ref, wb_ref, wc_ref, wgt_ref, gbc_ref,
         cos_ref, sin_ref, pa_ref, pb_ref, pc_ref, pgt_ref) = refs
    else:
        (x_ref, sh_ref, sc_ref, g_ref, wa_ref, wb_ref, wc_ref, wgt_ref, gbc_ref,
         pa_ref, pb_ref, pc_ref, pgt_ref) = refs
    x = x_ref[0]
    ms = jnp.mean(x * x, axis=-1, keepdims=True)
    h = x * lax.rsqrt(ms + EPS) * g_ref[...]
    h = h * (1.0 + sc_ref[0]) + sh_ref[0]
    hb = h.astype(BF16)
    pa_ref[0] = _dot(hb, wa_ref[...])
    pc_ref[0] = _dot(hb, wc_ref[...])
    pgt_ref[0] = _dot_nt(wgt_ref[...], hb) + gbc_ref[...]
    hw = H_B * LANES
    scale = DH_B ** -0.5
    for part in range(5):
        p = _dot(hb, wb_ref[:, part * hw:(part + 1) * hw])
        if part == 0:
            p = p * scale
        if rotary and part < 2:
            cos = cos_ref[...]
            sin = sin_ref[...]
            for hd in range(H_B):
                ph = p[:, hd * LANES:(hd + 1) * LANES]
                pb_ref[0, :, part * hw + hd * LANES:part * hw + (hd + 1) * LANES] = (
                    ph * cos + pltpu.roll(ph, LANES // 2, 1) * sin)
        else:
            pb_ref[0, :, part * hw:(part + 1) * hw] = p


def _layer_spec(a, layer):
    return pl.BlockSpec((None,) + a.shape[1:], lambda i, j: (layer, 0, 0))


def _in_projection(x, sh, sc, g, wts, layer, rope, tm):
    b, t, d = x.shape
    wa, wb, wc, wgt, gbc = wts
    bm = sh.shape[0]
    mod_map = (lambda i, j: (i, 0, 0)) if bm > 1 else (lambda i, j: (0, 0, 0))
    in_specs = [pl.BlockSpec((1, tm, d), lambda i, j: (i, j, 0)),
                pl.BlockSpec((1, 1, d), mod_map),
                pl.BlockSpec((1, 1, d), mod_map)] + [_layer_spec(a, layer) for a in (g, wa, wb, wc, wgt, gbc)]
    args = [x, sh, sc, g, wa, wb, wc, wgt, gbc]
    if rope is not None:
        in_specs += [pl.BlockSpec((tm, LANES), lambda i, j: (j, 0))] * 2
        args += list(rope)
    na, nb, nc, ng = wa.shape[2], wb.shape[2], wc.shape[2], wgt.shape[1]
    out_specs = [pl.BlockSpec((1, tm, na), lambda i, j: (i, j, 0)),
                 pl.BlockSpec((1, tm, nb), lambda i, j: (i, j, 0)),
                 pl.BlockSpec((1, tm, nc), lambda i, j: (i, j, 0)),
                 pl.BlockSpec((1, ng, tm), lambda i, j: (i, 0, j))]
    out_shape = [jax.ShapeDtypeStruct((b, t, na), F32),
                 jax.ShapeDtypeStruct((b, t, nb), F32),
                 jax.ShapeDtypeStruct((b, t, nc), F32),
                 jax.ShapeDtypeStruct((b, ng, t), F32)]
    return pl.pallas_call(
        functools.partial(_inproj_kernel, rotary=rope is not None),
        grid=(b, t // tm),
        in_specs=in_specs,
        out_specs=out_specs,
        out_shape=out_shape,
        compiler_params=_cparams("arbitrary", "arbitrary"),
        name="in_projection_rope" if rope is not None else "in_projection",
    )(*args)


def _hgrn_tables():
    L, nl = CHUNK, N_LEVELS
    idx = np.arange(L)
    t = idx[:, None]
    u = idx[None, :]
    tabq, levels = [], []
    for rev in (False, True):
        tabq.append(((u <= t) if not rev else (u >= t)).astype(np.float32))
        x = t ^ u
        lvl = np.where(x > 0, np.floor(np.log2(np.maximum(x, 1))), float(nl))
        valid = (u <= t) if not rev else (u >= t)
        lvl = np.where(valid, lvl, -1.0).astype(np.float32)
        levels.append(np.concatenate([lvl, lvl], axis=1))
    return jnp.asarray(np.stack(tabq), dtype=BF16), jnp.asarray(np.stack(levels), dtype=F32)


def _tri_tables():
    L = CHUNK
    idx = np.arange(L)
    t = idx[:, None]
    u = idx[None, :]
    fwd = (u <= t).astype(np.float32)
    bwd = (u >= t).astype(np.float32)
    spread = np.zeros((16, 2 * L), np.float32)
    spread[0:3, :L] = 1.0
    spread[3:6, L:] = 1.0
    return (jnp.asarray(np.concatenate([fwd.T, bwd.T], axis=1), dtype=BF16), jnp.asarray(spread, dtype=BF16))


def _hgrn_kernel(*refs, need_ctx, n_lat, n_ctx):
    (q_ref, ff_ref, fb_ref, i_ref, z_ref, qc_ref, ffc_ref, fbc_ref, ic_ref, zc_ref,
     lb_ref, gn_ref, tabq_ref, lvl_ref) = refs[:14]
    if need_ctx:
        y_ref, yc_ref, u_ref, dec_ref, gate_ref = refs[14:]
    else:
        y_ref, u_ref, dec_ref, gate_ref = refs[14:]
        yc_ref = None
    L, nl = CHUNK, N_LEVELS
    lane = lax.broadcasted_iota(jnp.int32, (1, LANES), 1)
    hm0 = lane < DH_A
    row_h = lax.broadcasted_iota(jnp.int32, (LANES, LANES), 0) // DH_A
    col_h = lax.broadcasted_iota(jnp.int32, (LANES, LANES), 1) // DH_A
    same_head = row_h == col_h
    gn = gn_ref[0]
    segs = ((0, n_ctx, qc_ref, (ffc_ref, fbc_ref), ic_ref, zc_ref, yc_ref),
            (n_ctx, n_lat, q_ref, (ff_ref, fb_ref), i_ref, z_ref, y_ref))

    def decay(fl, lbd):
        a0 = jnp.log(lbd)
        b0 = jnp.log(1.0 - lbd) + _log_sigmoid(fl)
        logf = jnp.maximum(a0, b0) + jnp.log(1.0 + jnp.exp(-jnp.abs(a0 - b0)))
        return logf * LOG2_E, (1.0 - lbd) * _sigmoid(-fl)

    def rows_of(c):
        return pl.ds(pl.multiple_of(c * L, L), L)

    tok = lax.broadcasted_iota(jnp.int32, (L, 1), 0)

    def level_exponents(logf, bcum, d):
        prev, nxt = pltpu.roll(logf, 1, 0), pltpu.roll(logf, L - 1, 0)
        if d == 0:
            eqs = [logf, logf + jnp.where((tok % 4) == 3, prev, 0.0)]
            eks = [None, jnp.where((tok % 4) == 0, nxt, 0.0)]
        else:
            eqs = [logf, logf + jnp.where((tok % 4) == 0, nxt, 0.0)]
            eks = [None, jnp.where((tok % 4) == 3, prev, 0.0)]
        for lv in range(2, nl):
            h = 1 << lv
            r = h - 1 if d == 0 else h
            g = jnp.concatenate([jnp.broadcast_to(bcum[blk + r:blk + r + 1], (2 * h, LANES))
                                 for blk in range(0, L, 2 * h)], axis=0)
            eqs.append(bcum - g)
            eks.append(g - bcum)
        return eqs, eks

    def chunk_total(bcum, d):
        return bcum[L - 1:L] if d == 0 else bcum[0:1]

    for base, n, _, f_refs, ai_ref, _, _ in segs:
        def increments(cs, base=base, f_refs=f_refs, ai_ref=ai_ref):
            jd = [(j, d) for j in range(len(cs)) for d in range(2)]
            lk = {(j, d): decay(f_refs[d][0, rows_of(cs[j]), :], lb_ref[d, 0]) for j, d in jd}
            bcum = {(j, d): _exact_left(tabq_ref[d], lk[(j, d)][0]) for j, d in jd}
            b_tot = {(j, d): chunk_total(bcum[(j, d)], d) for j, d in jd}
            vb = [ai_ref[0, rows_of(c), :].astype(BF16) for c in cs]
            kr = {x: (lk[x][1] * jnp.exp2(b_tot[x] - bcum[x])).astype(BF16) for x in jd}
            u = {(j, d): _dot_tn(kr[(j, d)], vb[j]) for j, d in jd}
            dec = {x: jnp.broadcast_to(jnp.exp2(b_tot[x]), (LANES, LANES)).T for x in jd}
            for j, d in jd:
                u_ref[d, base + cs[j]] = jnp.where(same_head, u[(j, d)], 0.0)
                dec_ref[d, base + cs[j]] = dec[(j, d)]
                tok_rows = rows_of(base + cs[j])
                gate_ref[d, 0, tok_rows, :] = lk[(j, d)][0]
                gate_ref[d, 1, tok_rows, :] = lk[(j, d)][1]
                gate_ref[d, 2, tok_rows, :] = bcum[(j, d)]
        _grouped_loop(n, increments)

    def advance(d, idx, s):
        u = u_ref[d, idx]
        u_ref[d, idx] = s
        return s * dec_ref[d, idx] + u

    states = (jnp.zeros((LANES, LANES), F32),) * 2
    for base, n, *_ in segs:
        def scan_step(i, states, base=base, n=n):
            return (advance(0, base + i, states[0]), advance(1, base + n - 1 - i, states[1]))
        states = lax.fori_loop(0, n, scan_step, states)

    def finish(o, z):
        sq = o * o
        ms0 = jnp.sum(jnp.where(hm0, sq, 0.0), axis=-1, keepdims=True)
        ms1 = jnp.sum(jnp.where(hm0, 0.0, sq), axis=-1, keepdims=True)
        ms = jnp.where(hm0, ms0, ms1) * (1.0 / DH_A)
        return o * lax.rsqrt(ms + EPS) * gn * _silu(z)

    for base, n, aq_ref, f_refs, ai_ref, zz_ref, out_ref in segs:
        if out_ref is None:
            continue

        def outputs(cs, base=base, aq_ref=aq_ref, f_refs=f_refs, ai_ref=ai_ref, zz_ref=zz_ref,
                    out_ref=out_ref):
            js = range(len(cs))
            jd = [(j, d) for j in js for d in range(2)]
            q = [_silu(aq_ref[0, rows_of(c), :]) * (DH_A ** -0.5) for c in cs]
            lk = {(j, d): (gate_ref[d, 0, rows_of(base + cs[j]), :], gate_ref[d, 1, rows_of(base + cs[j]), :])
                  for j, d in jd}
            bcum = {(j, d): gate_ref[d, 2, rows_of(base + cs[j]), :] for j, d in jd}
            ex = {(j, d): level_exponents(lk[(j, d)][0], bcum[(j, d)], d) for j, d in jd}
            a = {x: jnp.zeros((L, 2 * L), F32) for x in jd}
            zero = jnp.zeros((DH_A, L), BF16)
            for lv in range(nl + 1):
                at_level = [lvl_ref[d] == float(lv) for d in range(2)]
                for j, d in jd:
                    k = lk[(j, d)][1]
                    eqs, eks = ex[(j, d)]
                    if lv < nl:
                        qt = q[j] * jnp.exp2(eqs[lv])
                        kl = k if eks[lv] is None else k * jnp.exp2(eks[lv])
                    else:
                        qt, kl = q[j], k
                    ktl = kl.T.astype(BF16)
                    w = jnp.concatenate([jnp.concatenate([ktl[:DH_A], zero], axis=0),
                                         jnp.concatenate([zero, ktl[DH_A:]], axis=0)], axis=1)
                    a[(j, d)] = jnp.where(at_level[d], _dot(qt.astype(BF16), w), a[(j, d)])
            ai = [ai_ref[0, rows_of(c), :] for c in cs]
            vs = [jnp.concatenate([jnp.where(hm0, x, 0.0), jnp.where(hm0, 0.0, x)], axis=0).astype(BF16)
                  for x in ai]
            intra = {(j, d): _dot(a[(j, d)].astype(BF16), vs[j]) for j, d in jd}
            inter = {(j, d): _dot((q[j] * jnp.exp2(bcum[(j, d)])).astype(BF16),
                                  u_ref[d, base + cs[j]].astype(BF16)) for j, d in jd}
            for j, c in enumerate(cs):
                o = intra[(j, 0)] + inter[(j, 0)] + intra[(j, 1)] + inter[(j, 1)]
                out_ref[0, rows_of(c), :] = finish(o, zz_ref[0, rows_of(c), :]).astype(out_ref.dtype)
        _grouped_loop(n, outputs)


def _hgrn_mixer(pa, pac, lb, gn, tabs, need_ctx):
    b, t, _ = pa.shape
    tc = pac.shape[1]
    npair = W_A // LANES
    n_chunks = (t + tc) // CHUNK
    lat = lambda part: pl.BlockSpec((1, t, LANES), lambda i, p, part=part: (i, 0, part * npair + p))
    ctx = lambda part: pl.BlockSpec((1, tc, LANES), lambda i, p, part=part: (i, 0, part * npair + p))
    whole = lambda a: pl.BlockSpec(a.shape, lambda i, p: (0, 0, 0))
    in_specs = ([lat(0), lat(1), lat(2), lat(3), lat(4), ctx(0), ctx(1), ctx(2), ctx(3), ctx(4),
                 pl.BlockSpec((2, 1, 1, LANES), lambda i, p: (0, p, 0, 0)),
                 pl.BlockSpec((1, 1, LANES), lambda i, p: (p, 0, 0))] + [whole(a) for a in tabs])
    out_specs = [pl.BlockSpec((1, t, LANES), lambda i, p: (i, 0, p))]
    out_shape = [jax.ShapeDtypeStruct((b, t, W_A), BF16)]
    scratch = [pltpu.VMEM((2, n_chunks, LANES, LANES), F32)] * 2
    scratch.append(pltpu.VMEM((2, 3, t + tc, LANES), F32))
    if need_ctx:
        out_specs.append(pl.BlockSpec((1, tc, LANES), lambda i, p: (i, 0, p)))
        out_shape.append(jax.ShapeDtypeStruct((b, tc, W_A), BF16))
    res = pl.pallas_call(
        functools.partial(_hgrn_kernel, need_ctx=need_ctx, n_lat=t // CHUNK, n_ctx=tc // CHUNK),
        grid=(b, npair),
        in_specs=in_specs,
        out_specs=out_specs,
        out_shape=out_shape,
        scratch_shapes=scratch,
        compiler_params=_cparams("arbitrary", "arbitrary"),
        name="hgrn2_mixer_ctx" if need_ctx else "hgrn2_mixer",
    )(pa, pa, pa, pa, pa, pac, pac, pac, pac, pac,
      lb.reshape(2, npair, 1, LANES), gn.reshape(npair, 1, LANES), *tabs)
    return (res[0], res[1]) if need_ctx else (res[0], None)


def _mlstm_kernel(*refs, need_ctx, n_lat, n_ctx):
    (q_ref, k_ref, v_ref, o_ref, z_ref, qc_ref, kc_ref, vc_ref, oc_ref, zc_ref,
     gt_ref, gtc_ref, gn_ref, trit_ref, ones_ref) = refs[:15]
    if need_ctx:
        y_ref, yc_ref, c_ref, st_ref, vt_ref = refs[15:]
    else:
        y_ref, c_ref, st_ref, vt_ref = refs[15:]
        yc_ref = None
    L = CHUNK
    sub = lax.broadcasted_iota(jnp.int32, (LANES, 1), 0)
    rows_i = lax.broadcasted_iota(jnp.int32, (L, L), 0)
    cols_i = lax.broadcasted_iota(jnp.int32, (L, L), 1)
    gn = gn_ref[0]
    segs = ((0, n_ctx, qc_ref, kc_ref, vc_ref, oc_ref, zc_ref, gtc_ref, yc_ref),
            (n_ctx, n_lat, q_ref, k_ref, v_ref, o_ref, z_ref, gt_ref, y_ref))
    B_TOT, M_LOC, M_PREV = 0, 1, 2

    def rows_of(c):
        return pl.ds(pl.multiple_of(c * L, L), L)

    loop = _chunk_loop

    def row_gates(grow):
        er = _exact_right(_log_sigmoid(grow), trit_ref[...])
        out = []
        for d in range(2):
            b_row = er[2 + d:3 + d, d * L:(d + 1) * L]
            out.append((grow[d:d + 1, :], b_row, b_row[:, L - 1:L] if d == 0 else b_row[:, 0:1]))
        return out

    for base, n, _, _, vv, *_ in segs:
        def transpose_values(c, carry, base=base, vv=vv):
            vt_ref[:, rows_of(base + c)] = jnp.where(sub == ONE_LANE, 1.0, vv[0, rows_of(c), :].T)
            return carry
        loop(n, transpose_values)

    for base, n, _, kk, _, _, _, ggt, _ in segs:
        def local(cs, base=base, kk=kk, ggt=ggt):
            jd = [(j, d) for j in range(len(cs)) for d in range(2)]
            gates = [row_gates(ggt[0, :, rows_of(c)]) for c in cs]
            w = {(j, d): gates[j][d][2] + gates[j][d][0] - gates[j][d][1] for j, d in jd}
            m_loc = {x: jnp.max(w[x], axis=1, keepdims=True) for x in jd}
            ew = {x: jnp.exp(w[x] - m_loc[x]) for x in jd}
            kb = [kk[0, rows_of(c), :].astype(BF16) for c in cs]
            vat = [vt_ref[:, rows_of(base + c)] for c in cs]
            lhs = {(j, d): (vat[j] * ew[(j, d)]).astype(BF16) for j, d in jd}
            out = {(j, d): _dot(lhs[(j, d)], kb[j]) for j, d in jd}
            for j, d in jd:
                c_ref[d, base + cs[j]] = out[(j, d)]
                st_ref[d, base + cs[j], B_TOT:B_TOT + 1, :] = jnp.broadcast_to(gates[j][d][2], (1, LANES))
                st_ref[d, base + cs[j], M_LOC:M_LOC + 1, :] = jnp.broadcast_to(m_loc[(j, d)], (1, LANES))
        _grouped_loop(n, local, MLSTM_GROUP)

    def advance(d, idx, state):
        c_prev, m_prev = state
        c_loc = c_ref[d, idx]
        b_tot = st_ref[d, idx, B_TOT:B_TOT + 1, :]
        m_loc = st_ref[d, idx, M_LOC:M_LOC + 1, :]
        c_ref[d, idx] = c_prev
        st_ref[d, idx, M_PREV:M_PREV + 1, :] = m_prev
        m_new = jnp.maximum(b_tot + m_prev, m_loc)
        return (jnp.exp(b_tot + m_prev - m_new) * c_prev + jnp.exp(m_loc - m_new) * c_loc, m_new)

    s0 = (jnp.zeros((LANES, LANES), F32), jnp.zeros((1, LANES), F32))
    states = (s0, s0)
    for base, n, *_ in segs:
        def scan_step(i, states, base=base, n=n):
            return (advance(0, base + i, states[0]), advance(1, base + n - 1 - i, states[1]))
        states = lax.fori_loop(0, n, scan_step, states)

    for base, n, qq, kk, _, oo, zz, ggt, out_ref in segs:
        if out_ref is None:
            continue

        def outputs(cs, base=base, qq=qq, kk=kk, oo=oo, zz=zz, ggt=ggt, out_ref=out_ref):
            js = range(len(cs))
            jd = [(j, d) for j in js for d in range(2)]
            qb = [qq[0, rows_of(c), :].astype(BF16) for c in cs]
            sq = [_dot_nt(jnp.concatenate([kk[0, rows_of(c), :].astype(BF16),
                                           c_ref[0, base + c].astype(BF16),
                                           c_ref[1, base + c].astype(BF16)], axis=0), qb[j])
                  for j, c in enumerate(cs)]
            gates = [row_gates(ggt[0, :, rows_of(c)]) for c in cs]
            pieces = [jnp.concatenate(
                [p.astype(F32) for d in range(2) for p in _split3(gates[j][d][0] - gates[j][d][1])]
                + [jnp.zeros((10, L), F32)], axis=0).astype(BF16) for j in js]
            r_bc = [_dot_tn(x, ones_ref[...]) for x in pieces]
            valid = [rows_i <= cols_i, rows_i >= cols_i]
            dm = {(j, d): jnp.where(valid[d], gates[j][d][1] + r_bc[j][:, d * L:(d + 1) * L], NEG)
                  for j, d in jd}
            inter = {(j, d): gates[j][d][1] + st_ref[d, base + cs[j], M_PREV:M_PREV + 1, 0:1] for j, d in jd}
            m_t = {x: jnp.maximum(jnp.max(dm[x], axis=0, keepdims=True), inter[x]) for x in jd}
            p = {(j, d): sq[j][:L] * jnp.exp(dm[(j, d)] - m_t[(j, d)]) for j, d in jd}
            e_in = {x: jnp.exp(inter[x] - m_t[x]) for x in jd}
            vat = [vt_ref[:, rows_of(base + c)].astype(BF16) for c in cs]
            pv = {(j, d): _dot(vat[j], p[(j, d)].astype(BF16)) for j, d in jd}
            hts = []
            for j in js:
                ht = jnp.zeros((LANES, L), F32)
                for d in range(2):
                    x = (j, d)
                    qct = sq[j][(1 + d) * L:(2 + d) * L]
                    num = pv[x] + e_in[x] * qct
                    den = jnp.sum(p[x], axis=0, keepdims=True) + e_in[x] * qct[ONE_LANE:ONE_LANE + 1, :]
                    ht = ht + num / jnp.maximum(jnp.abs(den), jnp.exp(-m_t[x]))
                hts.append(jnp.where(sub < DH_B, ht, 0.0))
            ms = [jnp.sum(ht * ht, axis=0, keepdims=True) * (1.0 / DH_B) for ht in hts]
            h = [(ht * lax.rsqrt(m + EPS)).T for ht, m in zip(hts, ms)]
            for j, c in enumerate(cs):
                rows = rows_of(c)
                y = _sigmoid(oo[0, rows, :]) * (h[j] * gn) * _silu(zz[0, rows, :])
                out_ref[0, rows, :] = y.astype(out_ref.dtype)
        _grouped_loop(n, outputs, MLSTM_GROUP)


def _mlstm_mixer(pb, pgt, pbc, pgtc, gn, tris, need_ctx):
    b, t, _ = pb.shape
    tc = pbc.shape[1]
    trit, spread = tris
    lat = lambda part: pl.BlockSpec((1, t, LANES), lambda i, h, part=part: (i, 0, part * H_B + h))
    ctx = lambda part: pl.BlockSpec((1, tc, LANES), lambda i, h, part=part: (i, 0, part * H_B + h))
    in_specs = ([lat(p) for p in range(5)] + [ctx(p) for p in range(5)] +
                [pl.BlockSpec((1, 8, t), lambda i, h: (i, h, 0)),
                 pl.BlockSpec((1, 8, tc), lambda i, h: (i, h, 0)),
                 pl.BlockSpec((1, 1, LANES), lambda i, h: (h, 0, 0)),
                 pl.BlockSpec(trit.shape, lambda i, h: (0, 0)),
                 pl.BlockSpec(spread.shape, lambda i, h: (0, 0))])
    out_specs = [pl.BlockSpec((1, t, LANES), lambda i, h: (i, 0, h))]
    out_shape = [jax.ShapeDtypeStruct((b, t, H_B * LANES), BF16)]
    n_chunks = (t + tc) // CHUNK
    scratch = [pltpu.VMEM((2, n_chunks, LANES, LANES), F32),
               pltpu.VMEM((2, n_chunks, 8, LANES), F32),
               pltpu.VMEM((LANES, t + tc), F32)]
    if need_ctx:
        out_specs.append(pl.BlockSpec((1, tc, LANES), lambda i, h: (i, 0, h)))
        out_shape.append(jax.ShapeDtypeStruct((b, tc, H_B * LANES), BF16))
    res = pl.pallas_call(
        functools.partial(_mlstm_kernel, need_ctx=need_ctx, n_lat=t // CHUNK, n_ctx=tc // CHUNK),
        grid=(b, H_B),
        in_specs=in_specs,
        out_specs=out_specs,
        out_shape=out_shape,
        scratch_shapes=scratch,
        compiler_params=_cparams("arbitrary", "arbitrary"),
        name="mlstm_mixer_ctx" if need_ctx else "mlstm_mixer",
    )(pb, pb, pb, pb, pb, pbc, pbc, pbc, pbc, pbc, pgt, pgtc, gn, trit, spread)
    return (res[0], res[1]) if need_ctx else (res[0], None)


def _natten_kernel(*refs, need_ctx, n_rows):
    if need_ctx:
        (q_ref, k_ref, v_ref, z_ref, kc_ref, vc_ref, bias_ref, qc_ref, zc_ref, y_ref, yc_ref,
         kb_ref, vt_ref, kcb_ref, vct_ref) = refs
    else:
        (q_ref, k_ref, v_ref, z_ref, kc_ref, vc_ref, bias_ref, y_ref,
         kb_ref, vt_ref, kcb_ref, vct_ref) = refs
    lane = lax.broadcasted_iota(jnp.int32, (1, LANES), 1)
    hm0 = lane < DH_C
    scale = DH_C ** -0.5
    win = WIN_ROWS * GRID_W
    t = k_ref.shape[1]
    tc = kc_ref.shape[1]
    blk = LANES

    nb = t // blk
    group = math.gcd(nb, NA_STAGE_GROUP)
    row_group_size = math.gcd(n_rows, NA_ROW_UNROLL)

    def blocks(i):
        return [pl.ds(pl.multiple_of((i * group + j) * blk, blk), blk) for j in range(group)]

    def stage(i, carry):
        tiles = [v_ref[0, rows, :].T.astype(BF16) for rows in blocks(i)]
        for rows, x in zip(blocks(i), tiles):
            kb_ref[rows, :] = k_ref[0, rows, :].astype(BF16)
            vt_ref[0, :, rows] = x
        return carry

    lax.fori_loop(0, nb // group, stage, 0)

    def shifted(start):
        src = start + GRID_W
        if not isinstance(src, int):
            src = pl.multiple_of(src, GRID_W)
        return v_ref[0, pl.ds(src, blk), :].T.astype(BF16)

    def stage_shifted(i, carry):
        tiles = [shifted(rows.start) for rows in blocks(i)]
        for rows, x in zip(blocks(i), tiles):
            vt_ref[1, :, rows] = x
        return carry

    lax.fori_loop(0, nb // group - 1, stage_shifted, 0)
    tail = [(nb - group + j) * blk for j in range(group - 1)]
    tiles = [shifted(start) for start in tail]
    for start, x in zip(tail, tiles):
        vt_ref[1, :, start:start + blk] = x
    last = v_ref[0, t - blk:t, :].T
    vt_ref[1, :, t - blk:t] = pltpu.roll(last, GRID_W, 1).astype(BF16)

    for i in range(tc // blk):
        kcb_ref[i * blk:(i + 1) * blk, :] = kc_ref[0, i * blk:(i + 1) * blk, :].astype(BF16)
    tiles = [vc_ref[0, i * blk:(i + 1) * blk, :].T.astype(BF16) for i in range(tc // blk)]
    for i, x in enumerate(tiles):
        vct_ref[:, i * blk:(i + 1) * blk] = x

    def attend(qs, keys, vals_t, biases):
        n = qs[0].shape[0]
        q2 = [jnp.concatenate([jnp.where(hm0, q, 0.0), jnp.where(hm0, 0.0, q)], axis=0).astype(BF16)
              for q in qs]
        lt = [_dot_nt(k, q) for k, q in zip(keys, q2)]
        lt = [x if b is None else jnp.concatenate([x[:b.shape[0]] + b, x[b.shape[0]:]], axis=0)
              for x, b in zip(lt, biases)]
        p = [jnp.exp(x - jnp.max(x, axis=0, keepdims=True)) for x in lt]
        den = [jnp.sum(x, axis=0, keepdims=True) for x in p]
        ot = [_dot(v, x.astype(BF16)) for v, x in zip(vals_t, p)]
        o = [(x / d).T for x, d in zip(ot, den)]
        return [jnp.where(hm0, x[:n], x[n:]) for x in o]

    def row_group(i, carry):
        rows = [i * row_group_size + j for j in range(row_group_size)]
        rs = [jnp.clip(r - WIN_ROWS // 2, 0, n_rows - WIN_ROWS) for r in rows]
        qrows = [pl.ds(pl.multiple_of(r * GRID_W, GRID_W), GRID_W) for r in rows]
        keys, vals_t, biases = [], [], []
        for r, s in zip(rows, rs):
            par = s % 2
            wrows = pl.ds(pl.multiple_of(s * GRID_W, GRID_W), win)
            wlanes = pl.ds(pl.multiple_of((s - par) * GRID_W, LANES), win)
            keys.append(jnp.concatenate([kb_ref[wrows, :], kcb_ref[...]], axis=0))
            vals_t.append(jnp.concatenate([vt_ref[par, :, wlanes], vct_ref[...]], axis=1))
            dr0 = s - r + (WIN_ROWS - 1)
            biases.append(jnp.concatenate([bias_ref[dr0 + j, 0] for j in range(WIN_ROWS)], axis=0))
        outs = attend([q_ref[0, qr, :] * scale for qr in qrows], keys, vals_t, biases)
        for qr, o in zip(qrows, outs):
            y_ref[0, qr, :] = (o * _silu(z_ref[0, qr, :])).astype(y_ref.dtype)
        return carry

    lax.fori_loop(0, n_rows // row_group_size, row_group, 0)

    if need_ctx:
        o, = attend([qc_ref[0] * scale], [kcb_ref[...]], [vct_ref[...]], [None])
        yc_ref[0] = (o * _silu(zc_ref[0])).astype(yc_ref.dtype)


def _natten_mixer(pc, pcc, bias, layer, need_ctx):
    b, t, _ = pc.shape
    tc = pcc.shape[1]
    npair = W_C // LANES
    n_rows = t // GRID_W
    lat = lambda part: pl.BlockSpec((1, t, LANES), lambda i, p, part=part: (i, 0, part * npair + p))
    ctx = lambda part: pl.BlockSpec((1, tc, LANES), lambda i, p, part=part: (i, 0, part * npair + p))
    in_specs = [lat(0), lat(1), lat(2), lat(3), ctx(1), ctx(2),
                pl.BlockSpec((None, 2 * WIN_ROWS - 1, 1, GRID_W, LANES), lambda i, p: (layer, 0, p, 0, 0))]
    args = [pc, pc, pc, pc, pcc, pcc, bias]
    out_specs = [pl.BlockSpec((1, t, LANES), lambda i, p: (i, 0, p))]
    out_shape = [jax.ShapeDtypeStruct((b, t, W_C), BF16)]
    if need_ctx:
        in_specs += [ctx(0), ctx(3)]
        args += [pcc, pcc]
        out_specs.append(pl.BlockSpec((1, tc, LANES), lambda i, p: (i, 0, p)))
        out_shape.append(jax.ShapeDtypeStruct((b, tc, W_C), BF16))
    scratch = [pltpu.VMEM((t, LANES), BF16), pltpu.VMEM((2, LANES, t), BF16),
               pltpu.VMEM((tc, LANES), BF16), pltpu.VMEM((LANES, tc), BF16)]
    res = pl.pallas_call(
        functools.partial(_natten_kernel, need_ctx=need_ctx, n_rows=n_rows),
        grid=(b, npair),
        in_specs=in_specs,
        out_specs=out_specs,
        out_shape=out_shape,
        scratch_shapes=scratch,
        compiler_params=_cparams("arbitrary", "arbitrary"),
        name="natten_mixer_ctx" if need_ctx else "natten_mixer",
    )(*args)
    return (res[0], res[1]) if need_ctx else (res[0], None)


def _natten_bias(rpb):
    col = np.arange(GRID_W)
    cs = np.clip(col - WIN_COLS // 2, 0, GRID_W - WIN_COLS)
    band = (col[None, :] >= cs[:, None]) & (col[None, :] < cs[:, None] + WIN_COLS)
    dc = np.clip(col[None, :] - col[:, None] + WIN_COLS - 1, 0, 2 * WIN_COLS - 2)
    pick = (np.arange(2 * WIN_COLS - 1)[:, None, None] == dc.T[None]) & band.T[None]
    tz = jnp.einsum('lhrm,mkc->lhrkc', rpb.astype(F32), jnp.asarray(pick, F32),
                    precision=lax.Precision.HIGHEST)
    tz = jnp.where(band.T[None, None, None], tz, NEG)
    depth = rpb.shape[0]
    tab = tz.reshape(depth, H_C // 2, 2, 2 * WIN_ROWS - 1, GRID_W, GRID_W)
    tab = jnp.transpose(tab, (0, 3, 1, 4, 2, 5))
    return tab.reshape(depth, 2 * WIN_ROWS - 1, H_C // 2, GRID_W, 2 * GRID_W)


def _outproj_kernel(ya_ref, yb_ref, yc_ref, x_ref, gt_ref, g_ref, wa_ref, wb_ref, wc_ref, o_ref):
    u = (_dot(ya_ref[0].astype(BF16), wa_ref[...]) + _dot(yb_ref[0].astype(BF16), wb_ref[...])
         + _dot(yc_ref[0].astype(BF16), wc_ref[...]))
    ms = jnp.mean(u * u, axis=-1, keepdims=True)
    o_ref[0] = x_ref[0] + gt_ref[0] * (u * lax.rsqrt(ms + EPS) * g_ref[...])


def _out_projection(ya, yb, yc, x, gt, g, wts, layer, tm):
    b, t, d = x.shape
    wa, wb, wc = wts
    bm = gt.shape[0]
    mod_map = (lambda i, j: (i, 0, 0)) if bm > 1 else (lambda i, j: (0, 0, 0))
    tile = lambda n: pl.BlockSpec((1, tm, n), lambda i, j: (i, j, 0))
    return pl.pallas_call(
        _outproj_kernel,
        grid=(b, t // tm),
        in_specs=[tile(ya.shape[2]), tile(yb.shape[2]), tile(yc.shape[2]), tile(d),
                  pl.BlockSpec((1, 1, d), mod_map)] + [_layer_spec(a, layer) for a in (g, wa, wb, wc)],
        out_specs=tile(d),
        out_shape=jax.ShapeDtypeStruct((b, t, d), F32),
        compiler_params=_cparams("arbitrary", "arbitrary"),
        name="out_projection",
    )(ya, yb, yc, x, gt, g, wa, wb, wc)


def _mlstm_weights(w):
    lead = w.shape[:-1]
    q = DH_B // 4
    blk = w[..., OFF_B:OFF_G].reshape(lead + (5, H_B, DH_B))
    gap = jnp.zeros(lead + (2, H_B, LANES // 2 - 2 * q), w.dtype)
    qk = blk[..., :2, :, :]
    qk = jnp.concatenate([qk[..., 0:q], qk[..., 2 * q:3 * q], gap, qk[..., q:2 * q], qk[..., 3 * q:], gap], axis=-1)
    rest = jnp.pad(blk[..., 2:, :, :], [(0, 0)] * (len(lead) + 2) + [(0, LANES - DH_B)])
    wb = jnp.concatenate([qk, rest], axis=-3).reshape(lead + (5 * H_B * LANES,))
    g = jnp.swapaxes(w[..., OFF_G:OFF_C].reshape(lead + (4, H_B)), -1, -2)
    wgt = jnp.pad(g, [(0, 0)] * (len(lead) + 1) + [(0, 4)]).reshape(lead + (H_B * 8,))
    return wb.astype(BF16), jnp.swapaxes(wgt, -1, -2).astype(BF16)


def _rope_tables(t):
    pos = np.arange(t)
    q = DH_B // 4
    inv = ROPE_BASE ** (-jnp.arange(0, 2 * q, 2, dtype=F32) / (2 * q))
    ang_r = jnp.asarray(pos // GRID_W, F32)[:, None] * inv[None, :]
    ang_c = jnp.asarray(pos % GRID_W, F32)[:, None] * inv[None, :]
    zeros = jnp.zeros((t, LANES // 2 - 2 * q), F32)
    cos_half = jnp.concatenate([jnp.cos(ang_r), jnp.cos(ang_c), zeros], axis=1)
    sin_half = jnp.concatenate([jnp.sin(ang_r), jnp.sin(ang_c), zeros], axis=1)
    return (jnp.concatenate([cos_half, cos_half], axis=1),
            jnp.concatenate([-sin_half, sin_half], axis=1))


def _pad_heads(v):
    v = v.reshape(v.shape[:-1] + (H_B, DH_B))
    v = jnp.pad(v, [(0, 0)] * (v.ndim - 1) + [(0, LANES - DH_B)])
    return v.reshape(v.shape[:-2] + (H_B * LANES,))


def kernel(x, c, ctx, c_ctx, w_mod, b_mod, g_pre, g_post, w_in, w_out, hgrn_lb, hgrn_gn, mlstm_gate_b, mlstm_gn, na_rpb):
    depth = w_in.shape[0]
    b, t, d = x.shape
    tc = ctx.shape[1]
    assert t % CHUNK == 0 and tc % CHUNK == 0 and t // GRID_W >= WIN_ROWS and t % GRID_W == 0
    assert w_in.shape[2] == P_IN and (1 << N_LEVELS) == CHUNK

    lb_cum = jnp.cumsum(jax.nn.softmax(hgrn_lb.astype(F32), axis=0), axis=0)
    lb_all = lb_cum - lb_cum[0]
    hgrn_tabs = _hgrn_tables()
    tris = _tri_tables()
    rope = _rope_tables(t)
    mod_rows = 16
    cc = jnp.concatenate([c.astype(F32), c_ctx.astype(F32)[None],
                          jnp.zeros((mod_rows - b - 1, d), F32)], axis=0)
    tm = min(256, t)
    tmo = min(512, t)
    tmc = min(256, tc)

    wb_all, wgt_all = _mlstm_weights(w_in)
    gbc_all = jnp.pad(jnp.swapaxes(mlstm_gate_b.astype(F32), 1, 2), ((0, 0), (0, 0), (0, 4)))
    in_wts = (w_in[:, :, :OFF_B].astype(BF16), wb_all, w_in[:, :, OFF_C:].astype(BF16), wgt_all,
              gbc_all.reshape(depth, H_B * 8, 1))
    wo_b = jnp.pad(w_out[:, W_A:W_A + W_B].reshape(depth, H_B, DH_B, d),
                   ((0, 0), (0, 0), (0, LANES - DH_B), (0, 0)))
    out_wts = (w_out[:, :W_A].astype(BF16), wo_b.reshape(depth, H_B * LANES, d).astype(BF16),
               w_out[:, W_A + W_B:].astype(BF16))
    g_pre3 = g_pre.astype(F32).reshape(depth, 1, d)
    g_post3 = g_post.astype(F32).reshape(depth, 1, d)
    na_bias = _natten_bias(na_rpb)
    gnb_all = _pad_heads(mlstm_gn.astype(F32)).reshape(depth, H_B, 1, LANES)

    for l in range(depth):
        need_ctx = l < depth - 1
        mod = _modulation(cc, w_mod[l], b_mod[l])
        sh, sc, gt = (mod[:b, i * d:(i + 1) * d].reshape(b, 1, d) for i in range(3))
        shc, scc, gtc = (mod[b:b + 1, i * d:(i + 1) * d].reshape(1, 1, d) for i in range(3))

        pa, pb, pc, pgt = _in_projection(x, sh, sc, g_pre3, in_wts, l, rope, tm)
        pac, pbc, pcc, pgtc = _in_projection(ctx, shc, scc, g_pre3, in_wts, l, None, tmc)

        ya, yac = _hgrn_mixer(pa, pac, lb_all[l], hgrn_gn[l], hgrn_tabs, need_ctx)
        yb, ybc = _mlstm_mixer(pb, pgt, pbc, pgtc, gnb_all[l], tris, need_ctx)
        yc, ycc = _natten_mixer(pc, pcc, na_bias, l, need_ctx)

        x = _out_projection(ya, yb, yc, x, gt, g_post3, out_wts, l, tmo)
        if need_ctx:
            ctx = _out_projection(yac, ybc, ycc, ctx, gtc, g_post3, out_wts, l, tmc)
    return x
```

```python
import functools
import math

import numpy as np
import jax
import jax.numpy as jnp
from jax import lax
from jax.experimental import pallas as pl
from jax.experimental.pallas import tpu as pltpu

F32 = jnp.float32
BF16 = jnp.bfloat16

LANES = 128
GRID_W = 64
W_A, H_A, DH_A = 256, 4, 64
W_B, H_B, DH_B = 384, 4, 96
W_C, H_C, DH_C = 384, 6, 64
WIN_ROWS, WIN_COLS = 8, 16
ROPE_BASE = 10000.0
EPS = 1e-6
NEG = -1e30
LOG2_E = math.log2(math.e)
CHUNK = 128
N_LEVELS = 7
NA_ROW_UNROLL = 8
NA_STAGE_GROUP = 4
CHUNK_UNROLL = 4
MLSTM_GROUP = 8
OFF_B = 5 * W_A
OFF_G = OFF_B + 5 * W_B
OFF_C = OFF_G + 4 * H_B
P_IN = OFF_C + 4 * W_C
ONE_LANE = DH_B
VMEM_LIMIT = 56 * 1024 * 1024


def _cparams(*sem):
    return pltpu.CompilerParams(dimension_semantics=sem, vmem_limit_bytes=VMEM_LIMIT)


def _dot(a, b):
    return jnp.dot(a, b, preferred_element_type=F32)


def _dot_nt(a, b):
    return lax.dot_general(a, b, (((1,), (1,)), ((), ())), preferred_element_type=F32)


def _dot_tn(a, b):
    return lax.dot_general(a, b, (((0,), (0,)), ((), ())), preferred_element_type=F32)


def _chunk_loop(n, body):
    u = math.gcd(n, CHUNK_UNROLL)

    def step(i, carry):
        for j in range(u):
            body(i * u + j, carry)
        return carry

    return lax.fori_loop(0, n // u, step, 0)


def _grouped_loop(n, body, group=CHUNK_UNROLL):
    u = math.gcd(n, group)

    def step(i, carry):
        body([i * u + j for j in range(u)])
        return carry

    return lax.fori_loop(0, n // u, step, 0)


def _split3(x):
    hi = x.astype(BF16)
    r = x - hi.astype(F32)
    mid = r.astype(BF16)
    lo = (r - mid.astype(F32)).astype(BF16)
    return hi, mid, lo


def _exact_left(t01, x):
    n = x.shape[1]
    r = _dot(t01, jnp.concatenate(_split3(x), axis=1))
    return r[:, :n] + r[:, n:2 * n] + r[:, 2 * n:]


def _exact_right(x, t01):
    m = x.shape[0]
    pieces = jnp.concatenate([p.astype(F32) for p in _split3(x)], axis=0).astype(BF16)
    r = _dot(pieces, t01)
    return r[:m] + r[m:2 * m] + r[2 * m:]


def _sigmoid(x):
    return 1.0 / (1.0 + jnp.exp(-x))


def _silu(x):
    return x * _sigmoid(x)


def _log_sigmoid(x):
    return jnp.minimum(x, 0.0) - jnp.log(1.0 + jnp.exp(-jnp.abs(x)))


def _mod_kernel(c_ref, w_ref, b_ref, o_ref):
    s = _silu(c_ref[...])
    o_ref[...] = _dot(s.astype(BF16), w_ref[...].astype(BF16)) + b_ref[...]


def _modulation(cc, w, b):
    rows, d = cc.shape
    n = w.shape[1]
    tn = d
    assert n % tn == 0
    return pl.pallas_call(
        _mod_kernel,
        grid=(n // tn,),
        in_specs=[pl.BlockSpec((rows, d), lambda j: (0, 0)),
                  pl.BlockSpec((d, tn), lambda j: (0, j)),
                  pl.BlockSpec((1, tn), lambda j: (0, j))],
        out_specs=pl.BlockSpec((rows, tn), lambda j: (0, j)),
        out_shape=jax.ShapeDtypeStruct((rows, n), F32),
        compiler_params=_cparams("arbitrary"),
        name="modulation",
    )(cc, w, b.reshape(1, n))


def _inproj_kernel(*refs, rotary):
    if rotary:
        (x_ref, sh_ref, sc_ref, g_ref, wa_ref, wb_ref, wc_ref, wgt_ref, gbc_ref,
         cos_ref, sin_ref, pa_ref, pai_ref, pbqk_ref, pbv_ref, pcb_ref, pcz_ref, pgt_ref) = refs
    else:
        (x_ref, sh_ref, sc_ref, g_ref, wa_ref, wb_ref, wc_ref, wgt_ref, gbc_ref,
         pa_ref, pai_ref, pbqk_ref, pbv_ref, pcb_ref, pcz_ref, pgt_ref) = refs
    x = x_ref[0]
    ms = jnp.mean(x * x, axis=-1, keepdims=True)
    h = x * lax.rsqrt(ms + EPS) * g_ref[...]
    h = h * (1.0 + sc_ref[0]) + sh_ref[0]
    hb = h.astype(BF16)
    a = _dot(hb, wa_ref[...])
    pa_ref[0, :, :3 * W_A] = a[:, :3 * W_A]
    pai_ref[0] = a[:, 3 * W_A:4 * W_A].astype(BF16)
    pa_ref[0, :, 3 * W_A:] = a[:, 4 * W_A:]
    c = _dot(hb, wc_ref[...])
    pcb_ref[0] = c[:, :3 * W_C].astype(BF16)
    pcz_ref[0] = c[:, 3 * W_C:]
    pgt_ref[0] = _dot_nt(wgt_ref[...], hb) + gbc_ref[...]
    hw = H_B * LANES
    scale = DH_B ** -0.5
    for part in range(5):
        p = _dot(hb, wb_ref[:, part * hw:(part + 1) * hw])
        if part == 0:
            p = p * scale
        if part >= 2:
            pbv_ref[0, :, (part - 2) * hw:(part - 1) * hw] = p
        elif rotary:
            cos = cos_ref[...]
            sin = sin_ref[...]
            for hd in range(H_B):
                ph = p[:, hd * LANES:(hd + 1) * LANES]
                pbqk_ref[0, :, part * hw + hd * LANES:part * hw + (hd + 1) * LANES] = (
                    ph * cos + pltpu.roll(ph, LANES // 2, 1) * sin).astype(BF16)
        else:
            pbqk_ref[0, :, part * hw:(part + 1) * hw] = p.astype(BF16)


def _layer_spec(a, layer):
    return pl.BlockSpec((None,) + a.shape[1:], lambda i, j: (layer, 0, 0))


def _in_projection(x, sh, sc, g, wts, layer, rope, tm):
    b, t, d = x.shape
    wa, wb, wc, wgt, gbc = wts
    bm = sh.shape[0]
    mod_map = (lambda i, j: (i, 0, 0)) if bm > 1 else (lambda i, j: (0, 0, 0))
    in_specs = [pl.BlockSpec((1, tm, d), lambda i, j: (i, j, 0)),
                pl.BlockSpec((1, 1, d), mod_map),
                pl.BlockSpec((1, 1, d), mod_map)] + [_layer_spec(a, layer) for a in (g, wa, wb, wc, wgt, gbc)]
    args = [x, sh, sc, g, wa, wb, wc, wgt, gbc]
    if rope is not None:
        in_specs += [pl.BlockSpec((tm, LANES), lambda i, j: (j, 0))] * 2
        args += list(rope)
    ng = wgt.shape[1]
    hw = H_B * LANES
    widths = [(4 * W_A, F32), (W_A, BF16), (2 * hw, BF16), (3 * hw, F32), (3 * W_C, BF16), (W_C, F32)]
    out_specs = ([pl.BlockSpec((1, tm, n), lambda i, j: (i, j, 0)) for n, _ in widths]
                 + [pl.BlockSpec((1, ng, tm), lambda i, j: (i, 0, j))])
    out_shape = ([jax.ShapeDtypeStruct((b, t, n), dt) for n, dt in widths]
                 + [jax.ShapeDtypeStruct((b, ng, t), F32)])
    return pl.pallas_call(
        functools.partial(_inproj_kernel, rotary=rope is not None),
        grid=(b, t // tm),
        in_specs=in_specs,
        out_specs=out_specs,
        out_shape=out_shape,
        compiler_params=_cparams("arbitrary", "arbitrary"),
        name="in_projection_rope" if rope is not None else "in_projection",
    )(*args)


def _hgrn_tables():
    L, nl = CHUNK, N_LEVELS
    idx = np.arange(L)
    t = idx[:, None]
    u = idx[None, :]
    tabq, levels = [], []
    for rev in (False, True):
        tabq.append(((u <= t) if not rev else (u >= t)).astype(np.float32))
        x = t ^ u
        lvl = np.where(x > 0, np.floor(np.log2(np.maximum(x, 1))), float(nl))
        valid = (u <= t) if not rev else (u >= t)
        lvl = np.where(valid, lvl, -1.0).astype(np.float32)
        levels.append(np.concatenate([lvl, lvl], axis=1))
    return jnp.asarray(np.stack(tabq), dtype=BF16), jnp.asarray(np.stack(levels), dtype=F32)


def _tri_tables():
    L = CHUNK
    idx = np.arange(L)
    t = idx[:, None]
    u = idx[None, :]
    fwd = (u <= t).astype(np.float32)
    bwd = (u >= t).astype(np.float32)
    spread = np.zeros((16, 2 * L), np.float32)
    spread[0:3, :L] = 1.0
    spread[3:6, L:] = 1.0
    return (jnp.asarray(np.concatenate([fwd.T, bwd.T], axis=1), dtype=BF16), jnp.asarray(spread, dtype=BF16))


def _hgrn_kernel(*refs, need_ctx, n_lat, n_ctx):
    (q_ref, ff_ref, fb_ref, i_ref, z_ref, qc_ref, ffc_ref, fbc_ref, ic_ref, zc_ref,
     lb_ref, gn_ref, tabq_ref, lvl_ref) = refs[:14]
    if need_ctx:
        y_ref, yc_ref, u_ref, dec_ref, gate_ref = refs[14:]
    else:
        y_ref, u_ref, dec_ref, gate_ref = refs[14:]
        yc_ref = None
    L, nl = CHUNK, N_LEVELS
    lane = lax.broadcasted_iota(jnp.int32, (1, LANES), 1)
    hm0 = lane < DH_A
    row_h = lax.broadcasted_iota(jnp.int32, (LANES, LANES), 0) // DH_A
    col_h = lax.broadcasted_iota(jnp.int32, (LANES, LANES), 1) // DH_A
    same_head = row_h == col_h
    gn = gn_ref[0]
    segs = ((0, n_ctx, qc_ref, (ffc_ref, fbc_ref), ic_ref, zc_ref, yc_ref),
            (n_ctx, n_lat, q_ref, (ff_ref, fb_ref), i_ref, z_ref, y_ref))

    def decay(fl, lbd):
        a0 = jnp.log(lbd)
        b0 = jnp.log(1.0 - lbd) + _log_sigmoid(fl)
        logf = jnp.maximum(a0, b0) + jnp.log(1.0 + jnp.exp(-jnp.abs(a0 - b0)))
        return logf * LOG2_E, (1.0 - lbd) * _sigmoid(-fl)

    def rows_of(c):
        return pl.ds(pl.multiple_of(c * L, L), L)

    tok = lax.broadcasted_iota(jnp.int32, (L, 1), 0)

    def level_exponents(logf, bcum, d):
        prev, nxt = pltpu.roll(logf, 1, 0), pltpu.roll(logf, L - 1, 0)
        if d == 0:
            eqs = [logf, logf + jnp.where((tok % 4) == 3, prev, 0.0)]
            eks = [None, jnp.where((tok % 4) == 0, nxt, 0.0)]
        else:
            eqs = [logf, logf + jnp.where((tok % 4) == 0, nxt, 0.0)]
            eks = [None, jnp.where((tok % 4) == 3, prev, 0.0)]
        for lv in range(2, nl):
            h = 1 << lv
            r = h - 1 if d == 0 else h
            g = jnp.concatenate([jnp.broadcast_to(bcum[blk + r:blk + r + 1], (2 * h, LANES))
                                 for blk in range(0, L, 2 * h)], axis=0)
            eqs.append(bcum - g)
            eks.append(g - bcum)
        return eqs, eks

    def chunk_total(bcum, d):
        return bcum[L - 1:L] if d == 0 else bcum[0:1]

    for base, n, _, f_refs, ai_ref, _, _ in segs:
        def increments(cs, base=base, f_refs=f_refs, ai_ref=ai_ref):
            jd = [(j, d) for j in range(len(cs)) for d in range(2)]
            lk = {(j, d): decay(f_refs[d][0, rows_of(cs[j]), :], lb_ref[d, 0]) for j, d in jd}
            bcum = {(j, d): _exact_left(tabq_ref[d], lk[(j, d)][0]) for j, d in jd}
            b_tot = {(j, d): chunk_total(bcum[(j, d)], d) for j, d in jd}
            vb = [ai_ref[0, rows_of(c), :].astype(BF16) for c in cs]
            kr = {x: (lk[x][1] * jnp.exp2(b_tot[x] - bcum[x])).astype(BF16) for x in jd}
            u = {(j, d): _dot_tn(kr[(j, d)], vb[j]) for j, d in jd}
            dec = {x: jnp.broadcast_to(jnp.exp2(b_tot[x]), (LANES, LANES)).T for x in jd}
            for j, d in jd:
                u_ref[d, base + cs[j]] = jnp.where(same_head, u[(j, d)], 0.0)
                dec_ref[d, base + cs[j]] = dec[(j, d)]
                tok_rows = rows_of(base + cs[j])
                gate_ref[d, 0, tok_rows, :] = lk[(j, d)][0]
                gate_ref[d, 1, tok_rows, :] = lk[(j, d)][1]
                gate_ref[d, 2, tok_rows, :] = bcum[(j, d)]
        _grouped_loop(n, increments)

    def advance(d, idx, s):
        u = u_ref[d, idx]
        u_ref[d, idx] = s
        return s * dec_ref[d, idx] + u

    states = (jnp.zeros((LANES, LANES), F32),) * 2
    for base, n, *_ in segs:
        def scan_step(i, states, base=base, n=n):
            return (advance(0, base + i, states[0]), advance(1, base + n - 1 - i, states[1]))
        states = lax.fori_loop(0, n, scan_step, states)

    def finish(o, z):
        sq = o * o
        ms0 = jnp.sum(jnp.where(hm0, sq, 0.0), axis=-1, keepdims=True)
        ms1 = jnp.sum(jnp.where(hm0, 0.0, sq), axis=-1, keepdims=True)
        ms = jnp.where(hm0, ms0, ms1) * (1.0 / DH_A)
        return o * lax.rsqrt(ms + EPS) * gn * _silu(z)

    for base, n, aq_ref, f_refs, ai_ref, zz_ref, out_ref in segs:
        if out_ref is None:
            continue

        def outputs(cs, base=base, aq_ref=aq_ref, f_refs=f_refs, ai_ref=ai_ref, zz_ref=zz_ref,
                    out_ref=out_ref):
            js = range(len(cs))
            jd = [(j, d) for j in js for d in range(2)]
            q = [_silu(aq_ref[0, rows_of(c), :]) * (DH_A ** -0.5) for c in cs]
            lk = {(j, d): (gate_ref[d, 0, rows_of(base + cs[j]), :], gate_ref[d, 1, rows_of(base + cs[j]), :])
                  for j, d in jd}
            bcum = {(j, d): gate_ref[d, 2, rows_of(base + cs[j]), :] for j, d in jd}
            ex = {(j, d): level_exponents(lk[(j, d)][0], bcum[(j, d)], d) for j, d in jd}
            a = {x: jnp.zeros((L, 2 * L), F32) for x in jd}
            zero = jnp.zeros((DH_A, L), BF16)
            for lv in range(nl + 1):
                at_level = [lvl_ref[d] == float(lv) for d in range(2)]
                for j, d in jd:
                    k = lk[(j, d)][1]
                    eqs, eks = ex[(j, d)]
                    if lv < nl:
                        qt = q[j] * jnp.exp2(eqs[lv])
                        kl = k if eks[lv] is None else k * jnp.exp2(eks[lv])
                    else:
                        qt, kl = q[j], k
                    ktl = kl.T.astype(BF16)
                    w = jnp.concatenate([jnp.concatenate([ktl[:DH_A], zero], axis=0),
                                         jnp.concatenate([zero, ktl[DH_A:]], axis=0)], axis=1)
                    a[(j, d)] = jnp.where(at_level[d], _dot(qt.astype(BF16), w), a[(j, d)])
            ai = [ai_ref[0, rows_of(c), :] for c in cs]
            vs = [jnp.concatenate([jnp.where(hm0, x, 0.0), jnp.where(hm0, 0.0, x)], axis=0).astype(BF16)
                  for x in ai]
            intra = {(j, d): _dot(a[(j, d)].astype(BF16), vs[j]) for j, d in jd}
            inter = {(j, d): _dot((q[j] * jnp.exp2(bcum[(j, d)])).astype(BF16),
                                  u_ref[d, base + cs[j]].astype(BF16)) for j, d in jd}
            for j, c in enumerate(cs):
                o = intra[(j, 0)] + inter[(j, 0)] + intra[(j, 1)] + inter[(j, 1)]
                out_ref[0, rows_of(c), :] = finish(o, zz_ref[0, rows_of(c), :]).astype(out_ref.dtype)
        _grouped_loop(n, outputs)


def _hgrn_mixer(pa, pai, pac, paic, lb, gn, tabs, need_ctx):
    b, t, _ = pa.shape
    tc = pac.shape[1]
    npair = W_A // LANES
    n_chunks = (t + tc) // CHUNK
    lat = lambda part: pl.BlockSpec((1, t, LANES), lambda i, p, part=part: (i, 0, part * npair + p))
    ctx = lambda part: pl.BlockSpec((1, tc, LANES), lambda i, p, part=part: (i, 0, part * npair + p))
    whole = lambda a: pl.BlockSpec(a.shape, lambda i, p: (0, 0, 0))
    in_specs = ([lat(0), lat(1), lat(2), lat(0), lat(3), ctx(0), ctx(1), ctx(2), ctx(0), ctx(3),
                 pl.BlockSpec((2, 1, 1, LANES), lambda i, p: (0, p, 0, 0)),
                 pl.BlockSpec((1, 1, LANES), lambda i, p: (p, 0, 0))] + [whole(a) for a in tabs])
    out_specs = [pl.BlockSpec((1, t, LANES), lambda i, p: (i, 0, p))]
    out_shape = [jax.ShapeDtypeStruct((b, t, W_A), BF16)]
    scratch = [pltpu.VMEM((2, n_chunks, LANES, LANES), F32)] * 2
    scratch.append(pltpu.VMEM((2, 3, t + tc, LANES), F32))
    if need_ctx:
        out_specs.append(pl.BlockSpec((1, tc, LANES), lambda i, p: (i, 0, p)))
        out_shape.append(jax.ShapeDtypeStruct((b, tc, W_A), BF16))
    res = pl.pallas_call(
        functools.partial(_hgrn_kernel, need_ctx=need_ctx, n_lat=t // CHUNK, n_ctx=tc // CHUNK),
        grid=(b, npair),
        in_specs=in_specs,
        out_specs=out_specs,
        out_shape=out_shape,
        scratch_shapes=scratch,
        compiler_params=_cparams("arbitrary", "arbitrary"),
        name="hgrn2_mixer_ctx" if need_ctx else "hgrn2_mixer",
    )(pa, pa, pa, pai, pa, pac, pac, pac, paic, pac,
      lb.reshape(2, npair, 1, LANES), gn.reshape(npair, 1, LANES), *tabs)
    return (res[0], res[1]) if need_ctx else (res[0], None)


def _mlstm_kernel(*refs, need_ctx, n_lat, n_ctx):
    (q_ref, k_ref, v_ref, o_ref, z_ref, qc_ref, kc_ref, vc_ref, oc_ref, zc_ref,
     gt_ref, gtc_ref, gn_ref, trit_ref, ones_ref) = refs[:15]
    if need_ctx:
        y_ref, yc_ref, c_ref, st_ref, vt_ref = refs[15:]
    else:
        y_ref, c_ref, st_ref, vt_ref = refs[15:]
        yc_ref = None
    L = CHUNK
    sub = lax.broadcasted_iota(jnp.int32, (LANES, 1), 0)
    rows_i = lax.broadcasted_iota(jnp.int32, (L, L), 0)
    cols_i = lax.broadcasted_iota(jnp.int32, (L, L), 1)
    gn = gn_ref[0]
    segs = ((0, n_ctx, qc_ref, kc_ref, vc_ref, oc_ref, zc_ref, gtc_ref, yc_ref),
            (n_ctx, n_lat, q_ref, k_ref, v_ref, o_ref, z_ref, gt_ref, y_ref))
    B_TOT, M_LOC, M_PREV = 0, 1, 2

    def rows_of(c):
        return pl.ds(pl.multiple_of(c * L, L), L)

    loop = _chunk_loop

    def row_gates(grow):
        er = _exact_right(_log_sigmoid(grow), trit_ref[...])
        out = []
        for d in range(2):
            b_row = er[2 + d:3 + d, d * L:(d + 1) * L]
            out.append((grow[d:d + 1, :], b_row, b_row[:, L - 1:L] if d == 0 else b_row[:, 0:1]))
        return out

    for base, n, _, _, vv, *_ in segs:
        def transpose_values(cs, base=base, vv=vv):
            tiles = [jnp.where(sub == ONE_LANE, 1.0, vv[0, rows_of(c), :].T) for c in cs]
            for c, x in zip(cs, tiles):
                vt_ref[:, rows_of(base + c)] = x
        _grouped_loop(n, transpose_values, MLSTM_GROUP)

    for base, n, _, kk, _, _, _, ggt, _ in segs:
        def local(cs, base=base, kk=kk, ggt=ggt):
            jd = [(j, d) for j in range(len(cs)) for d in range(2)]
            gates = [row_gates(ggt[0, :, rows_of(c)]) for c in cs]
            w = {(j, d): gates[j][d][2] + gates[j][d][0] - gates[j][d][1] for j, d in jd}
            m_loc = {x: jnp.max(w[x], axis=1, keepdims=True) for x in jd}
            ew = {x: jnp.exp(w[x] - m_loc[x]) for x in jd}
            kb = [kk[0, rows_of(c), :].astype(BF16) for c in cs]
            vat = [vt_ref[:, rows_of(base + c)] for c in cs]
            lhs = {(j, d): (vat[j] * ew[(j, d)]).astype(BF16) for j, d in jd}
            out = {(j, d): _dot(lhs[(j, d)], kb[j]) for j, d in jd}
            for j, d in jd:
                c_ref[d, base + cs[j]] = out[(j, d)]
                st_ref[d, base + cs[j], B_TOT:B_TOT + 1, :] = jnp.broadcast_to(gates[j][d][2], (1, LANES))
                st_ref[d, base + cs[j], M_LOC:M_LOC + 1, :] = jnp.broadcast_to(m_loc[(j, d)], (1, LANES))
        _grouped_loop(n, local, MLSTM_GROUP)

    def advance(d, idx, state):
        c_prev, m_prev = state
        c_loc = c_ref[d, idx]
        b_tot = st_ref[d, idx, B_TOT:B_TOT + 1, :]
        m_loc = st_ref[d, idx, M_LOC:M_LOC + 1, :]
        c_ref[d, idx] = c_prev
        st_ref[d, idx, M_PREV:M_PREV + 1, :] = m_prev
        m_new = jnp.maximum(b_tot + m_prev, m_loc)
        return (jnp.exp(b_tot + m_prev - m_new) * c_prev + jnp.exp(m_loc - m_new) * c_loc, m_new)

    s0 = (jnp.zeros((LANES, LANES), F32), jnp.zeros((1, LANES), F32))
    states = (s0, s0)
    for base, n, *_ in segs:
        def scan_step(i, states, base=base, n=n):
            return (advance(0, base + i, states[0]), advance(1, base + n - 1 - i, states[1]))
        states = lax.fori_loop(0, n, scan_step, states)

    for base, n, qq, kk, _, oo, zz, ggt, out_ref in segs:
        if out_ref is None:
            continue

        def outputs(cs, base=base, qq=qq, kk=kk, oo=oo, zz=zz, ggt=ggt, out_ref=out_ref):
            js = range(len(cs))
            jd = [(j, d) for j in js for d in range(2)]
            qb = [qq[0, rows_of(c), :].astype(BF16) for c in cs]
            sq = [_dot_nt(jnp.concatenate([kk[0, rows_of(c), :].astype(BF16),
                                           c_ref[0, base + c].astype(BF16),
                                           c_ref[1, base + c].astype(BF16)], axis=0), qb[j])
                  for j, c in enumerate(cs)]
            gates = [row_gates(ggt[0, :, rows_of(c)]) for c in cs]
            pieces = [jnp.concatenate(
                [p.astype(F32) for d in range(2) for p in _split3(gates[j][d][0] - gates[j][d][1])]
                + [jnp.zeros((10, L), F32)], axis=0).astype(BF16) for j in js]
            r_bc = [_dot_tn(x, ones_ref[...]) for x in pieces]
            valid = [rows_i <= cols_i, rows_i >= cols_i]
            dm = {(j, d): jnp.where(valid[d], gates[j][d][1] + r_bc[j][:, d * L:(d + 1) * L], NEG)
                  for j, d in jd}
            inter = {(j, d): gates[j][d][1] + st_ref[d, base + cs[j], M_PREV:M_PREV + 1, 0:1] for j, d in jd}
            m_t = {x: jnp.maximum(jnp.max(dm[x], axis=0, keepdims=True), inter[x]) for x in jd}
            p = {(j, d): sq[j][:L] * jnp.exp(dm[(j, d)] - m_t[(j, d)]) for j, d in jd}
            e_in = {x: jnp.exp(inter[x] - m_t[x]) for x in jd}
            vat = [vt_ref[:, rows_of(base + c)].astype(BF16) for c in cs]
            pv = {(j, d): _dot(vat[j], p[(j, d)].astype(BF16)) for j, d in jd}
            hts = []
            for j in js:
                ht = jnp.zeros((LANES, L), F32)
                for d in range(2):
                    x = (j, d)
                    qct = sq[j][(1 + d) * L:(2 + d) * L]
                    num = pv[x] + e_in[x] * qct
                    den = jnp.sum(p[x], axis=0, keepdims=True) + e_in[x] * qct[ONE_LANE:ONE_LANE + 1, :]
                    ht = ht + num / jnp.maximum(jnp.abs(den), jnp.exp(-m_t[x]))
                hts.append(jnp.where(sub < DH_B, ht, 0.0))
            ms = [jnp.sum(ht * ht, axis=0, keepdims=True) * (1.0 / DH_B) for ht in hts]
            h = [(ht * lax.rsqrt(m + EPS)).T for ht, m in zip(hts, ms)]
            for j, c in enumerate(cs):
                rows = rows_of(c)
                y = _sigmoid(oo[0, rows, :]) * (h[j] * gn) * _silu(zz[0, rows, :])
                out_ref[0, rows, :] = y.astype(out_ref.dtype)
        _grouped_loop(n, outputs, MLSTM_GROUP)


def _mlstm_mixer(pbqk, pbv, pgt, pbqkc, pbvc, pgtc, gn, tris, need_ctx):
    b, t, _ = pbv.shape
    tc = pbvc.shape[1]
    trit, spread = tris
    lat = lambda part: pl.BlockSpec((1, t, LANES), lambda i, h, part=part: (i, 0, part * H_B + h))
    ctx = lambda part: pl.BlockSpec((1, tc, LANES), lambda i, h, part=part: (i, 0, part * H_B + h))
    in_specs = ([lat(0), lat(1), lat(0), lat(1), lat(2), ctx(0), ctx(1), ctx(0), ctx(1), ctx(2)] +
                [pl.BlockSpec((1, 8, t), lambda i, h: (i, h, 0)),
                 pl.BlockSpec((1, 8, tc), lambda i, h: (i, h, 0)),
                 pl.BlockSpec((1, 1, LANES), lambda i, h: (h, 0, 0)),
                 pl.BlockSpec(trit.shape, lambda i, h: (0, 0)),
                 pl.BlockSpec(spread.shape, lambda i, h: (0, 0))])
    out_specs = [pl.BlockSpec((1, t, LANES), lambda i, h: (i, 0, h))]
    out_shape = [jax.ShapeDtypeStruct((b, t, H_B * LANES), BF16)]
    n_chunks = (t + tc) // CHUNK
    scratch = [pltpu.VMEM((2, n_chunks, LANES, LANES), F32),
               pltpu.VMEM((2, n_chunks, 8, LANES), F32),
               pltpu.VMEM((LANES, t + tc), F32)]
    if need_ctx:
        out_specs.append(pl.BlockSpec((1, tc, LANES), lambda i, h: (i, 0, h)))
        out_shape.append(jax.ShapeDtypeStruct((b, tc, H_B * LANES), BF16))
    res = pl.pallas_call(
        functools.partial(_mlstm_kernel, need_ctx=need_ctx, n_lat=t // CHUNK, n_ctx=tc // CHUNK),
        grid=(b, H_B),
        in_specs=in_specs,
        out_specs=out_specs,
        out_shape=out_shape,
        scratch_shapes=scratch,
        compiler_params=_cparams("arbitrary", "arbitrary"),
        name="mlstm_mixer_ctx" if need_ctx else "mlstm_mixer",
    )(pbqk, pbqk, pbv, pbv, pbv, pbqkc, pbqkc, pbvc, pbvc, pbvc, pgt, pgtc, gn, trit, spread)
    return (res[0], res[1]) if need_ctx else (res[0], None)


def _natten_kernel(*refs, need_ctx, n_rows):
    if need_ctx:
        (q_ref, k_ref, v_ref, z_ref, kc_ref, vc_ref, bias_ref, qc_ref, zc_ref, y_ref, yc_ref,
         vt_ref, vct_ref) = refs
    else:
        (q_ref, k_ref, v_ref, z_ref, kc_ref, vc_ref, bias_ref, y_ref, vt_ref, vct_ref) = refs
    lane = lax.broadcasted_iota(jnp.int32, (1, LANES), 1)
    hm0 = lane < DH_C
    scale = DH_C ** -0.5
    win = WIN_ROWS * GRID_W
    t = k_ref.shape[1]
    tc = kc_ref.shape[1]
    blk = LANES

    def transposed(tile):
        return tile.astype(F32).T.astype(BF16)

    nb = t // blk
    group = math.gcd(nb, NA_STAGE_GROUP)
    row_group_size = math.gcd(n_rows, NA_ROW_UNROLL)

    def blocks(i):
        return [pl.ds(pl.multiple_of((i * group + j) * blk, blk), blk) for j in range(group)]

    def stage(i, carry):
        tiles = [transposed(v_ref[0, rows, :]) for rows in blocks(i)]
        for rows, x in zip(blocks(i), tiles):
            vt_ref[0, :, rows] = x
        return carry

    lax.fori_loop(0, nb // group, stage, 0)

    def shifted(start):
        src = start + GRID_W
        if not isinstance(src, int):
            src = pl.multiple_of(src, GRID_W)
        return transposed(v_ref[0, pl.ds(src, blk), :])

    def stage_shifted(i, carry):
        tiles = [shifted(rows.start) for rows in blocks(i)]
        for rows, x in zip(blocks(i), tiles):
            vt_ref[1, :, rows] = x
        return carry

    lax.fori_loop(0, nb // group - 1, stage_shifted, 0)
    tail = [(nb - group + j) * blk for j in range(group - 1)]
    tiles = [shifted(start) for start in tail]
    for start, x in zip(tail, tiles):
        vt_ref[1, :, start:start + blk] = x
    last = v_ref[0, t - blk:t, :].astype(F32).T
    vt_ref[1, :, t - blk:t] = pltpu.roll(last, GRID_W, 1).astype(BF16)

    tiles = [transposed(vc_ref[0, i * blk:(i + 1) * blk, :]) for i in range(tc // blk)]
    for i, x in enumerate(tiles):
        vct_ref[:, i * blk:(i + 1) * blk] = x

    def attend(qs, keys, vals_t, biases):
        n = qs[0].shape[0]
        q2 = [jnp.concatenate([jnp.where(hm0, q, 0.0), jnp.where(hm0, 0.0, q)], axis=0).astype(BF16)
              for q in qs]
        lt = [_dot_nt(k, q) for k, q in zip(keys, q2)]
        lt = [x if b is None else jnp.concatenate([x[:b.shape[0]] + b, x[b.shape[0]:]], axis=0)
              for x, b in zip(lt, biases)]
        p = [jnp.exp(x - jnp.max(x, axis=0, keepdims=True)) for x in lt]
        den = [jnp.sum(x, axis=0, keepdims=True) for x in p]
        ot = [_dot(v, x.astype(BF16)) for v, x in zip(vals_t, p)]
        o = [(x / d).T for x, d in zip(ot, den)]
        return [jnp.where(hm0, x[:n], x[n:]) for x in o]

    def row_group(i, carry):
        rows = [i * row_group_size + j for j in range(row_group_size)]
        rs = [jnp.clip(r - WIN_ROWS // 2, 0, n_rows - WIN_ROWS) for r in rows]
        qrows = [pl.ds(pl.multiple_of(r * GRID_W, GRID_W), GRID_W) for r in rows]
        keys, vals_t, biases = [], [], []
        for r, s in zip(rows, rs):
            par = s % 2
            wrows = pl.ds(pl.multiple_of(s * GRID_W, GRID_W), win)
            wlanes = pl.ds(pl.multiple_of((s - par) * GRID_W, LANES), win)
            keys.append(jnp.concatenate([k_ref[0, wrows, :], kc_ref[0]], axis=0))
            vals_t.append(jnp.concatenate([vt_ref[par, :, wlanes], vct_ref[...]], axis=1))
            dr0 = s - r + (WIN_ROWS - 1)
            biases.append(jnp.concatenate([bias_ref[dr0 + j, 0] for j in range(WIN_ROWS)], axis=0))
        outs = attend([q_ref[0, qr, :].astype(F32) * scale for qr in qrows], keys, vals_t, biases)
        for qr, o in zip(qrows, outs):
            y_ref[0, qr, :] = (o * _silu(z_ref[0, qr, :])).astype(y_ref.dtype)
        return carry

    lax.fori_loop(0, n_rows // row_group_size, row_group, 0)

    if need_ctx:
        o, = attend([qc_ref[0].astype(F32) * scale], [kc_ref[0]], [vct_ref[...]], [None])
        yc_ref[0] = (o * _silu(zc_ref[0])).astype(yc_ref.dtype)


def _natten_mixer(pcb, pcz, pcbc, pczc, bias, layer, need_ctx):
    b, t, _ = pcb.shape
    tc = pcbc.shape[1]
    npair = W_C // LANES
    n_rows = t // GRID_W
    lat = lambda part: pl.BlockSpec((1, t, LANES), lambda i, p, part=part: (i, 0, part * npair + p))
    ctx = lambda part: pl.BlockSpec((1, tc, LANES), lambda i, p, part=part: (i, 0, part * npair + p))
    in_specs = [lat(0), lat(1), lat(2), lat(0), ctx(1), ctx(2),
                pl.BlockSpec((None, 2 * WIN_ROWS - 1, 1, GRID_W, LANES), lambda i, p: (layer, 0, p, 0, 0))]
    args = [pcb, pcb, pcb, pcz, pcbc, pcbc, bias]
    out_specs = [pl.BlockSpec((1, t, LANES), lambda i, p: (i, 0, p))]
    out_shape = [jax.ShapeDtypeStruct((b, t, W_C), BF16)]
    if need_ctx:
        in_specs += [ctx(0), ctx(0)]
        args += [pcbc, pczc]
        out_specs.append(pl.BlockSpec((1, tc, LANES), lambda i, p: (i, 0, p)))
        out_shape.append(jax.ShapeDtypeStruct((b, tc, W_C), BF16))
    scratch = [pltpu.VMEM((2, LANES, t), BF16), pltpu.VMEM((LANES, tc), BF16)]
    res = pl.pallas_call(
        functools.partial(_natten_kernel, need_ctx=need_ctx, n_rows=n_rows),
        grid=(b, npair),
        in_specs=in_specs,
        out_specs=out_specs,
        out_shape=out_shape,
        scratch_shapes=scratch,
        compiler_params=_cparams("arbitrary", "arbitrary"),
        name="natten_mixer_ctx" if need_ctx else "natten_mixer",
    )(*args)
    return (res[0], res[1]) if need_ctx else (res[0], None)


def _natten_bias(rpb):
    col = np.arange(GRID_W)
    cs = np.clip(col - WIN_COLS // 2, 0, GRID_W - WIN_COLS)
    band = (col[None, :] >= cs[:, None]) & (col[None, :] < cs[:, None] + WIN_COLS)
    dc = np.clip(col[None, :] - col[:, None] + WIN_COLS - 1, 0, 2 * WIN_COLS - 2)
    pick = (np.arange(2 * WIN_COLS - 1)[:, None, None] == dc.T[None]) & band.T[None]
    tz = jnp.einsum('lhrm,mkc->lhrkc', rpb.astype(F32), jnp.asarray(pick, F32),
                    precision=lax.Precision.HIGHEST)
    tz = jnp.where(band.T[None, None, None], tz, NEG)
    depth = rpb.shape[0]
    tab = tz.reshape(depth, H_C // 2, 2, 2 * WIN_ROWS - 1, GRID_W, GRID_W)
    tab = jnp.transpose(tab, (0, 3, 1, 4, 2, 5))
    return tab.reshape(depth, 2 * WIN_ROWS - 1, H_C // 2, GRID_W, 2 * GRID_W)


def _outproj_kernel(ya_ref, yb_ref, yc_ref, x_ref, gt_ref, g_ref, wa_ref, wb_ref, wc_ref, o_ref):
    u = (_dot(ya_ref[0].astype(BF16), wa_ref[...]) + _dot(yb_ref[0].astype(BF16), wb_ref[...])
         + _dot(yc_ref[0].astype(BF16), wc_ref[...]))
    ms = jnp.mean(u * u, axis=-1, keepdims=True)
    o_ref[0] = x_ref[0] + gt_ref[0] * (u * lax.rsqrt(ms + EPS) * g_ref[...])


def _out_projection(ya, yb, yc, x, gt, g, wts, layer, tm):
    b, t, d = x.shape
    wa, wb, wc = wts
    bm = gt.shape[0]
    mod_map = (lambda i, j: (i, 0, 0)) if bm > 1 else (lambda i, j: (0, 0, 0))
    tile = lambda n: pl.BlockSpec((1, tm, n), lambda i, j: (i, j, 0))
    return pl.pallas_call(
        _outproj_kernel,
        grid=(b, t // tm),
        in_specs=[tile(ya.shape[2]), tile(yb.shape[2]), tile(yc.shape[2]), tile(d),
                  pl.BlockSpec((1, 1, d), mod_map)] + [_layer_spec(a, layer) for a in (g, wa, wb, wc)],
        out_specs=tile(d),
        out_shape=jax.ShapeDtypeStruct((b, t, d), F32),
        compiler_params=_cparams("arbitrary", "arbitrary"),
        name="out_projection",
    )(ya, yb, yc, x, gt, g, wa, wb, wc)


def _mlstm_weights(w):
    lead = w.shape[:-1]
    q = DH_B // 4
    blk = w[..., OFF_B:OFF_G].astype(BF16).reshape(lead + (5, H_B, DH_B))
    gap = jnp.zeros(lead + (2, H_B, LANES // 2 - 2 * q), BF16)
    qk = blk[..., :2, :, :]
    qk = jnp.concatenate([qk[..., 0:q], qk[..., 2 * q:3 * q], gap, qk[..., q:2 * q], qk[..., 3 * q:], gap], axis=-1)
    rest = jnp.pad(blk[..., 2:, :, :], [(0, 0)] * (len(lead) + 2) + [(0, LANES - DH_B)])
    wb = jnp.concatenate([qk, rest], axis=-3).reshape(lead + (5 * H_B * LANES,))
    g = jnp.swapaxes(w[..., OFF_G:OFF_C].astype(BF16).reshape(lead + (4, H_B)), -1, -2)
    wgt = jnp.pad(g, [(0, 0)] * (len(lead) + 1) + [(0, 4)]).reshape(lead + (H_B * 8,))
    return wb, jnp.swapaxes(wgt, -1, -2)


def _rope_tables(t):
    pos = np.arange(t)
    q = DH_B // 4
    inv = ROPE_BASE ** (-jnp.arange(0, 2 * q, 2, dtype=F32) / (2 * q))
    ang_r = jnp.asarray(pos // GRID_W, F32)[:, None] * inv[None, :]
    ang_c = jnp.asarray(pos % GRID_W, F32)[:, None] * inv[None, :]
    zeros = jnp.zeros((t, LANES // 2 - 2 * q), F32)
    cos_half = jnp.concatenate([jnp.cos(ang_r), jnp.cos(ang_c), zeros], axis=1)
    sin_half = jnp.concatenate([jnp.sin(ang_r), jnp.sin(ang_c), zeros], axis=1)
    return (jnp.concatenate([cos_half, cos_half], axis=1),
            jnp.concatenate([-sin_half, sin_half], axis=1))


def _pad_heads(v):
    v = v.reshape(v.shape[:-1] + (H_B, DH_B))
    v = jnp.pad(v, [(0, 0)] * (v.ndim - 1) + [(0, LANES - DH_B)])
    return v.reshape(v.shape[:-2] + (H_B * LANES,))


def kernel(x, c, ctx, c_ctx, w_mod, b_mod, g_pre, g_post, w_in, w_out, hgrn_lb, hgrn_gn, mlstm_gate_b, mlstm_gn, na_rpb):
    depth = w_in.shape[0]
    b, t, d = x.shape
    tc = ctx.shape[1]
    assert t % CHUNK == 0 and tc % CHUNK == 0 and t // GRID_W >= WIN_ROWS and t % GRID_W == 0
    assert w_in.shape[2] == P_IN and (1 << N_LEVELS) == CHUNK

    lb_cum = jnp.cumsum(jax.nn.softmax(hgrn_lb.astype(F32), axis=0), axis=0)
    lb_all = lb_cum - lb_cum[0]
    hgrn_tabs = _hgrn_tables()
    tris = _tri_tables()
    rope = _rope_tables(t)
    mod_rows = 16
    cc = jnp.concatenate([c.astype(F32), c_ctx.astype(F32)[None],
                          jnp.zeros((mod_rows - b - 1, d), F32)], axis=0)
    tm = min(256, t)
    tmo = min(512, t)
    tmc = min(256, tc)

    wb_all, wgt_all = _mlstm_weights(w_in)
    gbc_all = jnp.pad(jnp.swapaxes(mlstm_gate_b.astype(F32), 1, 2), ((0, 0), (0, 0), (0, 4)))
    in_wts = (w_in[:, :, :OFF_B].astype(BF16), wb_all, w_in[:, :, OFF_C:].astype(BF16), wgt_all,
              gbc_all.reshape(depth, H_B * 8, 1))
    wo_b = jnp.pad(w_out[:, W_A:W_A + W_B].reshape(depth, H_B, DH_B, d),
                   ((0, 0), (0, 0), (0, LANES - DH_B), (0, 0)))
    out_wts = (w_out[:, :W_A].astype(BF16), wo_b.reshape(depth, H_B * LANES, d).astype(BF16),
               w_out[:, W_A + W_B:].astype(BF16))
    g_pre3 = g_pre.astype(F32).reshape(depth, 1, d)
    g_post3 = g_post.astype(F32).reshape(depth, 1, d)
    na_bias = _natten_bias(na_rpb)
    gnb_all = _pad_heads(mlstm_gn.astype(F32)).reshape(depth, H_B, 1, LANES)

    for l in range(depth):
        need_ctx = l < depth - 1
        mod = _modulation(cc, w_mod[l], b_mod[l])
        sh, sc, gt = (mod[:b, i * d:(i + 1) * d].reshape(b, 1, d) for i in range(3))
        shc, scc, gtc = (mod[b:b + 1, i * d:(i + 1) * d].reshape(1, 1, d) for i in range(3))

        pa, pai, pbqk, pbv, pcb, pcz, pgt = _in_projection(x, sh, sc, g_pre3, in_wts, l, rope, tm)
        pac, paic, pbqkc, pbvc, pcbc, pczc, pgtc = _in_projection(ctx, shc, scc, g_pre3, in_wts, l, None, tmc)

        ya, yac = _hgrn_mixer(pa, pai, pac, paic, lb_all[l], hgrn_gn[l], hgrn_tabs, need_ctx)
        yb, ybc = _mlstm_mixer(pbqk, pbv, pgt, pbqkc, pbvc, pgtc, gnb_all[l], tris, need_ctx)
        yc, ycc = _natten_mixer(pcb, pcz, pcbc, pczc, na_bias, l, need_ctx)

        x = _out_projection(ya, yb, yc, x, gt, g_post3, out_wts, l, tmo)
        if need_ctx:
            ctx = _out_projection(yac, ybc, ycc, ctx, gtc, g_post3, out_wts, l, tmc)
    return x
```

```python
import functools
import math

import numpy as np
import jax
import jax.numpy as jnp
from jax import lax
from jax.experimental import pallas as pl
from jax.experimental.pallas import tpu as pltpu

F32 = jnp.float32
BF16 = jnp.bfloat16

LANES = 128
GRID_W = 64
W_A, H_A, DH_A = 256, 4, 64
W_B, H_B, DH_B = 384, 4, 96
W_C, H_C, DH_C = 384, 6, 64
WIN_ROWS, WIN_COLS = 8, 16
ROPE_BASE = 10000.0
EPS = 1e-6
NEG = -1e30
LOG2_E = math.log2(math.e)
CHUNK = 128
N_LEVELS = 7
NA_ROW_UNROLL = 8
NA_STAGE_GROUP = 4
CHUNK_UNROLL = 8
MLSTM_GROUP = 8
OFF_B = 5 * W_A
OFF_G = OFF_B + 5 * W_B
OFF_C = OFF_G + 4 * H_B
P_IN = OFF_C + 4 * W_C
ONE_LANE = DH_B
VMEM_LIMIT = 56 * 1024 * 1024


def _cparams(*sem):
    return pltpu.CompilerParams(dimension_semantics=sem, vmem_limit_bytes=VMEM_LIMIT)


def _dot(a, b):
    return jnp.dot(a, b, preferred_element_type=F32)


def _dot_nt(a, b):
    return lax.dot_general(a, b, (((1,), (1,)), ((), ())), preferred_element_type=F32)


def _dot_tn(a, b):
    return lax.dot_general(a, b, (((0,), (0,)), ((), ())), preferred_element_type=F32)


def _chunk_loop(n, body):
    u = math.gcd(n, CHUNK_UNROLL)

    def step(i, carry):
        for j in range(u):
            body(i * u + j, carry)
        return carry

    return lax.fori_loop(0, n // u, step, 0)


def _grouped_loop(n, body, group=CHUNK_UNROLL):
    u = math.gcd(n, group)

    def step(i, carry):
        body([i * u + j for j in range(u)])
        return carry

    return lax.fori_loop(0, n // u, step, 0)


def _split3(x):
    hi = x.astype(BF16)
    r = x - hi.astype(F32)
    mid = r.astype(BF16)
    lo = (r - mid.astype(F32)).astype(BF16)
    return hi, mid, lo


def _exact_left(t01, x):
    n = x.shape[1]
    r = _dot(t01, jnp.concatenate(_split3(x), axis=1))
    return r[:, :n] + r[:, n:2 * n] + r[:, 2 * n:]


def _exact_right(x, t01):
    m = x.shape[0]
    pieces = jnp.concatenate([p.astype(F32) for p in _split3(x)], axis=0).astype(BF16)
    r = _dot(pieces, t01)
    return r[:m] + r[m:2 * m] + r[2 * m:]


def _sigmoid(x):
    return 1.0 / (1.0 + jnp.exp(-x))


def _silu(x):
    return x * _sigmoid(x)


def _log_sigmoid(x):
    return jnp.minimum(x, 0.0) - jnp.log(1.0 + jnp.exp(-jnp.abs(x)))


def _mod_kernel(c_ref, w_ref, b_ref, o_ref):
    s = _silu(c_ref[...])
    o_ref[...] = _dot(s.astype(BF16), w_ref[...].astype(BF16)) + b_ref[...]


def _modulation(cc, w, b):
    rows, d = cc.shape
    n = w.shape[1]
    tn = d
    assert n % tn == 0
    return pl.pallas_call(
        _mod_kernel,
        grid=(n // tn,),
        in_specs=[pl.BlockSpec((rows, d), lambda j: (0, 0)),
                  pl.BlockSpec((d, tn), lambda j: (0, j)),
                  pl.BlockSpec((1, tn), lambda j: (0, j))],
        out_specs=pl.BlockSpec((rows, tn), lambda j: (0, j)),
        out_shape=jax.ShapeDtypeStruct((rows, n), F32),
        compiler_params=_cparams("arbitrary"),
        name="modulation",
    )(cc, w, b.reshape(1, n))


def _inproj_kernel(*refs, rotary):
    if rotary:
        (x_ref, sh_ref, sc_ref, g_ref, wa_ref, wb_ref, wc_ref, wgt_ref, gbc_ref,
         cos_ref, sin_ref, pa_ref, pai_ref, pbqk_ref, pbv_ref, pcb_ref, pcz_ref, pgt_ref) = refs
    else:
        (x_ref, sh_ref, sc_ref, g_ref, wa_ref, wb_ref, wc_ref, wgt_ref, gbc_ref,
         pa_ref, pai_ref, pbqk_ref, pbv_ref, pcb_ref, pcz_ref, pgt_ref) = refs
    x = x_ref[0]
    ms = jnp.mean(x * x, axis=-1, keepdims=True)
    h = x * lax.rsqrt(ms + EPS) * g_ref[...]
    h = h * (1.0 + sc_ref[0]) + sh_ref[0]
    hb = h.astype(BF16)
    a = _dot(hb, wa_ref[...])
    pa_ref[0, :, :3 * W_A] = a[:, :3 * W_A]
    pai_ref[0] = a[:, 3 * W_A:4 * W_A].astype(BF16)
    pa_ref[0, :, 3 * W_A:] = a[:, 4 * W_A:]
    c = _dot(hb, wc_ref[...])
    pcb_ref[0] = c[:, :3 * W_C].astype(BF16)
    pcz_ref[0] = c[:, 3 * W_C:]
    pgt_ref[0] = _dot_nt(wgt_ref[...], hb) + gbc_ref[...]
    hw = H_B * LANES
    scale = DH_B ** -0.5
    for part in range(5):
        p = _dot(hb, wb_ref[:, part * hw:(part + 1) * hw])
        if part == 0:
            p = p * scale
        if part >= 2:
            pbv_ref[0, :, (part - 2) * hw:(part - 1) * hw] = p
        elif rotary:
            cos = cos_ref[...]
            sin = sin_ref[...]
            for hd in range(H_B):
                ph = p[:, hd * LANES:(hd + 1) * LANES]
                pbqk_ref[0, :, part * hw + hd * LANES:part * hw + (hd + 1) * LANES] = (
                    ph * cos + pltpu.roll(ph, LANES // 2, 1) * sin).astype(BF16)
        else:
            pbqk_ref[0, :, part * hw:(part + 1) * hw] = p.astype(BF16)


def _layer_spec(a, layer):
    return pl.BlockSpec((None,) + a.shape[1:], lambda i, j: (layer, 0, 0))


def _in_projection(x, sh, sc, g, wts, layer, rope, tm):
    b, t, d = x.shape
    wa, wb, wc, wgt, gbc = wts
    bm = sh.shape[0]
    mod_map = (lambda i, j: (i, 0, 0)) if bm > 1 else (lambda i, j: (0, 0, 0))
    in_specs = [pl.BlockSpec((1, tm, d), lambda i, j: (i, j, 0)),
                pl.BlockSpec((1, 1, d), mod_map),
                pl.BlockSpec((1, 1, d), mod_map)] + [_layer_spec(a, layer) for a in (g, wa, wb, wc, wgt, gbc)]
    args = [x, sh, sc, g, wa, wb, wc, wgt, gbc]
    if rope is not None:
        in_specs += [pl.BlockSpec((tm, LANES), lambda i, j: (j, 0))] * 2
        args += list(rope)
    ng = wgt.shape[1]
    hw = H_B * LANES
    widths = [(4 * W_A, F32), (W_A, BF16), (2 * hw, BF16), (3 * hw, F32), (3 * W_C, BF16), (W_C, F32)]
    out_specs = ([pl.BlockSpec((1, tm, n), lambda i, j: (i, j, 0)) for n, _ in widths]
                 + [pl.BlockSpec((1, ng, tm), lambda i, j: (i, 0, j))])
    out_shape = ([jax.ShapeDtypeStruct((b, t, n), dt) for n, dt in widths]
                 + [jax.ShapeDtypeStruct((b, ng, t), F32)])
    return pl.pallas_call(
        functools.partial(_inproj_kernel, rotary=rope is not None),
        grid=(b, t // tm),
        in_specs=in_specs,
        out_specs=out_specs,
        out_shape=out_shape,
        compiler_params=_cparams("arbitrary", "arbitrary"),
        name="in_projection_rope" if rope is not None else "in_projection",
    )(*args)


def _hgrn_tables():
    L, nl = CHUNK, N_LEVELS
    idx = np.arange(L)
    t = idx[:, None]
    u = idx[None, :]
    tabq, levels = [], []
    for rev in (False, True):
        tabq.append(((u <= t) if not rev else (u >= t)).astype(np.float32))
        x = t ^ u
        lvl = np.where(x > 0, np.floor(np.log2(np.maximum(x, 1))), float(nl))
        valid = (u <= t) if not rev else (u >= t)
        lvl = np.where(valid, lvl, -1.0).astype(np.float32)
        levels.append(np.concatenate([lvl, lvl], axis=1))
    return jnp.asarray(np.stack(tabq), dtype=BF16), jnp.asarray(np.stack(levels), dtype=BF16)


def _tri_tables():
    L = CHUNK
    idx = np.arange(L)
    t = idx[:, None]
    u = idx[None, :]
    fwd = (u <= t).astype(np.float32)
    bwd = (u >= t).astype(np.float32)
    spread = np.zeros((16, 2 * L), np.float32)
    spread[0:3, :L] = 1.0
    spread[3:6, L:] = 1.0
    return (jnp.asarray(np.concatenate([fwd.T, bwd.T], axis=1), dtype=BF16), jnp.asarray(spread, dtype=BF16))


def _hgrn_kernel(*refs, need_ctx, n_lat, n_ctx):
    (q_ref, ff_ref, fb_ref, i_ref, z_ref, qc_ref, ffc_ref, fbc_ref, ic_ref, zc_ref,
     lb_ref, gn_ref, tabq_ref, lvl_ref) = refs[:14]
    if need_ctx:
        y_ref, yc_ref, u_ref, dec_ref, gate_ref = refs[14:]
    else:
        y_ref, u_ref, dec_ref, gate_ref = refs[14:]
        yc_ref = None
    L, nl = CHUNK, N_LEVELS
    lane = lax.broadcasted_iota(jnp.int32, (1, LANES), 1)
    hm0 = lane < DH_A
    row_h = lax.broadcasted_iota(jnp.int32, (LANES, LANES), 0) // DH_A
    col_h = lax.broadcasted_iota(jnp.int32, (LANES, LANES), 1) // DH_A
    same_head = row_h == col_h
    gn = gn_ref[0]
    segs = ((0, n_ctx, qc_ref, (ffc_ref, fbc_ref), ic_ref, zc_ref, yc_ref),
            (n_ctx, n_lat, q_ref, (ff_ref, fb_ref), i_ref, z_ref, y_ref))

    def decay(fl, lbd):
        a0 = jnp.log(lbd)
        b0 = jnp.log(1.0 - lbd) + _log_sigmoid(fl)
        logf = jnp.maximum(a0, b0) + jnp.log(1.0 + jnp.exp(-jnp.abs(a0 - b0)))
        return logf * LOG2_E, (1.0 - lbd) * _sigmoid(-fl)

    def rows_of(c):
        return pl.ds(pl.multiple_of(c * L, L), L)

    tok = lax.broadcasted_iota(jnp.int32, (L, 1), 0)

    def level_exponents(logf, bcum, d):
        prev, nxt = pltpu.roll(logf, 1, 0), pltpu.roll(logf, L - 1, 0)
        if d == 0:
            eqs = [logf, logf + jnp.where((tok % 4) == 3, prev, 0.0)]
            eks = [None, jnp.where((tok % 4) == 0, nxt, 0.0)]
        else:
            eqs = [logf, logf + jnp.where((tok % 4) == 0, nxt, 0.0)]
            eks = [None, jnp.where((tok % 4) == 3, prev, 0.0)]
        for lv in range(2, nl):
            h = 1 << lv
            r = h - 1 if d == 0 else h
            g = jnp.concatenate([jnp.broadcast_to(bcum[blk + r:blk + r + 1], (2 * h, LANES))
                                 for blk in range(0, L, 2 * h)], axis=0)
            eqs.append(bcum - g)
            eks.append(g - bcum)
        return eqs, eks

    def chunk_total(bcum, d):
        return bcum[L - 1:L] if d == 0 else bcum[0:1]

    for base, n, _, f_refs, ai_ref, _, _ in segs:
        def increments(cs, base=base, f_refs=f_refs, ai_ref=ai_ref):
            jd = [(j, d) for j in range(len(cs)) for d in range(2)]
            lk = {(j, d): decay(f_refs[d][0, rows_of(cs[j]), :], lb_ref[d, 0]) for j, d in jd}
            bcum = {(j, d): _exact_left(tabq_ref[d], lk[(j, d)][0]) for j, d in jd}
            b_tot = {(j, d): chunk_total(bcum[(j, d)], d) for j, d in jd}
            vb = [ai_ref[0, rows_of(c), :].astype(BF16) for c in cs]
            kr = {x: (lk[x][1] * jnp.exp2(b_tot[x] - bcum[x])).astype(BF16) for x in jd}
            u = {(j, d): _dot_tn(kr[(j, d)], vb[j]) for j, d in jd}
            dec = {x: jnp.broadcast_to(jnp.exp2(b_tot[x]), (LANES, LANES)).T for x in jd}
            for j, d in jd:
                u_ref[d, base + cs[j]] = jnp.where(same_head, u[(j, d)], 0.0)
                dec_ref[d, base + cs[j]] = dec[(j, d)]
                tok_rows = rows_of(base + cs[j])
                gate_ref[d, 0, tok_rows, :] = lk[(j, d)][0]
                gate_ref[d, 1, tok_rows, :] = lk[(j, d)][1]
                gate_ref[d, 2, tok_rows, :] = bcum[(j, d)]
        _grouped_loop(n, increments)

    def advance(d, idx, s):
        u = u_ref[d, idx]
        u_ref[d, idx] = s
        return s * dec_ref[d, idx] + u

    states = (jnp.zeros((LANES, LANES), F32),) * 2
    for base, n, *_ in segs:
        def scan_step(i, states, base=base, n=n):
            return (advance(0, base + i, states[0]), advance(1, base + n - 1 - i, states[1]))
        states = lax.fori_loop(0, n, scan_step, states)

    def finish(o, z):
        sq = o * o
        ms0 = jnp.sum(jnp.where(hm0, sq, 0.0), axis=-1, keepdims=True)
        ms1 = jnp.sum(jnp.where(hm0, 0.0, sq), axis=-1, keepdims=True)
        ms = jnp.where(hm0, ms0, ms1) * (1.0 / DH_A)
        return o * lax.rsqrt(ms + EPS) * gn * _silu(z)

    for base, n, aq_ref, f_refs, ai_ref, zz_ref, out_ref in segs:
        if out_ref is None:
            continue

        def outputs(cs, base=base, aq_ref=aq_ref, f_refs=f_refs, ai_ref=ai_ref, zz_ref=zz_ref,
                    out_ref=out_ref):
            js = range(len(cs))
            jd = [(j, d) for j in js for d in range(2)]
            q = [_silu(aq_ref[0, rows_of(c), :]) * (DH_A ** -0.5) for c in cs]
            lk = {(j, d): (gate_ref[d, 0, rows_of(base + cs[j]), :], gate_ref[d, 1, rows_of(base + cs[j]), :])
                  for j, d in jd}
            bcum = {(j, d): gate_ref[d, 2, rows_of(base + cs[j]), :] for j, d in jd}
            ex = {(j, d): level_exponents(lk[(j, d)][0], bcum[(j, d)], d) for j, d in jd}
            a = {x: jnp.zeros((L, 2 * L), BF16) for x in jd}
            zero = jnp.zeros((DH_A, L), BF16)
            for lv in range(nl + 1):
                at_level = [lvl_ref[d] == jnp.asarray(lv, BF16) for d in range(2)]
                for j, d in jd:
                    k = lk[(j, d)][1]
                    eqs, eks = ex[(j, d)]
                    if lv < nl:
                        qt = q[j] * jnp.exp2(eqs[lv])
                        kl = k if eks[lv] is None else k * jnp.exp2(eks[lv])
                    else:
                        qt, kl = q[j], k
                    ktl = kl.T.astype(BF16)
                    w = jnp.concatenate([jnp.concatenate([ktl[:DH_A], zero], axis=0),
                                         jnp.concatenate([zero, ktl[DH_A:]], axis=0)], axis=1)
                    a[(j, d)] = jnp.where(at_level[d], _dot(qt.astype(BF16), w).astype(BF16), a[(j, d)])
            ai = [ai_ref[0, rows_of(c), :] for c in cs]
            vs = [jnp.concatenate([jnp.where(hm0, x, 0.0), jnp.where(hm0, 0.0, x)], axis=0).astype(BF16)
                  for x in ai]
            intra = {(j, d): _dot(a[(j, d)], vs[j]) for j, d in jd}
            inter = {(j, d): _dot((q[j] * jnp.exp2(bcum[(j, d)])).astype(BF16),
                                  u_ref[d, base + cs[j]].astype(BF16)) for j, d in jd}
            for j, c in enumerate(cs):
                o = intra[(j, 0)] + inter[(j, 0)] + intra[(j, 1)] + inter[(j, 1)]
                out_ref[0, rows_of(c), :] = finish(o, zz_ref[0, rows_of(c), :]).astype(out_ref.dtype)
        _grouped_loop(n, outputs)


def _hgrn_mixer(pa, pai, pac, paic, lb, gn, tabs, need_ctx):
    b, t, _ = pa.shape
    tc = pac.shape[1]
    npair = W_A // LANES
    n_chunks = (t + tc) // CHUNK
    lat = lambda part: pl.BlockSpec((1, t, LANES), lambda i, p, part=part: (i, 0, part * npair + p))
    ctx = lambda part: pl.BlockSpec((1, tc, LANES), lambda i, p, part=part: (i, 0, part * npair + p))
    whole = lambda a: pl.BlockSpec(a.shape, lambda i, p: (0, 0, 0))
    in_specs = ([lat(0), lat(1), lat(2), lat(0), lat(3), ctx(0), ctx(1), ctx(2), ctx(0), ctx(3),
                 pl.BlockSpec((2, 1, 1, LANES), lambda i, p: (0, p, 0, 0)),
                 pl.BlockSpec((1, 1, LANES), lambda i, p: (p, 0, 0))] + [whole(a) for a in tabs])
    out_specs = [pl.BlockSpec((1, t, LANES), lambda i, p: (i, 0, p))]
    out_shape = [jax.ShapeDtypeStruct((b, t, W_A), BF16)]
    scratch = [pltpu.VMEM((2, n_chunks, LANES, LANES), F32)] * 2
    scratch.append(pltpu.VMEM((2, 3, t + tc, LANES), F32))
    if need_ctx:
        out_specs.append(pl.BlockSpec((1, tc, LANES), lambda i, p: (i, 0, p)))
        out_shape.append(jax.ShapeDtypeStruct((b, tc, W_A), BF16))
    res = pl.pallas_call(
        functools.partial(_hgrn_kernel, need_ctx=need_ctx, n_lat=t // CHUNK, n_ctx=tc // CHUNK),
        grid=(b, npair),
        in_specs=in_specs,
        out_specs=out_specs,
        out_shape=out_shape,
        scratch_shapes=scratch,
        compiler_params=_cparams("arbitrary", "arbitrary"),
        name="hgrn2_mixer_ctx" if need_ctx else "hgrn2_mixer",
    )(pa, pa, pa, pai, pa, pac, pac, pac, paic, pac,
      lb.reshape(2, npair, 1, LANES), gn.reshape(npair, 1, LANES), *tabs)
    return (res[0], res[1]) if need_ctx else (res[0], None)


def _mlstm_kernel(*refs, need_ctx, n_lat, n_ctx):
    (q_ref, k_ref, v_ref, o_ref, z_ref, qc_ref, kc_ref, vc_ref, oc_ref, zc_ref,
     gt_ref, gtc_ref, gn_ref, trit_ref, ones_ref) = refs[:15]
    if need_ctx:
        y_ref, yc_ref, c_ref, st_ref, vt_ref = refs[15:]
    else:
        y_ref, c_ref, st_ref, vt_ref = refs[15:]
        yc_ref = None
    L = CHUNK
    sub = lax.broadcasted_iota(jnp.int32, (LANES, 1), 0)
    rows_i = lax.broadcasted_iota(jnp.int32, (L, L), 0)
    cols_i = lax.broadcasted_iota(jnp.int32, (L, L), 1)
    gn = gn_ref[0]
    segs = ((0, n_ctx, qc_ref, kc_ref, vc_ref, oc_ref, zc_ref, gtc_ref, yc_ref),
            (n_ctx, n_lat, q_ref, k_ref, v_ref, o_ref, z_ref, gt_ref, y_ref))
    B_TOT, M_LOC, M_PREV = 0, 1, 2

    def rows_of(c):
        return pl.ds(pl.multiple_of(c * L, L), L)

    loop = _chunk_loop

    def row_gates(grow):
        er = _exact_right(_log_sigmoid(grow), trit_ref[...])
        out = []
        for d in range(2):
            b_row = er[2 + d:3 + d, d * L:(d + 1) * L]
            out.append((grow[d:d + 1, :], b_row, b_row[:, L - 1:L] if d == 0 else b_row[:, 0:1]))
        return out

    for base, n, _, _, vv, *_ in segs:
        def transpose_values(cs, base=base, vv=vv):
            tiles = [jnp.where(sub == ONE_LANE, 1.0, vv[0, rows_of(c), :].T) for c in cs]
            for c, x in zip(cs, tiles):
                vt_ref[:, rows_of(base + c)] = x
        _grouped_loop(n, transpose_values, MLSTM_GROUP)

    for base, n, _, kk, _, _, _, ggt, _ in segs:
        def local(cs, base=base, kk=kk, ggt=ggt):
            jd = [(j, d) for j in range(len(cs)) for d in range(2)]
            gates = [row_gates(ggt[0, :, rows_of(c)]) for c in cs]
            w = {(j, d): gates[j][d][2] + gates[j][d][0] - gates[j][d][1] for j, d in jd}
            m_loc = {x: jnp.max(w[x], axis=1, keepdims=True) for x in jd}
            ew = {x: jnp.exp(w[x] - m_loc[x]) for x in jd}
            kb = [kk[0, rows_of(c), :].astype(BF16) for c in cs]
            vat = [vt_ref[:, rows_of(base + c)] for c in cs]
            lhs = {(j, d): (vat[j] * ew[(j, d)]).astype(BF16) for j, d in jd}
            out = {(j, d): _dot(lhs[(j, d)], kb[j]) for j, d in jd}
            for j, d in jd:
                c_ref[d, base + cs[j]] = out[(j, d)]
                st_ref[d, base + cs[j], B_TOT:B_TOT + 1, :] = jnp.broadcast_to(gates[j][d][2], (1, LANES))
                st_ref[d, base + cs[j], M_LOC:M_LOC + 1, :] = jnp.broadcast_to(m_loc[(j, d)], (1, LANES))
        _grouped_loop(n, local, MLSTM_GROUP)

    def advance(d, idx, state):
        c_prev, m_prev = state
        c_loc = c_ref[d, idx]
        b_tot = st_ref[d, idx, B_TOT:B_TOT + 1, :]
        m_loc = st_ref[d, idx, M_LOC:M_LOC + 1, :]
        c_ref[d, idx] = c_prev
        st_ref[d, idx, M_PREV:M_PREV + 1, :] = m_prev
        m_new = jnp.maximum(b_tot + m_prev, m_loc)
        return (jnp.exp(b_tot + m_prev - m_new) * c_prev + jnp.exp(m_loc - m_new) * c_loc, m_new)

    s0 = (jnp.zeros((LANES, LANES), F32), jnp.zeros((1, LANES), F32))
    states = (s0, s0)
    for base, n, *_ in segs:
        def scan_step(i, states, base=base, n=n):
            return (advance(0, base + i, states[0]), advance(1, base + n - 1 - i, states[1]))
        states = lax.fori_loop(0, n, scan_step, states)

    for base, n, qq, kk, _, oo, zz, ggt, out_ref in segs:
        if out_ref is None:
            continue

        def outputs(cs, base=base, qq=qq, kk=kk, oo=oo, zz=zz, ggt=ggt, out_ref=out_ref):
            js = range(len(cs))
            jd = [(j, d) for j in js for d in range(2)]
            qb = [qq[0, rows_of(c), :].astype(BF16) for c in cs]
            sq = [_dot_nt(jnp.concatenate([kk[0, rows_of(c), :].astype(BF16),
                                           c_ref[0, base + c].astype(BF16),
                                           c_ref[1, base + c].astype(BF16)], axis=0), qb[j])
                  for j, c in enumerate(cs)]
            gates = [row_gates(ggt[0, :, rows_of(c)]) for c in cs]
            pieces = [jnp.concatenate(
                [p.astype(F32) for d in range(2) for p in _split3(gates[j][d][0] - gates[j][d][1])]
                + [jnp.zeros((10, L), F32)], axis=0).astype(BF16) for j in js]
            r_bc = [_dot_tn(x, ones_ref[...]) for x in pieces]
            valid = [rows_i <= cols_i, rows_i >= cols_i]
            dm = {(j, d): jnp.where(valid[d], gates[j][d][1] + r_bc[j][:, d * L:(d + 1) * L], NEG)
                  for j, d in jd}
            inter = {(j, d): gates[j][d][1] + st_ref[d, base + cs[j], M_PREV:M_PREV + 1, 0:1] for j, d in jd}
            m_t = {x: jnp.maximum(jnp.max(dm[x], axis=0, keepdims=True), inter[x]) for x in jd}
            p = {(j, d): sq[j][:L] * jnp.exp(dm[(j, d)] - m_t[(j, d)]) for j, d in jd}
            e_in = {x: jnp.exp(inter[x] - m_t[x]) for x in jd}
            vat = [vt_ref[:, rows_of(base + c)].astype(BF16) for c in cs]
            pv = {(j, d): _dot(vat[j], p[(j, d)].astype(BF16)) for j, d in jd}
            hts = []
            for j in js:
                ht = jnp.zeros((LANES, L), F32)
                for d in range(2):
                    x = (j, d)
                    qct = sq[j][(1 + d) * L:(2 + d) * L]
                    num = pv[x] + e_in[x] * qct
                    den = jnp.sum(p[x], axis=0, keepdims=True) + e_in[x] * qct[ONE_LANE:ONE_LANE + 1, :]
                    ht = ht + num / jnp.maximum(jnp.abs(den), jnp.exp(-m_t[x]))
                hts.append(jnp.where(sub < DH_B, ht, 0.0))
            ms = [jnp.sum(ht * ht, axis=0, keepdims=True) * (1.0 / DH_B) for ht in hts]
            h = [(ht * lax.rsqrt(m + EPS)).T for ht, m in zip(hts, ms)]
            for j, c in enumerate(cs):
                rows = rows_of(c)
                y = _sigmoid(oo[0, rows, :]) * (h[j] * gn) * _silu(zz[0, rows, :])
                out_ref[0, rows, :] = y.astype(out_ref.dtype)
        _grouped_loop(n, outputs, MLSTM_GROUP)


def _mlstm_mixer(pbqk, pbv, pgt, pbqkc, pbvc, pgtc, gn, tris, need_ctx):
    b, t, _ = pbv.shape
    tc = pbvc.shape[1]
    trit, spread = tris
    lat = lambda part: pl.BlockSpec((1, t, LANES), lambda i, h, part=part: (i, 0, part * H_B + h))
    ctx = lambda part: pl.BlockSpec((1, tc, LANES), lambda i, h, part=part: (i, 0, part * H_B + h))
    in_specs = ([lat(0), lat(1), lat(0), lat(1), lat(2), ctx(0), ctx(1), ctx(0), ctx(1), ctx(2)] +
                [pl.BlockSpec((1, 8, t), lambda i, h: (i, h, 0)),
                 pl.BlockSpec((1, 8, tc), lambda i, h: (i, h, 0)),
                 pl.BlockSpec((1, 1, LANES), lambda i, h: (h, 0, 0)),
                 pl.BlockSpec(trit.shape, lambda i, h: (0, 0)),
                 pl.BlockSpec(spread.shape, lambda i, h: (0, 0))])
    out_specs = [pl.BlockSpec((1, t, LANES), lambda i, h: (i, 0, h))]
    out_shape = [jax.ShapeDtypeStruct((b, t, H_B * LANES), BF16)]
    n_chunks = (t + tc) // CHUNK
    scratch = [pltpu.VMEM((2, n_chunks, LANES, LANES), F32),
               pltpu.VMEM((2, n_chunks, 8, LANES), F32),
               pltpu.VMEM((LANES, t + tc), F32)]
    if need_ctx:
        out_specs.append(pl.BlockSpec((1, tc, LANES), lambda i, h: (i, 0, h)))
        out_shape.append(jax.ShapeDtypeStruct((b, tc, H_B * LANES), BF16))
    res = pl.pallas_call(
        functools.partial(_mlstm_kernel, need_ctx=need_ctx, n_lat=t // CHUNK, n_ctx=tc // CHUNK),
        grid=(b, H_B),
        in_specs=in_specs,
        out_specs=out_specs,
        out_shape=out_shape,
        scratch_shapes=scratch,
        compiler_params=_cparams("arbitrary", "arbitrary"),
        name="mlstm_mixer_ctx" if need_ctx else "mlstm_mixer",
    )(pbqk, pbqk, pbv, pbv, pbv, pbqkc, pbqkc, pbvc, pbvc, pbvc, pgt, pgtc, gn, trit, spread)
    return (res[0], res[1]) if need_ctx else (res[0], None)


def _natten_kernel(*refs, need_ctx, n_rows):
    if need_ctx:
        (q_ref, k_ref, v_ref, z_ref, kc_ref, vc_ref, bias_ref, qc_ref, zc_ref, y_ref, yc_ref,
         vt_ref, vct_ref) = refs
    else:
        (q_ref, k_ref, v_ref, z_ref, kc_ref, vc_ref, bias_ref, y_ref, vt_ref, vct_ref) = refs
    lane = lax.broadcasted_iota(jnp.int32, (1, LANES), 1)
    hm0 = lane < DH_C
    scale = DH_C ** -0.5
    win = WIN_ROWS * GRID_W
    t = k_ref.shape[1]
    tc = kc_ref.shape[1]
    blk = LANES

    def transposed(tile):
        return tile.astype(F32).T.astype(BF16)

    nb = t // blk
    group = math.gcd(nb, NA_STAGE_GROUP)
    row_group_size = math.gcd(n_rows, NA_ROW_UNROLL)

    def blocks(i):
        return [pl.ds(pl.multiple_of((i * group + j) * blk, blk), blk) for j in range(group)]

    def stage(i, carry):
        tiles = [transposed(v_ref[0, rows, :]) for rows in blocks(i)]
        for rows, x in zip(blocks(i), tiles):
            vt_ref[0, :, rows] = x
        return carry

    lax.fori_loop(0, nb // group, stage, 0)

    def shifted(start):
        src = start + GRID_W
        if not isinstance(src, int):
            src = pl.multiple_of(src, GRID_W)
        return transposed(v_ref[0, pl.ds(src, blk), :])

    def stage_shifted(i, carry):
        tiles = [shifted(rows.start) for rows in blocks(i)]
        for rows, x in zip(blocks(i), tiles):
            vt_ref[1, :, rows] = x
        return carry

    lax.fori_loop(0, nb // group - 1, stage_shifted, 0)
    tail = [(nb - group + j) * blk for j in range(group - 1)]
    tiles = [shifted(start) for start in tail]
    for start, x in zip(tail, tiles):
        vt_ref[1, :, start:start + blk] = x
    last = v_ref[0, t - blk:t, :].astype(F32).T
    vt_ref[1, :, t - blk:t] = pltpu.roll(last, GRID_W, 1).astype(BF16)

    tiles = [transposed(vc_ref[0, i * blk:(i + 1) * blk, :]) for i in range(tc // blk)]
    for i, x in enumerate(tiles):
        vct_ref[:, i * blk:(i + 1) * blk] = x

    def attend(qs, keys, vals_t, biases):
        n = qs[0].shape[0]
        q2 = [jnp.concatenate([jnp.where(hm0, q, 0.0), jnp.where(hm0, 0.0, q)], axis=0).astype(BF16)
              for q in qs]
        lt = [_dot_nt(k, q) for k, q in zip(keys, q2)]
        lt = [x if b is None else jnp.concatenate([x[:b.shape[0]] + b, x[b.shape[0]:]], axis=0)
              for x, b in zip(lt, biases)]
        p = [jnp.exp(x - jnp.max(x, axis=0, keepdims=True)) for x in lt]
        den = [jnp.sum(x, axis=0, keepdims=True) for x in p]
        ot = [_dot(v, x.astype(BF16)) for v, x in zip(vals_t, p)]
        o = [(x / d).T for x, d in zip(ot, den)]
        return [jnp.where(hm0, x[:n], x[n:]) for x in o]

    def row_group(i, carry):
        rows = [i * row_group_size + j for j in range(row_group_size)]
        rs = [jnp.clip(r - WIN_ROWS // 2, 0, n_rows - WIN_ROWS) for r in rows]
        qrows = [pl.ds(pl.multiple_of(r * GRID_W, GRID_W), GRID_W) for r in rows]
        keys, vals_t, biases = [], [], []
        for r, s in zip(rows, rs):
            par = s % 2
            wrows = pl.ds(pl.multiple_of(s * GRID_W, GRID_W), win)
            wlanes = pl.ds(pl.multiple_of((s - par) * GRID_W, LANES), win)
            keys.append(jnp.concatenate([k_ref[0, wrows, :], kc_ref[0]], axis=0))
            vals_t.append(jnp.concatenate([vt_ref[par, :, wlanes], vct_ref[...]], axis=1))
            dr0 = s - r + (WIN_ROWS - 1)
            biases.append(jnp.concatenate([bias_ref[dr0 + j, 0] for j in range(WIN_ROWS)], axis=0))
        outs = attend([q_ref[0, qr, :].astype(F32) * scale for qr in qrows], keys, vals_t, biases)
        for qr, o in zip(qrows, outs):
            y_ref[0, qr, :] = (o * _silu(z_ref[0, qr, :])).astype(y_ref.dtype)
        return carry

    lax.fori_loop(0, n_rows // row_group_size, row_group, 0)

    if need_ctx:
        o, = attend([qc_ref[0].astype(F32) * scale], [kc_ref[0]], [vct_ref[...]], [None])
        yc_ref[0] = (o * _silu(zc_ref[0])).astype(yc_ref.dtype)


def _natten_mixer(pcb, pcz, pcbc, pczc, bias, layer, need_ctx):
    b, t, _ = pcb.shape
    tc = pcbc.shape[1]
    npair = W_C // LANES
    n_rows = t // GRID_W
    lat = lambda part: pl.BlockSpec((1, t, LANES), lambda i, p, part=part: (i, 0, part * npair + p))
    ctx = lambda part: pl.BlockSpec((1, tc, LANES), lambda i, p, part=part: (i, 0, part * npair + p))
    in_specs = [lat(0), lat(1), lat(2), lat(0), ctx(1), ctx(2),
                pl.BlockSpec((None, 2 * WIN_ROWS - 1, 1, GRID_W, LANES), lambda i, p: (layer, 0, p, 0, 0))]
    args = [pcb, pcb, pcb, pcz, pcbc, pcbc, bias]
    out_specs = [pl.BlockSpec((1, t, LANES), lambda i, p: (i, 0, p))]
    out_shape = [jax.ShapeDtypeStruct((b, t, W_C), BF16)]
    if need_ctx:
        in_specs += [ctx(0), ctx(0)]
        args += [pcbc, pczc]
        out_specs.append(pl.BlockSpec((1, tc, LANES), lambda i, p: (i, 0, p)))
        out_shape.append(jax.ShapeDtypeStruct((b, tc, W_C), BF16))
    scratch = [pltpu.VMEM((2, LANES, t), BF16), pltpu.VMEM((LANES, tc), BF16)]
    res = pl.pallas_call(
        functools.partial(_natten_kernel, need_ctx=need_ctx, n_rows=n_rows),
        grid=(b, npair),
        in_specs=in_specs,
        out_specs=out_specs,
        out_shape=out_shape,
        scratch_shapes=scratch,
        compiler_params=_cparams("arbitrary", "arbitrary"),
        name="natten_mixer_ctx" if need_ctx else "natten_mixer",
    )(*args)
    return (res[0], res[1]) if need_ctx else (res[0], None)


def _natten_bias(rpb):
    col = np.arange(GRID_W)
    cs = np.clip(col - WIN_COLS // 2, 0, GRID_W - WIN_COLS)
    band = (col[None, :] >= cs[:, None]) & (col[None, :] < cs[:, None] + WIN_COLS)
    dc = np.clip(col[None, :] - col[:, None] + WIN_COLS - 1, 0, 2 * WIN_COLS - 2)
    pick = (np.arange(2 * WIN_COLS - 1)[:, None, None] == dc.T[None]) & band.T[None]
    tz = jnp.einsum('lhrm,mkc->lhrkc', rpb.astype(F32), jnp.asarray(pick, F32),
                    precision=lax.Precision.HIGHEST)
    tz = jnp.where(band.T[None, None, None], tz, NEG)
    depth = rpb.shape[0]
    tab = tz.reshape(depth, H_C // 2, 2, 2 * WIN_ROWS - 1, GRID_W, GRID_W)
    tab = jnp.transpose(tab, (0, 3, 1, 4, 2, 5))
    return tab.reshape(depth, 2 * WIN_ROWS - 1, H_C // 2, GRID_W, 2 * GRID_W)


def _outproj_kernel(ya_ref, yb_ref, yc_ref, x_ref, gt_ref, g_ref, wa_ref, wb_ref, wc_ref, o_ref):
    u = (_dot(ya_ref[0].astype(BF16), wa_ref[...]) + _dot(yb_ref[0].astype(BF16), wb_ref[...])
         + _dot(yc_ref[0].astype(BF16), wc_ref[...]))
    ms = jnp.mean(u * u, axis=-1, keepdims=True)
    o_ref[0] = x_ref[0] + gt_ref[0] * (u * lax.rsqrt(ms + EPS) * g_ref[...])


def _out_projection(ya, yb, yc, x, gt, g, wts, layer, tm):
    b, t, d = x.shape
    wa, wb, wc = wts
    bm = gt.shape[0]
    mod_map = (lambda i, j: (i, 0, 0)) if bm > 1 else (lambda i, j: (0, 0, 0))
    tile = lambda n: pl.BlockSpec((1, tm, n), lambda i, j: (i, j, 0))
    return pl.pallas_call(
        _outproj_kernel,
        grid=(b, t // tm),
        in_specs=[tile(ya.shape[2]), tile(yb.shape[2]), tile(yc.shape[2]), tile(d),
                  pl.BlockSpec((1, 1, d), mod_map)] + [_layer_spec(a, layer) for a in (g, wa, wb, wc)],
        out_specs=tile(d),
        out_shape=jax.ShapeDtypeStruct((b, t, d), F32),
        compiler_params=_cparams("arbitrary", "arbitrary"),
        name="out_projection",
    )(ya, yb, yc, x, gt, g, wa, wb, wc)


def _mlstm_weights(w):
    lead = w.shape[:-1]
    q = DH_B // 4
    blk = w[..., OFF_B:OFF_G].astype(BF16).reshape(lead + (5, H_B, DH_B))
    gap = jnp.zeros(lead + (2, H_B, LANES // 2 - 2 * q), BF16)
    qk = blk[..., :2, :, :]
    qk = jnp.concatenate([qk[..., 0:q], qk[..., 2 * q:3 * q], gap, qk[..., q:2 * q], qk[..., 3 * q:], gap], axis=-1)
    rest = jnp.pad(blk[..., 2:, :, :], [(0, 0)] * (len(lead) + 2) + [(0, LANES - DH_B)])
    wb = jnp.concatenate([qk, rest], axis=-3).reshape(lead + (5 * H_B * LANES,))
    g = jnp.swapaxes(w[..., OFF_G:OFF_C].astype(BF16).reshape(lead + (4, H_B)), -1, -2)
    wgt = jnp.pad(g, [(0, 0)] * (len(lead) + 1) + [(0, 4)]).reshape(lead + (H_B * 8,))
    return wb, jnp.swapaxes(wgt, -1, -2)


def _rope_tables(t):
    pos = np.arange(t)
    q = DH_B // 4
    inv = ROPE_BASE ** (-jnp.arange(0, 2 * q, 2, dtype=F32) / (2 * q))
    ang_r = jnp.asarray(pos // GRID_W, F32)[:, None] * inv[None, :]
    ang_c = jnp.asarray(pos % GRID_W, F32)[:, None] * inv[None, :]
    zeros = jnp.zeros((t, LANES // 2 - 2 * q), F32)
    cos_half = jnp.concatenate([jnp.cos(ang_r), jnp.cos(ang_c), zeros], axis=1)
    sin_half = jnp.concatenate([jnp.sin(ang_r), jnp.sin(ang_c), zeros], axis=1)
    return (jnp.concatenate([cos_half, cos_half], axis=1),
            jnp.concatenate([-sin_half, sin_half], axis=1))


def _pad_heads(v):
    v = v.reshape(v.shape[:-1] + (H_B, DH_B))
    v = jnp.pad(v, [(0, 0)] * (v.ndim - 1) + [(0, LANES - DH_B)])
    return v.reshape(v.shape[:-2] + (H_B * LANES,))


def kernel(x, c, ctx, c_ctx, w_mod, b_mod, g_pre, g_post, w_in, w_out, hgrn_lb, hgrn_gn, mlstm_gate_b, mlstm_gn, na_rpb):
    depth = w_in.shape[0]
    b, t, d = x.shape
    tc = ctx.shape[1]
    assert t % CHUNK == 0 and tc % CHUNK == 0 and t // GRID_W >= WIN_ROWS and t % GRID_W == 0
    assert w_in.shape[2] == P_IN and (1 << N_LEVELS) == CHUNK

    lb_cum = jnp.cumsum(jax.nn.softmax(hgrn_lb.astype(F32), axis=0), axis=0)
    lb_all = lb_cum - lb_cum[0]
    hgrn_tabs = _hgrn_tables()
    tris = _tri_tables()
    rope = _rope_tables(t)
    mod_rows = 16
    cc = jnp.concatenate([c.astype(F32), c_ctx.astype(F32)[None],
                          jnp.zeros((mod_rows - b - 1, d), F32)], axis=0)
    tm = min(256, t)
    tmo = min(512, t)
    tmc = min(256, tc)

    wb_all, wgt_all = _mlstm_weights(w_in)
    gbc_all = jnp.pad(jnp.swapaxes(mlstm_gate_b.astype(F32), 1, 2), ((0, 0), (0, 0), (0, 4)))
    in_wts = (w_in[:, :, :OFF_B].astype(BF16), wb_all, w_in[:, :, OFF_C:].astype(BF16), wgt_all,
              gbc_all.reshape(depth, H_B * 8, 1))
    wo_b = jnp.pad(w_out[:, W_A:W_A + W_B].reshape(depth, H_B, DH_B, d),
                   ((0, 0), (0, 0), (0, LANES - DH_B), (0, 0)))
    out_wts = (w_out[:, :W_A].astype(BF16), wo_b.reshape(depth, H_B * LANES, d).astype(BF16),
               w_out[:, W_A + W_B:].astype(BF16))
    g_pre3 = g_pre.astype(F32).reshape(depth, 1, d)
    g_post3 = g_post.astype(F32).reshape(depth, 1, d)
    na_bias = _natten_bias(na_rpb)
    gnb_all = _pad_heads(mlstm_gn.astype(F32)).reshape(depth, H_B, 1, LANES)

    for l in range(depth):
        need_ctx = l < depth - 1
        mod = _modulation(cc, w_mod[l], b_mod[l])
        sh, sc, gt = (mod[:b, i * d:(i + 1) * d].reshape(b, 1, d) for i in range(3))
        shc, scc, gtc = (mod[b:b + 1, i * d:(i + 1) * d].reshape(1, 1, d) for i in range(3))

        pa, pai, pbqk, pbv, pcb, pcz, pgt = _in_projection(x, sh, sc, g_pre3, in_wts, l, rope, tm)
        pac, paic, pbqkc, pbvc, pcbc, pczc, pgtc = _in_projection(ctx, shc, scc, g_pre3, in_wts, l, None, tmc)

        ya, yac = _hgrn_mixer(pa, pai, pac, paic, lb_all[l], hgrn_gn[l], hgrn_tabs, need_ctx)
        yb, ybc = _mlstm_mixer(pbqk, pbv, pgt, pbqkc, pbvc, pgtc, gnb_all[l], tris, need_ctx)
        yc, ycc = _natten_mixer(pcb, pcz, pcbc, pczc, na_bias, l, need_ctx)

        x = _out_projection(ya, yb, yc, x, gt, g_post3, out_wts, l, tmo)
        if need_ctx:
            ctx = _out_projection(yac, ybc, ycc, ctx, gtc, g_post3, out_wts, l, tmc)
    return x
```

```python
import functools
import math

import numpy as np
import jax
import jax.numpy as jnp
from jax import lax
from jax.experimental import pallas as pl
from jax.experimental.pallas import tpu as pltpu

F32 = jnp.float32
BF16 = jnp.bfloat16

LANES = 128
GRID_W = 64
W_A, H_A, DH_A = 256, 4, 64
W_B, H_B, DH_B = 384, 4, 96
W_C, H_C, DH_C = 384, 6, 64
WIN_ROWS, WIN_COLS = 8, 16
ROPE_BASE = 10000.0
EPS = 1e-6
NEG = -1e30
LOG2_E = math.log2(math.e)
CHUNK = 128
N_LEVELS = 7
NA_ROW_UNROLL = 32
NA_STAGE_GROUP = 8
CHUNK_UNROLL = 8
MLSTM_GROUP = 16
OFF_B = 5 * W_A
OFF_G = OFF_B + 5 * W_B
OFF_C = OFF_G + 4 * H_B
P_IN = OFF_C + 4 * W_C
ONE_LANE = DH_B
VMEM_LIMIT = 56 * 1024 * 1024


def _cparams(*sem):
    return pltpu.CompilerParams(dimension_semantics=sem, vmem_limit_bytes=VMEM_LIMIT)


def _dot(a, b):
    return jnp.dot(a, b, preferred_element_type=F32)


def _dot_nt(a, b):
    return lax.dot_general(a, b, (((1,), (1,)), ((), ())), preferred_element_type=F32)


def _dot_tn(a, b):
    return lax.dot_general(a, b, (((0,), (0,)), ((), ())), preferred_element_type=F32)


def _chunk_loop(n, body):
    u = math.gcd(n, CHUNK_UNROLL)

    def step(i, carry):
        for j in range(u):
            body(i * u + j, carry)
        return carry

    return lax.fori_loop(0, n // u, step, 0)


def _grouped_loop(n, body, group=CHUNK_UNROLL):
    u = math.gcd(n, group)

    def step(i, carry):
        body([i * u + j for j in range(u)])
        return carry

    return lax.fori_loop(0, n // u, step, 0)


def _split3(x):
    hi = x.astype(BF16)
    r = x - hi.astype(F32)
    mid = r.astype(BF16)
    lo = (r - mid.astype(F32)).astype(BF16)
    return hi, mid, lo


def _exact_left(t01, x):
    n = x.shape[1]
    r = _dot(t01, jnp.concatenate(_split3(x), axis=1))
    return r[:, :n] + r[:, n:2 * n] + r[:, 2 * n:]


def _exact_right(x, t01):
    m = x.shape[0]
    pieces = jnp.concatenate([p.astype(F32) for p in _split3(x)], axis=0).astype(BF16)
    r = _dot(pieces, t01)
    return r[:m] + r[m:2 * m] + r[2 * m:]


def _sigmoid(x):
    return 1.0 / (1.0 + jnp.exp(-x))


def _silu(x):
    return x * _sigmoid(x)


def _log_sigmoid(x):
    return jnp.minimum(x, 0.0) - jnp.log(1.0 + jnp.exp(-jnp.abs(x)))


def _mod_kernel(c_ref, w_ref, b_ref, o_ref):
    s = _silu(c_ref[...])
    o_ref[...] = _dot(s.astype(BF16), w_ref[...].astype(BF16)) + b_ref[...]


def _modulation(cc, w, b):
    rows, d = cc.shape
    n = w.shape[1]
    tn = d
    assert n % tn == 0
    return pl.pallas_call(
        _mod_kernel,
        grid=(n // tn,),
        in_specs=[pl.BlockSpec((rows, d), lambda j: (0, 0)),
                  pl.BlockSpec((d, tn), lambda j: (0, j)),
                  pl.BlockSpec((1, tn), lambda j: (0, j))],
        out_specs=pl.BlockSpec((rows, tn), lambda j: (0, j)),
        out_shape=jax.ShapeDtypeStruct((rows, n), F32),
        compiler_params=_cparams("arbitrary"),
        name="modulation",
    )(cc, w, b.reshape(1, n))


def _inproj_kernel(*refs, rotary):
    if rotary:
        (x_ref, sh_ref, sc_ref, g_ref, wa_ref, wb_ref, wc_ref, wgt_ref, gbc_ref,
         cos_ref, sin_ref, pa_ref, pai_ref, pbqk_ref, pbv_ref, pcb_ref, pcz_ref, pgt_ref) = refs
    else:
        (x_ref, sh_ref, sc_ref, g_ref, wa_ref, wb_ref, wc_ref, wgt_ref, gbc_ref,
         pa_ref, pai_ref, pbqk_ref, pbv_ref, pcb_ref, pcz_ref, pgt_ref) = refs
    x = x_ref[0]
    ms = jnp.mean(x * x, axis=-1, keepdims=True)
    h = x * lax.rsqrt(ms + EPS) * g_ref[...]
    h = h * (1.0 + sc_ref[0]) + sh_ref[0]
    hb = h.astype(BF16)
    a = _dot(hb, wa_ref[...])
    pa_ref[0, :, :3 * W_A] = a[:, :3 * W_A]
    pai_ref[0] = a[:, 3 * W_A:4 * W_A].astype(BF16)
    pa_ref[0, :, 3 * W_A:] = a[:, 4 * W_A:]
    c = _dot(hb, wc_ref[...])
    pcb_ref[0] = c[:, :3 * W_C].astype(BF16)
    pcz_ref[0] = c[:, 3 * W_C:]
    pgt_ref[0] = _dot_nt(wgt_ref[...], hb) + gbc_ref[...]
    hw = H_B * LANES
    scale = DH_B ** -0.5
    for part in range(5):
        p = _dot(hb, wb_ref[:, part * hw:(part + 1) * hw])
        if part == 0:
            p = p * scale
        if part >= 2:
            pbv_ref[0, :, (part - 2) * hw:(part - 1) * hw] = p
        elif rotary:
            cos = cos_ref[...]
            sin = sin_ref[...]
            for hd in range(H_B):
                ph = p[:, hd * LANES:(hd + 1) * LANES]
                pbqk_ref[0, :, part * hw + hd * LANES:part * hw + (hd + 1) * LANES] = (
                    ph * cos + pltpu.roll(ph, LANES // 2, 1) * sin).astype(BF16)
        else:
            pbqk_ref[0, :, part * hw:(part + 1) * hw] = p.astype(BF16)


def _layer_spec(a, layer):
    return pl.BlockSpec((None,) + a.shape[1:], lambda i, j: (layer, 0, 0))


def _in_projection(x, sh, sc, g, wts, layer, rope, tm):
    b, t, d = x.shape
    wa, wb, wc, wgt, gbc = wts
    bm = sh.shape[0]
    mod_map = (lambda i, j: (i, 0, 0)) if bm > 1 else (lambda i, j: (0, 0, 0))
    in_specs = [pl.BlockSpec((1, tm, d), lambda i, j: (i, j, 0)),
                pl.BlockSpec((1, 1, d), mod_map),
                pl.BlockSpec((1, 1, d), mod_map)] + [_layer_spec(a, layer) for a in (g, wa, wb, wc, wgt, gbc)]
    args = [x, sh, sc, g, wa, wb, wc, wgt, gbc]
    if rope is not None:
        in_specs += [pl.BlockSpec((tm, LANES), lambda i, j: (j, 0))] * 2
        args += list(rope)
    ng = wgt.shape[1]
    hw = H_B * LANES
    widths = [(4 * W_A, F32), (W_A, BF16), (2 * hw, BF16), (3 * hw, F32), (3 * W_C, BF16), (W_C, F32)]
    out_specs = ([pl.BlockSpec((1, tm, n), lambda i, j: (i, j, 0)) for n, _ in widths]
                 + [pl.BlockSpec((1, ng, tm), lambda i, j: (i, 0, j))])
    out_shape = ([jax.ShapeDtypeStruct((b, t, n), dt) for n, dt in widths]
                 + [jax.ShapeDtypeStruct((b, ng, t), F32)])
    return pl.pallas_call(
        functools.partial(_inproj_kernel, rotary=rope is not None),
        grid=(b, t // tm),
        in_specs=in_specs,
        out_specs=out_specs,
        out_shape=out_shape,
        compiler_params=_cparams("arbitrary", "arbitrary"),
        name="in_projection_rope" if rope is not None else "in_projection",
    )(*args)


def _hgrn_tables():
    L, nl = CHUNK, N_LEVELS
    idx = np.arange(L)
    t = idx[:, None]
    u = idx[None, :]
    tabq, levels = [], []
    for rev in (False, True):
        tabq.append(((u <= t) if not rev else (u >= t)).astype(np.float32))
        x = t ^ u
        lvl = np.where(x > 0, np.floor(np.log2(np.maximum(x, 1))), float(nl))
        valid = (u <= t) if not rev else (u >= t)
        lvl = np.where(valid, lvl, -1.0).astype(np.float32)
        levels.append(np.concatenate([lvl, lvl], axis=1))
    return jnp.asarray(np.stack(tabq), dtype=BF16), jnp.asarray(np.stack(levels), dtype=BF16)


def _tri_tables():
    L = CHUNK
    idx = np.arange(L)
    t = idx[:, None]
    u = idx[None, :]
    fwd = (u <= t).astype(np.float32)
    bwd = (u >= t).astype(np.float32)
    spread = np.zeros((16, 2 * L), np.float32)
    spread[0:3, :L] = 1.0
    spread[3:6, L:] = 1.0
    return (jnp.asarray(np.concatenate([fwd.T, bwd.T], axis=1), dtype=BF16), jnp.asarray(spread, dtype=BF16))


def _hgrn_kernel(*refs, need_ctx, n_lat, n_ctx):
    (q_ref, ff_ref, fb_ref, i_ref, z_ref, qc_ref, ffc_ref, fbc_ref, ic_ref, zc_ref,
     lb_ref, gn_ref, tabq_ref, lvl_ref) = refs[:14]
    if need_ctx:
        y_ref, yc_ref, u_ref, dec_ref, gate_ref = refs[14:]
    else:
        y_ref, u_ref, dec_ref, gate_ref = refs[14:]
        yc_ref = None
    L, nl = CHUNK, N_LEVELS
    lane = lax.broadcasted_iota(jnp.int32, (1, LANES), 1)
    hm0 = lane < DH_A
    row_h = lax.broadcasted_iota(jnp.int32, (LANES, LANES), 0) // DH_A
    col_h = lax.broadcasted_iota(jnp.int32, (LANES, LANES), 1) // DH_A
    same_head = row_h == col_h
    gn = gn_ref[0]
    segs = ((0, n_ctx, qc_ref, (ffc_ref, fbc_ref), ic_ref, zc_ref, yc_ref),
            (n_ctx, n_lat, q_ref, (ff_ref, fb_ref), i_ref, z_ref, y_ref))

    def decay(fl, lbd):
        a0 = jnp.log(lbd)
        b0 = jnp.log(1.0 - lbd) + _log_sigmoid(fl)
        logf = jnp.maximum(a0, b0) + jnp.log(1.0 + jnp.exp(-jnp.abs(a0 - b0)))
        return logf * LOG2_E, (1.0 - lbd) * _sigmoid(-fl)

    def rows_of(c):
        return pl.ds(pl.multiple_of(c * L, L), L)

    tok = lax.broadcasted_iota(jnp.int32, (L, 1), 0)

    def level_exponents(logf, bcum, d):
        prev, nxt = pltpu.roll(logf, 1, 0), pltpu.roll(logf, L - 1, 0)
        if d == 0:
            eqs = [logf, logf + jnp.where((tok % 4) == 3, prev, 0.0)]
            eks = [None, jnp.where((tok % 4) == 0, nxt, 0.0)]
        else:
            eqs = [logf, logf + jnp.where((tok % 4) == 0, nxt, 0.0)]
            eks = [None, jnp.where((tok % 4) == 3, prev, 0.0)]
        for lv in range(2, nl):
            h = 1 << lv
            r = h - 1 if d == 0 else h
            g = jnp.concatenate([jnp.broadcast_to(bcum[blk + r:blk + r + 1], (2 * h, LANES))
                                 for blk in range(0, L, 2 * h)], axis=0)
            eqs.append(bcum - g)
            eks.append(g - bcum)
        return eqs, eks

    def chunk_total(bcum, d):
        return bcum[L - 1:L] if d == 0 else bcum[0:1]

    for base, n, _, f_refs, ai_ref, _, _ in segs:
        def increments(cs, base=base, f_refs=f_refs, ai_ref=ai_ref):
            jd = [(j, d) for j in range(len(cs)) for d in range(2)]
            lk = {(j, d): decay(f_refs[d][0, rows_of(cs[j]), :], lb_ref[d, 0]) for j, d in jd}
            bcum = {(j, d): _exact_left(tabq_ref[d], lk[(j, d)][0]) for j, d in jd}
            b_tot = {(j, d): chunk_total(bcum[(j, d)], d) for j, d in jd}
            vb = [ai_ref[0, rows_of(c), :].astype(BF16) for c in cs]
            kr = {x: (lk[x][1] * jnp.exp2(b_tot[x] - bcum[x])).astype(BF16) for x in jd}
            u = {(j, d): _dot_tn(kr[(j, d)], vb[j]) for j, d in jd}
            dec = {x: jnp.broadcast_to(jnp.exp2(b_tot[x]), (LANES, LANES)).T for x in jd}
            for j, d in jd:
                u_ref[d, base + cs[j]] = jnp.where(same_head, u[(j, d)], 0.0)
                dec_ref[d, base + cs[j]] = dec[(j, d)]
                tok_rows = rows_of(base + cs[j])
                gate_ref[d, 0, tok_rows, :] = lk[(j, d)][0]
                gate_ref[d, 1, tok_rows, :] = lk[(j, d)][1]
                gate_ref[d, 2, tok_rows, :] = bcum[(j, d)]
        _grouped_loop(n, increments)

    def advance(d, idx, s):
        u = u_ref[d, idx]
        u_ref[d, idx] = s
        return s * dec_ref[d, idx] + u

    states = (jnp.zeros((LANES, LANES), F32),) * 2
    for base, n, *_ in segs:
        def scan_step(i, states, base=base, n=n):
            return (advance(0, base + i, states[0]), advance(1, base + n - 1 - i, states[1]))
        states = lax.fori_loop(0, n, scan_step, states)

    def finish(o, z):
        sq = o * o
        ms0 = jnp.sum(jnp.where(hm0, sq, 0.0), axis=-1, keepdims=True)
        ms1 = jnp.sum(jnp.where(hm0, 0.0, sq), axis=-1, keepdims=True)
        ms = jnp.where(hm0, ms0, ms1) * (1.0 / DH_A)
        return o * lax.rsqrt(ms + EPS) * gn * _silu(z)

    for base, n, aq_ref, f_refs, ai_ref, zz_ref, out_ref in segs:
        if out_ref is None:
            continue

        def outputs(cs, base=base, aq_ref=aq_ref, f_refs=f_refs, ai_ref=ai_ref, zz_ref=zz_ref,
                    out_ref=out_ref):
            js = range(len(cs))
            jd = [(j, d) for j in js for d in range(2)]
            q = [_silu(aq_ref[0, rows_of(c), :]) * (DH_A ** -0.5) for c in cs]
            lk = {(j, d): (gate_ref[d, 0, rows_of(base + cs[j]), :], gate_ref[d, 1, rows_of(base + cs[j]), :])
                  for j, d in jd}
            bcum = {(j, d): gate_ref[d, 2, rows_of(base + cs[j]), :] for j, d in jd}
            ex = {(j, d): level_exponents(lk[(j, d)][0], bcum[(j, d)], d) for j, d in jd}
            a = {x: jnp.zeros((L, 2 * L), BF16) for x in jd}
            zero = jnp.zeros((DH_A, L), BF16)
            for lv in range(nl + 1):
                at_level = [lvl_ref[d] == jnp.asarray(lv, BF16) for d in range(2)]
                for j, d in jd:
                    k = lk[(j, d)][1]
                    eqs, eks = ex[(j, d)]
                    if lv < nl:
                        qt = q[j] * jnp.exp2(eqs[lv])
                        kl = k if eks[lv] is None else k * jnp.exp2(eks[lv])
                    else:
                        qt, kl = q[j], k
                    ktl = kl.T.astype(BF16)
                    w = jnp.concatenate([jnp.concatenate([ktl[:DH_A], zero], axis=0),
                                         jnp.concatenate([zero, ktl[DH_A:]], axis=0)], axis=1)
                    a[(j, d)] = jnp.where(at_level[d], _dot(qt.astype(BF16), w).astype(BF16), a[(j, d)])
            ai = [ai_ref[0, rows_of(c), :] for c in cs]
            vs = [jnp.concatenate([jnp.where(hm0, x, 0.0), jnp.where(hm0, 0.0, x)], axis=0).astype(BF16)
                  for x in ai]
            intra = {(j, d): _dot(a[(j, d)], vs[j]) for j, d in jd}
            inter = {(j, d): _dot((q[j] * jnp.exp2(bcum[(j, d)])).astype(BF16),
                                  u_ref[d, base + cs[j]].astype(BF16)) for j, d in jd}
            for j, c in enumerate(cs):
                o = intra[(j, 0)] + inter[(j, 0)] + intra[(j, 1)] + inter[(j, 1)]
                out_ref[0, rows_of(c), :] = finish(o, zz_ref[0, rows_of(c), :]).astype(out_ref.dtype)
        _grouped_loop(n, outputs)


def _hgrn_mixer(pa, pai, pac, paic, lb, gn, tabs, need_ctx):
    b, t, _ = pa.shape
    tc = pac.shape[1]
    npair = W_A // LANES
    n_chunks = (t + tc) // CHUNK
    lat = lambda part: pl.BlockSpec((1, t, LANES), lambda i, p, part=part: (i, 0, part * npair + p))
    ctx = lambda part: pl.BlockSpec((1, tc, LANES), lambda i, p, part=part: (i, 0, part * npair + p))
    whole = lambda a: pl.BlockSpec(a.shape, lambda i, p: (0, 0, 0))
    in_specs = ([lat(0), lat(1), lat(2), lat(0), lat(3), ctx(0), ctx(1), ctx(2), ctx(0), ctx(3),
                 pl.BlockSpec((2, 1, 1, LANES), lambda i, p: (0, p, 0, 0)),
                 pl.BlockSpec((1, 1, LANES), lambda i, p: (p, 0, 0))] + [whole(a) for a in tabs])
    out_specs = [pl.BlockSpec((1, t, LANES), lambda i, p: (i, 0, p))]
    out_shape = [jax.ShapeDtypeStruct((b, t, W_A), BF16)]
    scratch = [pltpu.VMEM((2, n_chunks, LANES, LANES), F32)] * 2
    scratch.append(pltpu.VMEM((2, 3, t + tc, LANES), F32))
    if need_ctx:
        out_specs.append(pl.BlockSpec((1, tc, LANES), lambda i, p: (i, 0, p)))
        out_shape.append(jax.ShapeDtypeStruct((b, tc, W_A), BF16))
    res = pl.pallas_call(
        functools.partial(_hgrn_kernel, need_ctx=need_ctx, n_lat=t // CHUNK, n_ctx=tc // CHUNK),
        grid=(b, npair),
        in_specs=in_specs,
        out_specs=out_specs,
        out_shape=out_shape,
        scratch_shapes=scratch,
        compiler_params=_cparams("arbitrary", "arbitrary"),
        name="hgrn2_mixer_ctx" if need_ctx else "hgrn2_mixer",
    )(pa, pa, pa, pai, pa, pac, pac, pac, paic, pac,
      lb.reshape(2, npair, 1, LANES), gn.reshape(npair, 1, LANES), *tabs)
    return (res[0], res[1]) if need_ctx else (res[0], None)


def _mlstm_kernel(*refs, need_ctx, n_lat, n_ctx):
    (q_ref, k_ref, v_ref, o_ref, z_ref, qc_ref, kc_ref, vc_ref, oc_ref, zc_ref,
     gt_ref, gtc_ref, gn_ref, trit_ref, ones_ref) = refs[:15]
    if need_ctx:
        y_ref, yc_ref, c_ref, st_ref, vt_ref = refs[15:]
    else:
        y_ref, c_ref, st_ref, vt_ref = refs[15:]
        yc_ref = None
    L = CHUNK
    sub = lax.broadcasted_iota(jnp.int32, (LANES, 1), 0)
    rows_i = lax.broadcasted_iota(jnp.int32, (L, L), 0)
    cols_i = lax.broadcasted_iota(jnp.int32, (L, L), 1)
    gn = gn_ref[0]
    segs = ((0, n_ctx, qc_ref, kc_ref, vc_ref, oc_ref, zc_ref, gtc_ref, yc_ref),
            (n_ctx, n_lat, q_ref, k_ref, v_ref, o_ref, z_ref, gt_ref, y_ref))
    B_TOT, M_LOC, M_PREV = 0, 1, 2

    def rows_of(c):
        return pl.ds(pl.multiple_of(c * L, L), L)

    loop = _chunk_loop

    def row_gates(grow):
        er = _exact_right(_log_sigmoid(grow), trit_ref[...])
        out = []
        for d in range(2):
            b_row = er[2 + d:3 + d, d * L:(d + 1) * L]
            out.append((grow[d:d + 1, :], b_row, b_row[:, L - 1:L] if d == 0 else b_row[:, 0:1]))
        return out

    for base, n, _, _, vv, *_ in segs:
        def transpose_values(cs, base=base, vv=vv):
            tiles = [jnp.where(sub == ONE_LANE, 1.0, vv[0, rows_of(c), :].T) for c in cs]
            for c, x in zip(cs, tiles):
                vt_ref[:, rows_of(base + c)] = x
        _grouped_loop(n, transpose_values, MLSTM_GROUP)

    for base, n, _, kk, _, _, _, ggt, _ in segs:
        def local(cs, base=base, kk=kk, ggt=ggt):
            jd = [(j, d) for j in range(len(cs)) for d in range(2)]
            gates = [row_gates(ggt[0, :, rows_of(c)]) for c in cs]
            w = {(j, d): gates[j][d][2] + gates[j][d][0] - gates[j][d][1] for j, d in jd}
            m_loc = {x: jnp.max(w[x], axis=1, keepdims=True) for x in jd}
            ew = {x: jnp.exp(w[x] - m_loc[x]) for x in jd}
            kb = [kk[0, rows_of(c), :].astype(BF16) for c in cs]
            vat = [vt_ref[:, rows_of(base + c)] for c in cs]
            lhs = {(j, d): (vat[j] * ew[(j, d)]).astype(BF16) for j, d in jd}
            out = {(j, d): _dot(lhs[(j, d)], kb[j]) for j, d in jd}
            for j, d in jd:
                c_ref[d, base + cs[j]] = out[(j, d)]
                st_ref[d, base + cs[j], B_TOT:B_TOT + 1, :] = jnp.broadcast_to(gates[j][d][2], (1, LANES))
                st_ref[d, base + cs[j], M_LOC:M_LOC + 1, :] = jnp.broadcast_to(m_loc[(j, d)], (1, LANES))
        _grouped_loop(n, local, MLSTM_GROUP)

    def advance(d, idx, state):
        c_prev, m_prev = state
        c_loc = c_ref[d, idx]
        b_tot = st_ref[d, idx, B_TOT:B_TOT + 1, :]
        m_loc = st_ref[d, idx, M_LOC:M_LOC + 1, :]
        c_ref[d, idx] = c_prev
        st_ref[d, idx, M_PREV:M_PREV + 1, :] = m_prev
        m_new = jnp.maximum(b_tot + m_prev, m_loc)
        return (jnp.exp(b_tot + m_prev - m_new) * c_prev + jnp.exp(m_loc - m_new) * c_loc, m_new)

    s0 = (jnp.zeros((LANES, LANES), F32), jnp.zeros((1, LANES), F32))
    states = (s0, s0)
    for base, n, *_ in segs:
        def scan_step(i, states, base=base, n=n):
            return (advance(0, base + i, states[0]), advance(1, base + n - 1 - i, states[1]))
        states = lax.fori_loop(0, n, scan_step, states)

    for base, n, qq, kk, _, oo, zz, ggt, out_ref in segs:
        if out_ref is None:
            continue

        def outputs(cs, base=base, qq=qq, kk=kk, oo=oo, zz=zz, ggt=ggt, out_ref=out_ref):
            js = range(len(cs))
            jd = [(j, d) for j in js for d in range(2)]
            qb = [qq[0, rows_of(c), :].astype(BF16) for c in cs]
            sq = [_dot_nt(jnp.concatenate([kk[0, rows_of(c), :].astype(BF16),
                                           c_ref[0, base + c].astype(BF16),
                                           c_ref[1, base + c].astype(BF16)], axis=0), qb[j])
                  for j, c in enumerate(cs)]
            gates = [row_gates(ggt[0, :, rows_of(c)]) for c in cs]
            pieces = [jnp.concatenate(
                [p.astype(F32) for d in range(2) for p in _split3(gates[j][d][0] - gates[j][d][1])]
                + [jnp.zeros((10, L), F32)], axis=0).astype(BF16) for j in js]
            r_bc = [_dot_tn(x, ones_ref[...]) for x in pieces]
            valid = [rows_i <= cols_i, rows_i >= cols_i]
            dm = {(j, d): jnp.where(valid[d], gates[j][d][1] + r_bc[j][:, d * L:(d + 1) * L], NEG)
                  for j, d in jd}
            inter = {(j, d): gates[j][d][1] + st_ref[d, base + cs[j], M_PREV:M_PREV + 1, 0:1] for j, d in jd}
            m_t = {x: jnp.maximum(jnp.max(dm[x], axis=0, keepdims=True), inter[x]) for x in jd}
            p = {(j, d): sq[j][:L] * jnp.exp(dm[(j, d)] - m_t[(j, d)]) for j, d in jd}
            e_in = {x: jnp.exp(inter[x] - m_t[x]) for x in jd}
            vat = [vt_ref[:, rows_of(base + c)].astype(BF16) for c in cs]
            pv = {(j, d): _dot(vat[j], p[(j, d)].astype(BF16)) for j, d in jd}
            hts = []
            for j in js:
                ht = jnp.zeros((LANES, L), F32)
                for d in range(2):
                    x = (j, d)
                    qct = sq[j][(1 + d) * L:(2 + d) * L]
                    num = pv[x] + e_in[x] * qct
                    den = jnp.sum(p[x], axis=0, keepdims=True) + e_in[x] * qct[ONE_LANE:ONE_LANE + 1, :]
                    ht = ht + num / jnp.maximum(jnp.abs(den), jnp.exp(-m_t[x]))
                hts.append(jnp.where(sub < DH_B, ht, 0.0))
            ms = [jnp.sum(ht * ht, axis=0, keepdims=True) * (1.0 / DH_B) for ht in hts]
            h = [(ht * lax.rsqrt(m + EPS)).T for ht, m in zip(hts, ms)]
            for j, c in enumerate(cs):
                rows = rows_of(c)
                y = _sigmoid(oo[0, rows, :]) * (h[j] * gn) * _silu(zz[0, rows, :])
                out_ref[0, rows, :] = y.astype(out_ref.dtype)
        _grouped_loop(n, outputs, MLSTM_GROUP)


def _mlstm_mixer(pbqk, pbv, pgt, pbqkc, pbvc, pgtc, gn, tris, need_ctx):
    b, t, _ = pbv.shape
    tc = pbvc.shape[1]
    trit, spread = tris
    lat = lambda part: pl.BlockSpec((1, t, LANES), lambda i, h, part=part: (i, 0, part * H_B + h))
    ctx = lambda part: pl.BlockSpec((1, tc, LANES), lambda i, h, part=part: (i, 0, part * H_B + h))
    in_specs = ([lat(0), lat(1), lat(0), lat(1), lat(2), ctx(0), ctx(1), ctx(0), ctx(1), ctx(2)] +
                [pl.BlockSpec((1, 8, t), lambda i, h: (i, h, 0)),
                 pl.BlockSpec((1, 8, tc), lambda i, h: (i, h, 0)),
                 pl.BlockSpec((1, 1, LANES), lambda i, h: (h, 0, 0)),
                 pl.BlockSpec(trit.shape, lambda i, h: (0, 0)),
                 pl.BlockSpec(spread.shape, lambda i, h: (0, 0))])
    out_specs = [pl.BlockSpec((1, t, LANES), lambda i, h: (i, 0, h))]
    out_shape = [jax.ShapeDtypeStruct((b, t, H_B * LANES), BF16)]
    n_chunks = (t + tc) // CHUNK
    scratch = [pltpu.VMEM((2, n_chunks, LANES, LANES), F32),
               pltpu.VMEM((2, n_chunks, 8, LANES), F32),
               pltpu.VMEM((LANES, t + tc), F32)]
    if need_ctx:
        out_specs.append(pl.BlockSpec((1, tc, LANES), lambda i, h: (i, 0, h)))
        out_shape.append(jax.ShapeDtypeStruct((b, tc, H_B * LANES), BF16))
    res = pl.pallas_call(
        functools.partial(_mlstm_kernel, need_ctx=need_ctx, n_lat=t // CHUNK, n_ctx=tc // CHUNK),
        grid=(b, H_B),
        in_specs=in_specs,
        out_specs=out_specs,
        out_shape=out_shape,
        scratch_shapes=scratch,
        compiler_params=_cparams("arbitrary", "arbitrary"),
        name="mlstm_mixer_ctx" if need_ctx else "mlstm_mixer",
    )(pbqk, pbqk, pbv, pbv, pbv, pbqkc, pbqkc, pbvc, pbvc, pbvc, pgt, pgtc, gn, trit, spread)
    return (res[0], res[1]) if need_ctx else (res[0], None)


def _natten_kernel(*refs, need_ctx, n_rows):
    if need_ctx:
        (q_ref, k_ref, v_ref, z_ref, kc_ref, vc_ref, bias_ref, qc_ref, zc_ref, y_ref, yc_ref,
         vt_ref, vct_ref) = refs
    else:
        (q_ref, k_ref, v_ref, z_ref, kc_ref, vc_ref, bias_ref, y_ref, vt_ref, vct_ref) = refs
    lane = lax.broadcasted_iota(jnp.int32, (1, LANES), 1)
    hm0 = lane < DH_C
    scale = DH_C ** -0.5
    win = WIN_ROWS * GRID_W
    t = k_ref.shape[1]
    tc = kc_ref.shape[1]
    blk = LANES

    def transposed(tile):
        return tile.astype(F32).T.astype(BF16)

    nb = t // blk
    group = math.gcd(nb, NA_STAGE_GROUP)
    row_group_size = math.gcd(n_rows, NA_ROW_UNROLL)

    def blocks(i):
        return [pl.ds(pl.multiple_of((i * group + j) * blk, blk), blk) for j in range(group)]

    def stage(i, carry):
        tiles = [transposed(v_ref[0, rows, :]) for rows in blocks(i)]
        for rows, x in zip(blocks(i), tiles):
            vt_ref[0, :, rows] = x
        return carry

    lax.fori_loop(0, nb // group, stage, 0)

    def shifted(start):
        src = start + GRID_W
        if not isinstance(src, int):
            src = pl.multiple_of(src, GRID_W)
        return transposed(v_ref[0, pl.ds(src, blk), :])

    def stage_shifted(i, carry):
        tiles = [shifted(rows.start) for rows in blocks(i)]
        for rows, x in zip(blocks(i), tiles):
            vt_ref[1, :, rows] = x
        return carry

    lax.fori_loop(0, nb // group - 1, stage_shifted, 0)
    tail = [(nb - group + j) * blk for j in range(group - 1)]
    tiles = [shifted(start) for start in tail]
    for start, x in zip(tail, tiles):
        vt_ref[1, :, start:start + blk] = x
    last = v_ref[0, t - blk:t, :].astype(F32).T
    vt_ref[1, :, t - blk:t] = pltpu.roll(last, GRID_W, 1).astype(BF16)

    tiles = [transposed(vc_ref[0, i * blk:(i + 1) * blk, :]) for i in range(tc // blk)]
    for i, x in enumerate(tiles):
        vct_ref[:, i * blk:(i + 1) * blk] = x

    def attend(qs, keys, vals_t, biases):
        n = qs[0].shape[0]
        q2 = [jnp.concatenate([jnp.where(hm0, q, 0.0), jnp.where(hm0, 0.0, q)], axis=0).astype(BF16)
              for q in qs]
        lt = [_dot_nt(k, q) for k, q in zip(keys, q2)]
        lt = [x if b is None else jnp.concatenate([x[:b.shape[0]] + b, x[b.shape[0]:]], axis=0)
              for x, b in zip(lt, biases)]
        p = [jnp.exp(x - jnp.max(x, axis=0, keepdims=True)) for x in lt]
        den = [jnp.sum(x, axis=0, keepdims=True) for x in p]
        ot = [_dot(v, x.astype(BF16)) for v, x in zip(vals_t, p)]
        o = [(x / d).T for x, d in zip(ot, den)]
        return [jnp.where(hm0, x[:n], x[n:]) for x in o]

    def row_group(i, carry):
        rows = [i * row_group_size + j for j in range(row_group_size)]
        rs = [jnp.clip(r - WIN_ROWS // 2, 0, n_rows - WIN_ROWS) for r in rows]
        qrows = [pl.ds(pl.multiple_of(r * GRID_W, GRID_W), GRID_W) for r in rows]
        keys, vals_t, biases = [], [], []
        for r, s in zip(rows, rs):
            par = s % 2
            wrows = pl.ds(pl.multiple_of(s * GRID_W, GRID_W), win)
            wlanes = pl.ds(pl.multiple_of((s - par) * GRID_W, LANES), win)
            keys.append(jnp.concatenate([k_ref[0, wrows, :], kc_ref[0]], axis=0))
            vals_t.append(jnp.concatenate([vt_ref[par, :, wlanes], vct_ref[...]], axis=1))
            dr0 = s - r + (WIN_ROWS - 1)
            biases.append(jnp.concatenate([bias_ref[dr0 + j, 0] for j in range(WIN_ROWS)], axis=0))
        outs = attend([q_ref[0, qr, :].astype(F32) * scale for qr in qrows], keys, vals_t, biases)
        for qr, o in zip(qrows, outs):
            y_ref[0, qr, :] = (o * _silu(z_ref[0, qr, :])).astype(y_ref.dtype)
        return carry

    lax.fori_loop(0, n_rows // row_group_size, row_group, 0)

    if need_ctx:
        o, = attend([qc_ref[0].astype(F32) * scale], [kc_ref[0]], [vct_ref[...]], [None])
        yc_ref[0] = (o * _silu(zc_ref[0])).astype(yc_ref.dtype)


def _natten_mixer(pcb, pcz, pcbc, pczc, bias, layer, need_ctx):
    b, t, _ = pcb.shape
    tc = pcbc.shape[1]
    npair = W_C // LANES
    n_rows = t // GRID_W
    lat = lambda part: pl.BlockSpec((1, t, LANES), lambda i, p, part=part: (i, 0, part * npair + p))
    ctx = lambda part: pl.BlockSpec((1, tc, LANES), lambda i, p, part=part: (i, 0, part * npair + p))
    in_specs = [lat(0), lat(1), lat(2), lat(0), ctx(1), ctx(2),
                pl.BlockSpec((None, 2 * WIN_ROWS - 1, 1, GRID_W, LANES), lambda i, p: (layer, 0, p, 0, 0))]
    args = [pcb, pcb, pcb, pcz, pcbc, pcbc, bias]
    out_specs = [pl.BlockSpec((1, t, LANES), lambda i, p: (i, 0, p))]
    out_shape = [jax.ShapeDtypeStruct((b, t, W_C), BF16)]
    if need_ctx:
        in_specs += [ctx(0), ctx(0)]
        args += [pcbc, pczc]
        out_specs.append(pl.BlockSpec((1, tc, LANES), lambda i, p: (i, 0, p)))
        out_shape.append(jax.ShapeDtypeStruct((b, tc, W_C), BF16))
    scratch = [pltpu.VMEM((2, LANES, t), BF16), pltpu.VMEM((LANES, tc), BF16)]
    res = pl.pallas_call(
        functools.partial(_natten_kernel, need_ctx=need_ctx, n_rows=n_rows),
        grid=(b, npair),
        in_specs=in_specs,
        out_specs=out_specs,
        out_shape=out_shape,
        scratch_shapes=scratch,
        compiler_params=_cparams("arbitrary", "arbitrary"),
        name="natten_mixer_ctx" if need_ctx else "natten_mixer",
    )(*args)
    return (res[0], res[1]) if need_ctx else (res[0], None)


def _natten_bias(rpb):
    col = np.arange(GRID_W)
    cs = np.clip(col - WIN_COLS // 2, 0, GRID_W - WIN_COLS)
    band = (col[None, :] >= cs[:, None]) & (col[None, :] < cs[:, None] + WIN_COLS)
    dc = np.clip(col[None, :] - col[:, None] + WIN_COLS - 1, 0, 2 * WIN_COLS - 2)
    pick = (np.arange(2 * WIN_COLS - 1)[:, None, None] == dc.T[None]) & band.T[None]
    tz = jnp.einsum('lhrm,mkc->lhrkc', rpb.astype(F32), jnp.asarray(pick, F32),
                    precision=lax.Precision.HIGHEST)
    tz = jnp.where(band.T[None, None, None], tz, NEG)
    depth = rpb.shape[0]
    tab = tz.reshape(depth, H_C // 2, 2, 2 * WIN_ROWS - 1, GRID_W, GRID_W)
    tab = jnp.transpose(tab, (0, 3, 1, 4, 2, 5))
    return tab.reshape(depth, 2 * WIN_ROWS - 1, H_C // 2, GRID_W, 2 * GRID_W)


def _outproj_kernel(ya_ref, yb_ref, yc_ref, x_ref, gt_ref, g_ref, wa_ref, wb_ref, wc_ref, o_ref):
    u = (_dot(ya_ref[0].astype(BF16), wa_ref[...]) + _dot(yb_ref[0].astype(BF16), wb_ref[...])
         + _dot(yc_ref[0].astype(BF16), wc_ref[...]))
    ms = jnp.mean(u * u, axis=-1, keepdims=True)
    o_ref[0] = x_ref[0] + gt_ref[0] * (u * lax.rsqrt(ms + EPS) * g_ref[...])


def _out_projection(ya, yb, yc, x, gt, g, wts, layer, tm):
    b, t, d = x.shape
    wa, wb, wc = wts
    bm = gt.shape[0]
    mod_map = (lambda i, j: (i, 0, 0)) if bm > 1 else (lambda i, j: (0, 0, 0))
    tile = lambda n: pl.BlockSpec((1, tm, n), lambda i, j: (i, j, 0))
    return pl.pallas_call(
        _outproj_kernel,
        grid=(b, t // tm),
        in_specs=[tile(ya.shape[2]), tile(yb.shape[2]), tile(yc.shape[2]), tile(d),
                  pl.BlockSpec((1, 1, d), mod_map)] + [_layer_spec(a, layer) for a in (g, wa, wb, wc)],
        out_specs=tile(d),
        out_shape=jax.ShapeDtypeStruct((b, t, d), F32),
        compiler_params=_cparams("arbitrary", "arbitrary"),
        name="out_projection",
    )(ya, yb, yc, x, gt, g, wa, wb, wc)


def _mlstm_weights(w):
    lead = w.shape[:-1]
    q = DH_B // 4
    blk = w[..., OFF_B:OFF_G].astype(BF16).reshape(lead + (5, H_B, DH_B))
    gap = jnp.zeros(lead + (2, H_B, LANES // 2 - 2 * q), BF16)
    qk = blk[..., :2, :, :]
    qk = jnp.concatenate([qk[..., 0:q], qk[..., 2 * q:3 * q], gap, qk[..., q:2 * q], qk[..., 3 * q:], gap], axis=-1)
    rest = jnp.pad(blk[..., 2:, :, :], [(0, 0)] * (len(lead) + 2) + [(0, LANES - DH_B)])
    wb = jnp.concatenate([qk, rest], axis=-3).reshape(lead + (5 * H_B * LANES,))
    g = jnp.swapaxes(w[..., OFF_G:OFF_C].astype(BF16).reshape(lead + (4, H_B)), -1, -2)
    wgt = jnp.pad(g, [(0, 0)] * (len(lead) + 1) + [(0, 4)]).reshape(lead + (H_B * 8,))
    return wb, jnp.swapaxes(wgt, -1, -2)


def _rope_tables(t):
    pos = np.arange(t)
    q = DH_B // 4
    inv = ROPE_BASE ** (-jnp.arange(0, 2 * q, 2, dtype=F32) / (2 * q))
    ang_r = jnp.asarray(pos // GRID_W, F32)[:, None] * inv[None, :]
    ang_c = jnp.asarray(pos % GRID_W, F32)[:, None] * inv[None, :]
    zeros = jnp.zeros((t, LANES // 2 - 2 * q), F32)
    cos_half = jnp.concatenate([jnp.cos(ang_r), jnp.cos(ang_c), zeros], axis=1)
    sin_half = jnp.concatenate([jnp.sin(ang_r), jnp.sin(ang_c), zeros], axis=1)
    return (jnp.concatenate([cos_half, cos_half], axis=1),
            jnp.concatenate([-sin_half, sin_half], axis=1))


def _pad_heads(v):
    v = v.reshape(v.shape[:-1] + (H_B, DH_B))
    v = jnp.pad(v, [(0, 0)] * (v.ndim - 1) + [(0, LANES - DH_B)])
    return v.reshape(v.shape[:-2] + (H_B * LANES,))


def kernel(x, c, ctx, c_ctx, w_mod, b_mod, g_pre, g_post, w_in, w_out, hgrn_lb, hgrn_gn, mlstm_gate_b, mlstm_gn, na_rpb):
    depth = w_in.shape[0]
    b, t, d = x.shape
    tc = ctx.shape[1]
    assert t % CHUNK == 0 and tc % CHUNK == 0 and t // GRID_W >= WIN_ROWS and t % GRID_W == 0
    assert w_in.shape[2] == P_IN and (1 << N_LEVELS) == CHUNK

    lb_cum = jnp.cumsum(jax.nn.softmax(hgrn_lb.astype(F32), axis=0), axis=0)
    lb_all = lb_cum - lb_cum[0]
    hgrn_tabs = _hgrn_tables()
    tris = _tri_tables()
    rope = _rope_tables(t)
    mod_rows = 16
    cc = jnp.concatenate([c.astype(F32), c_ctx.astype(F32)[None],
                          jnp.zeros((mod_rows - b - 1, d), F32)], axis=0)
    tm = min(256, t)
    tmo = min(512, t)
    tmc = min(256, tc)

    wb_all, wgt_all = _mlstm_weights(w_in)
    gbc_all = jnp.pad(jnp.swapaxes(mlstm_gate_b.astype(F32), 1, 2), ((0, 0), (0, 0), (0, 4)))
    in_wts = (w_in[:, :, :OFF_B].astype(BF16), wb_all, w_in[:, :, OFF_C:].astype(BF16), wgt_all,
              gbc_all.reshape(depth, H_B * 8, 1))
    wo_b = jnp.pad(w_out[:, W_A:W_A + W_B].reshape(depth, H_B, DH_B, d),
                   ((0, 0), (0, 0), (0, LANES - DH_B), (0, 0)))
    out_wts = (w_out[:, :W_A].astype(BF16), wo_b.reshape(depth, H_B * LANES, d).astype(BF16),
               w_out[:, W_A + W_B:].astype(BF16))
    g_pre3 = g_pre.astype(F32).reshape(depth, 1, d)
    g_post3 = g_post.astype(F32).reshape(depth, 1, d)
    na_bias = _natten_bias(na_rpb)
    gnb_all = _pad_heads(mlstm_gn.astype(F32)).reshape(depth, H_B, 1, LANES)

    for l in range(depth):
        need_ctx = l < depth - 1
        mod = _modulation(cc, w_mod[l], b_mod[l])
        sh, sc, gt = (mod[:b, i * d:(i + 1) * d].reshape(b, 1, d) for i in range(3))
        shc, scc, gtc = (mod[b:b + 1, i * d:(i + 1) * d].reshape(1, 1, d) for i in range(3))

        pa, pai, pbqk, pbv, pcb, pcz, pgt = _in_projection(x, sh, sc, g_pre3, in_wts, l, rope, tm)
        pac, paic, pbqkc, pbvc, pcbc, pczc, pgtc = _in_projection(ctx, shc, scc, g_pre3, in_wts, l, None, tmc)

        ya, yac = _hgrn_mixer(pa, pai, pac, paic, lb_all[l], hgrn_gn[l], hgrn_tabs, need_ctx)
        yb, ybc = _mlstm_mixer(pbqk, pbv, pgt, pbqkc, pbvc, pgtc, gnb_all[l], tris, need_ctx)
        yc, ycc = _natten_mixer(pcb, pcz, pcbc, pczc, na_bias, l, need_ctx)

        x = _out_projection(ya, yb, yc, x, gt, g_post3, out_wts, l, tmo)
        if need_ctx:
            ctx = _out_projection(yac, ybc, ycc, ctx, gtc, g_post3, out_wts, l, tmc)
    return x
```

```python
import functools
import math

import numpy as np
import jax
import jax.numpy as jnp
from jax import lax
from jax.experimental import pallas as pl
from jax.experimental.pallas import tpu as pltpu

F32 = jnp.float32
BF16 = jnp.bfloat16

LANES = 128
GRID_W = 64
W_A, H_A, DH_A = 256, 4, 64
W_B, H_B, DH_B = 384, 4, 96
W_C, H_C, DH_C = 384, 6, 64
WIN_ROWS, WIN_COLS = 8, 16
ROPE_BASE = 10000.0
EPS = 1e-6
NEG = -1e30
LOG2_E = math.log2(math.e)
CHUNK = 128
N_LEVELS = 7
NA_ROW_UNROLL = 64
NA_STAGE_GROUP = 8
CHUNK_UNROLL = 8
MLSTM_GROUP = 32
OFF_B = 5 * W_A
OFF_G = OFF_B + 5 * W_B
OFF_C = OFF_G + 4 * H_B
P_IN = OFF_C + 4 * W_C
ONE_LANE = DH_B
VMEM_LIMIT = 56 * 1024 * 1024


def _cparams(*sem):
    return pltpu.CompilerParams(dimension_semantics=sem, vmem_limit_bytes=VMEM_LIMIT)


def _dot(a, b):
    return jnp.dot(a, b, preferred_element_type=F32)


def _dot_nt(a, b):
    return lax.dot_general(a, b, (((1,), (1,)), ((), ())), preferred_element_type=F32)


def _dot_tn(a, b):
    return lax.dot_general(a, b, (((0,), (0,)), ((), ())), preferred_element_type=F32)


def _chunk_loop(n, body):
    u = math.gcd(n, CHUNK_UNROLL)

    def step(i, carry):
        for j in range(u):
            body(i * u + j, carry)
        return carry

    return lax.fori_loop(0, n // u, step, 0)


def _grouped_loop(n, body, group=CHUNK_UNROLL):
    u = math.gcd(n, group)

    def step(i, carry):
        body([i * u + j for j in range(u)])
        return carry

    return lax.fori_loop(0, n // u, step, 0)


def _split3(x):
    hi = x.astype(BF16)
    r = x - hi.astype(F32)
    mid = r.astype(BF16)
    lo = (r - mid.astype(F32)).astype(BF16)
    return hi, mid, lo


def _exact_left(t01, x):
    n = x.shape[1]
    r = _dot(t01, jnp.concatenate(_split3(x), axis=1))
    return r[:, :n] + r[:, n:2 * n] + r[:, 2 * n:]


def _exact_right(x, t01):
    m = x.shape[0]
    pieces = jnp.concatenate([p.astype(F32) for p in _split3(x)], axis=0).astype(BF16)
    r = _dot(pieces, t01)
    return r[:m] + r[m:2 * m] + r[2 * m:]


def _sigmoid(x):
    return 1.0 / (1.0 + jnp.exp(-x))


def _silu(x):
    return x * _sigmoid(x)


def _log_sigmoid(x):
    return jnp.minimum(x, 0.0) - jnp.log(1.0 + jnp.exp(-jnp.abs(x)))


def _mod_kernel(c_ref, w_ref, b_ref, o_ref):
    s = _silu(c_ref[...])
    o_ref[...] = _dot(s.astype(BF16), w_ref[...].astype(BF16)) + b_ref[...]


def _modulation(cc, w, b):
    rows, d = cc.shape
    n = w.shape[1]
    tn = d
    assert n % tn == 0
    return pl.pallas_call(
        _mod_kernel,
        grid=(n // tn,),
        in_specs=[pl.BlockSpec((rows, d), lambda j: (0, 0)),
                  pl.BlockSpec((d, tn), lambda j: (0, j)),
                  pl.BlockSpec((1, tn), lambda j: (0, j))],
        out_specs=pl.BlockSpec((rows, tn), lambda j: (0, j)),
        out_shape=jax.ShapeDtypeStruct((rows, n), F32),
        compiler_params=_cparams("arbitrary"),
        name="modulation",
    )(cc, w, b.reshape(1, n))


def _inproj_kernel(*refs, rotary):
    if rotary:
        (x_ref, sh_ref, sc_ref, g_ref, wa_ref, wb_ref, wc_ref, wgt_ref, gbc_ref,
         cos_ref, sin_ref, pa_ref, pai_ref, pbqk_ref, pbv_ref, pcb_ref, pcz_ref, pgt_ref) = refs
    else:
        (x_ref, sh_ref, sc_ref, g_ref, wa_ref, wb_ref, wc_ref, wgt_ref, gbc_ref,
         pa_ref, pai_ref, pbqk_ref, pbv_ref, pcb_ref, pcz_ref, pgt_ref) = refs
    x = x_ref[0]
    ms = jnp.mean(x * x, axis=-1, keepdims=True)
    h = x * lax.rsqrt(ms + EPS) * g_ref[...]
    h = h * (1.0 + sc_ref[0]) + sh_ref[0]
    hb = h.astype(BF16)
    a = _dot(hb, wa_ref[...])
    pa_ref[0, :, :3 * W_A] = a[:, :3 * W_A]
    pai_ref[0] = a[:, 3 * W_A:4 * W_A].astype(BF16)
    pa_ref[0, :, 3 * W_A:] = a[:, 4 * W_A:]
    c = _dot(hb, wc_ref[...])
    pcb_ref[0] = c[:, :3 * W_C].astype(BF16)
    pcz_ref[0] = c[:, 3 * W_C:]
    pgt_ref[0] = _dot_nt(wgt_ref[...], hb) + gbc_ref[...]
    hw = H_B * LANES
    scale = DH_B ** -0.5
    for part in range(5):
        p = _dot(hb, wb_ref[:, part * hw:(part + 1) * hw])
        if part == 0:
            p = p * scale
        if part >= 2:
            pbv_ref[0, :, (part - 2) * hw:(part - 1) * hw] = p
        elif rotary:
            cos = cos_ref[...]
            sin = sin_ref[...]
            for hd in range(H_B):
                ph = p[:, hd * LANES:(hd + 1) * LANES]
                pbqk_ref[0, :, part * hw + hd * LANES:part * hw + (hd + 1) * LANES] = (
                    ph * cos + pltpu.roll(ph, LANES // 2, 1) * sin).astype(BF16)
        else:
            pbqk_ref[0, :, part * hw:(part + 1) * hw] = p.astype(BF16)


def _layer_spec(a, layer):
    return pl.BlockSpec((None,) + a.shape[1:], lambda i, j: (layer, 0, 0))


def _in_projection(x, sh, sc, g, wts, layer, rope, tm):
    b, t, d = x.shape
    wa, wb, wc, wgt, gbc = wts
    bm = sh.shape[0]
    mod_map = (lambda i, j: (i, 0, 0)) if bm > 1 else (lambda i, j: (0, 0, 0))
    in_specs = [pl.BlockSpec((1, tm, d), lambda i, j: (i, j, 0)),
                pl.BlockSpec((1, 1, d), mod_map),
                pl.BlockSpec((1, 1, d), mod_map)] + [_layer_spec(a, layer) for a in (g, wa, wb, wc, wgt, gbc)]
    args = [x, sh, sc, g, wa, wb, wc, wgt, gbc]
    if rope is not None:
        in_specs += [pl.BlockSpec((tm, LANES), lambda i, j: (j, 0))] * 2
        args += list(rope)
    ng = wgt.shape[1]
    hw = H_B * LANES
    widths = [(4 * W_A, F32), (W_A, BF16), (2 * hw, BF16), (3 * hw, F32), (3 * W_C, BF16), (W_C, F32)]
    out_specs = ([pl.BlockSpec((1, tm, n), lambda i, j: (i, j, 0)) for n, _ in widths]
                 + [pl.BlockSpec((1, ng, tm), lambda i, j: (i, 0, j))])
    out_shape = ([jax.ShapeDtypeStruct((b, t, n), dt) for n, dt in widths]
                 + [jax.ShapeDtypeStruct((b, ng, t), F32)])
    return pl.pallas_call(
        functools.partial(_inproj_kernel, rotary=rope is not None),
        grid=(b, t // tm),
        in_specs=in_specs,
        out_specs=out_specs,
        out_shape=out_shape,
        compiler_params=_cparams("arbitrary", "arbitrary"),
        name="in_projection_rope" if rope is not None else "in_projection",
    )(*args)


def _hgrn_tables():
    L, nl = CHUNK, N_LEVELS
    idx = np.arange(L)
    t = idx[:, None]
    u = idx[None, :]
    tabq, levels = [], []
    for rev in (False, True):
        tabq.append(((u <= t) if not rev else (u >= t)).astype(np.float32))
        x = t ^ u
        lvl = np.where(x > 0, np.floor(np.log2(np.maximum(x, 1))), float(nl))
        valid = (u <= t) if not rev else (u >= t)
        lvl = np.where(valid, lvl, -1.0).astype(np.float32)
        levels.append(np.concatenate([lvl, lvl], axis=1))
    return jnp.asarray(np.stack(tabq), dtype=BF16), jnp.asarray(np.stack(levels), dtype=BF16)


def _tri_tables():
    L = CHUNK
    idx = np.arange(L)
    t = idx[:, None]
    u = idx[None, :]
    fwd = (u <= t).astype(np.float32)
    bwd = (u >= t).astype(np.float32)
    spread = np.zeros((16, 2 * L), np.float32)
    spread[0:3, :L] = 1.0
    spread[3:6, L:] = 1.0
    return (jnp.asarray(np.concatenate([fwd.T, bwd.T], axis=1), dtype=BF16), jnp.asarray(spread, dtype=BF16))


def _hgrn_kernel(*refs, need_ctx, n_lat, n_ctx):
    (q_ref, ff_ref, fb_ref, i_ref, z_ref, qc_ref, ffc_ref, fbc_ref, ic_ref, zc_ref,
     lb_ref, gn_ref, tabq_ref, lvl_ref) = refs[:14]
    if need_ctx:
        y_ref, yc_ref, u_ref, dec_ref, gate_ref = refs[14:]
    else:
        y_ref, u_ref, dec_ref, gate_ref = refs[14:]
        yc_ref = None
    L, nl = CHUNK, N_LEVELS
    lane = lax.broadcasted_iota(jnp.int32, (1, LANES), 1)
    hm0 = lane < DH_A
    row_h = lax.broadcasted_iota(jnp.int32, (LANES, LANES), 0) // DH_A
    col_h = lax.broadcasted_iota(jnp.int32, (LANES, LANES), 1) // DH_A
    same_head = row_h == col_h
    gn = gn_ref[0]
    segs = ((0, n_ctx, qc_ref, (ffc_ref, fbc_ref), ic_ref, zc_ref, yc_ref),
            (n_ctx, n_lat, q_ref, (ff_ref, fb_ref), i_ref, z_ref, y_ref))

    def decay(fl, lbd):
        a0 = jnp.log(lbd)
        b0 = jnp.log(1.0 - lbd) + _log_sigmoid(fl)
        logf = jnp.maximum(a0, b0) + jnp.log(1.0 + jnp.exp(-jnp.abs(a0 - b0)))
        return logf * LOG2_E, (1.0 - lbd) * _sigmoid(-fl)

    def rows_of(c):
        return pl.ds(pl.multiple_of(c * L, L), L)

    tok = lax.broadcasted_iota(jnp.int32, (L, 1), 0)

    def level_exponents(logf, bcum, d):
        prev, nxt = pltpu.roll(logf, 1, 0), pltpu.roll(logf, L - 1, 0)
        if d == 0:
            eqs = [logf, logf + jnp.where((tok % 4) == 3, prev, 0.0)]
            eks = [None, jnp.where((tok % 4) == 0, nxt, 0.0)]
        else:
            eqs = [logf, logf + jnp.where((tok % 4) == 0, nxt, 0.0)]
            eks = [None, jnp.where((tok % 4) == 3, prev, 0.0)]
        for lv in range(2, nl):
            h = 1 << lv
            r = h - 1 if d == 0 else h
            g = jnp.concatenate([jnp.broadcast_to(bcum[blk + r:blk + r + 1], (2 * h, LANES))
                                 for blk in range(0, L, 2 * h)], axis=0)
            eqs.append(bcum - g)
            eks.append(g - bcum)
        return eqs, eks

    def chunk_total(bcum, d):
        return bcum[L - 1:L] if d == 0 else bcum[0:1]

    for base, n, _, f_refs, ai_ref, _, _ in segs:
        def increments(cs, base=base, f_refs=f_refs, ai_ref=ai_ref):
            jd = [(j, d) for j in range(len(cs)) for d in range(2)]
            lk = {(j, d): decay(f_refs[d][0, rows_of(cs[j]), :], lb_ref[d, 0]) for j, d in jd}
            bcum = {(j, d): _exact_left(tabq_ref[d], lk[(j, d)][0]) for j, d in jd}
            b_tot = {(j, d): chunk_total(bcum[(j, d)], d) for j, d in jd}
            vb = [ai_ref[0, rows_of(c), :].astype(BF16) for c in cs]
            kr = {x: (lk[x][1] * jnp.exp2(b_tot[x] - bcum[x])).astype(BF16) for x in jd}
            u = {(j, d): _dot_tn(kr[(j, d)], vb[j]) for j, d in jd}
            dec = {x: jnp.broadcast_to(jnp.exp2(b_tot[x]), (LANES, LANES)).T for x in jd}
            for j, d in jd:
                u_ref[d, base + cs[j]] = jnp.where(same_head, u[(j, d)], 0.0)
                dec_ref[d, base + cs[j]] = dec[(j, d)]
                tok_rows = rows_of(base + cs[j])
                gate_ref[d, 0, tok_rows, :] = lk[(j, d)][0]
                gate_ref[d, 1, tok_rows, :] = lk[(j, d)][1]
                gate_ref[d, 2, tok_rows, :] = bcum[(j, d)]
        _grouped_loop(n, increments)

    def advance(d, idx, s):
        u = u_ref[d, idx]
        u_ref[d, idx] = s
        return s * dec_ref[d, idx] + u

    states = (jnp.zeros((LANES, LANES), F32),) * 2
    for base, n, *_ in segs:
        def scan_step(i, states, base=base, n=n):
            return (advance(0, base + i, states[0]), advance(1, base + n - 1 - i, states[1]))
        states = lax.fori_loop(0, n, scan_step, states)

    def finish(o, z):
        sq = o * o
        ms0 = jnp.sum(jnp.where(hm0, sq, 0.0), axis=-1, keepdims=True)
        ms1 = jnp.sum(jnp.where(hm0, 0.0, sq), axis=-1, keepdims=True)
        ms = jnp.where(hm0, ms0, ms1) * (1.0 / DH_A)
        return o * lax.rsqrt(ms + EPS) * gn * _silu(z)

    for base, n, aq_ref, f_refs, ai_ref, zz_ref, out_ref in segs:
        if out_ref is None:
            continue

        def outputs(cs, base=base, aq_ref=aq_ref, f_refs=f_refs, ai_ref=ai_ref, zz_ref=zz_ref,
                    out_ref=out_ref):
            js = range(len(cs))
            jd = [(j, d) for j in js for d in range(2)]
            q = [_silu(aq_ref[0, rows_of(c), :]) * (DH_A ** -0.5) for c in cs]
            lk = {(j, d): (gate_ref[d, 0, rows_of(base + cs[j]), :], gate_ref[d, 1, rows_of(base + cs[j]), :])
                  for j, d in jd}
            bcum = {(j, d): gate_ref[d, 2, rows_of(base + cs[j]), :] for j, d in jd}
            ex = {(j, d): level_exponents(lk[(j, d)][0], bcum[(j, d)], d) for j, d in jd}
            a = {x: jnp.zeros((L, 2 * L), BF16) for x in jd}
            zero = jnp.zeros((DH_A, L), BF16)
            for lv in range(nl + 1):
                at_level = [lvl_ref[d] == jnp.asarray(lv, BF16) for d in range(2)]
                for j, d in jd:
                    k = lk[(j, d)][1]
                    eqs, eks = ex[(j, d)]
                    if lv < nl:
                        qt = q[j] * jnp.exp2(eqs[lv])
                        kl = k if eks[lv] is None else k * jnp.exp2(eks[lv])
                    else:
                        qt, kl = q[j], k
                    ktl = kl.T.astype(BF16)
                    w = jnp.concatenate([jnp.concatenate([ktl[:DH_A], zero], axis=0),
                                         jnp.concatenate([zero, ktl[DH_A:]], axis=0)], axis=1)
                    a[(j, d)] = jnp.where(at_level[d], _dot(qt.astype(BF16), w).astype(BF16), a[(j, d)])
            ai = [ai_ref[0, rows_of(c), :] for c in cs]
            vs = [jnp.concatenate([jnp.where(hm0, x, 0.0), jnp.where(hm0, 0.0, x)], axis=0).astype(BF16)
                  for x in ai]
            intra = {(j, d): _dot(a[(j, d)], vs[j]) for j, d in jd}
            inter = {(j, d): _dot((q[j] * jnp.exp2(bcum[(j, d)])).astype(BF16),
                                  u_ref[d, base + cs[j]].astype(BF16)) for j, d in jd}
            for j, c in enumerate(cs):
                o = intra[(j, 0)] + inter[(j, 0)] + intra[(j, 1)] + inter[(j, 1)]
                out_ref[0, rows_of(c), :] = finish(o, zz_ref[0, rows_of(c), :]).astype(out_ref.dtype)
        _grouped_loop(n, outputs)


def _hgrn_mixer(pa, pai, pac, paic, lb, gn, tabs, need_ctx):
    b, t, _ = pa.shape
    tc = pac.shape[1]
    npair = W_A // LANES
    n_chunks = (t + tc) // CHUNK
    lat = lambda part: pl.BlockSpec((1, t, LANES), lambda i, p, part=part: (i, 0, part * npair + p))
    ctx = lambda part: pl.BlockSpec((1, tc, LANES), lambda i, p, part=part: (i, 0, part * npair + p))
    whole = lambda a: pl.BlockSpec(a.shape, lambda i, p: (0, 0, 0))
    in_specs = ([lat(0), lat(1), lat(2), lat(0), lat(3), ctx(0), ctx(1), ctx(2), ctx(0), ctx(3),
                 pl.BlockSpec((2, 1, 1, LANES), lambda i, p: (0, p, 0, 0)),
                 pl.BlockSpec((1, 1, LANES), lambda i, p: (p, 0, 0))] + [whole(a) for a in tabs])
    out_specs = [pl.BlockSpec((1, t, LANES), lambda i, p: (i, 0, p))]
    out_shape = [jax.ShapeDtypeStruct((b, t, W_A), BF16)]
    scratch = [pltpu.VMEM((2, n_chunks, LANES, LANES), F32)] * 2
    scratch.append(pltpu.VMEM((2, 3, t + tc, LANES), F32))
    if need_ctx:
        out_specs.append(pl.BlockSpec((1, tc, LANES), lambda i, p: (i, 0, p)))
        out_shape.append(jax.ShapeDtypeStruct((b, tc, W_A), BF16))
    res = pl.pallas_call(
        functools.partial(_hgrn_kernel, need_ctx=need_ctx, n_lat=t // CHUNK, n_ctx=tc // CHUNK),
        grid=(b, npair),
        in_specs=in_specs,
        out_specs=out_specs,
        out_shape=out_shape,
        scratch_shapes=scratch,
        compiler_params=_cparams("arbitrary", "arbitrary"),
        name="hgrn2_mixer_ctx" if need_ctx else "hgrn2_mixer",
    )(pa, pa, pa, pai, pa, pac, pac, pac, paic, pac,
      lb.reshape(2, npair, 1, LANES), gn.reshape(npair, 1, LANES), *tabs)
    return (res[0], res[1]) if need_ctx else (res[0], None)


def _mlstm_kernel(*refs, need_ctx, n_lat, n_ctx):
    (q_ref, k_ref, v_ref, o_ref, z_ref, qc_ref, kc_ref, vc_ref, oc_ref, zc_ref,
     gt_ref, gtc_ref, gn_ref, trit_ref, ones_ref) = refs[:15]
    if need_ctx:
        y_ref, yc_ref, c_ref, st_ref, vt_ref = refs[15:]
    else:
        y_ref, c_ref, st_ref, vt_ref = refs[15:]
        yc_ref = None
    L = CHUNK
    sub = lax.broadcasted_iota(jnp.int32, (LANES, 1), 0)
    rows_i = lax.broadcasted_iota(jnp.int32, (L, L), 0)
    cols_i = lax.broadcasted_iota(jnp.int32, (L, L), 1)
    gn = gn_ref[0]
    segs = ((0, n_ctx, qc_ref, kc_ref, vc_ref, oc_ref, zc_ref, gtc_ref, yc_ref),
            (n_ctx, n_lat, q_ref, k_ref, v_ref, o_ref, z_ref, gt_ref, y_ref))
    B_TOT, M_LOC, M_PREV = 0, 1, 2

    def rows_of(c):
        return pl.ds(pl.multiple_of(c * L, L), L)

    loop = _chunk_loop

    def row_gates(grow):
        er = _exact_right(_log_sigmoid(grow), trit_ref[...])
        out = []
        for d in range(2):
            b_row = er[2 + d:3 + d, d * L:(d + 1) * L]
            out.append((grow[d:d + 1, :], b_row, b_row[:, L - 1:L] if d == 0 else b_row[:, 0:1]))
        return out

    for base, n, _, _, vv, *_ in segs:
        def transpose_values(cs, base=base, vv=vv):
            tiles = [jnp.where(sub == ONE_LANE, 1.0, vv[0, rows_of(c), :].T) for c in cs]
            for c, x in zip(cs, tiles):
                vt_ref[:, rows_of(base + c)] = x
        _grouped_loop(n, transpose_values, MLSTM_GROUP)

    for base, n, _, kk, _, _, _, ggt, _ in segs:
        def local(cs, base=base, kk=kk, ggt=ggt):
            jd = [(j, d) for j in range(len(cs)) for d in range(2)]
            gates = [row_gates(ggt[0, :, rows_of(c)]) for c in cs]
            w = {(j, d): gates[j][d][2] + gates[j][d][0] - gates[j][d][1] for j, d in jd}
            m_loc = {x: jnp.max(w[x], axis=1, keepdims=True) for x in jd}
            ew = {x: jnp.exp(w[x] - m_loc[x]) for x in jd}
            kb = [kk[0, rows_of(c), :].astype(BF16) for c in cs]
            vat = [vt_ref[:, rows_of(base + c)] for c in cs]
            lhs = {(j, d): (vat[j] * ew[(j, d)]).astype(BF16) for j, d in jd}
            out = {(j, d): _dot(lhs[(j, d)], kb[j]) for j, d in jd}
            for j, d in jd:
                c_ref[d, base + cs[j]] = out[(j, d)]
                st_ref[d, base + cs[j], B_TOT:B_TOT + 1, :] = jnp.broadcast_to(gates[j][d][2], (1, LANES))
                st_ref[d, base + cs[j], M_LOC:M_LOC + 1, :] = jnp.broadcast_to(m_loc[(j, d)], (1, LANES))
        _grouped_loop(n, local, MLSTM_GROUP)

    def advance(d, idx, state):
        c_prev, m_prev = state
        c_loc = c_ref[d, idx]
        b_tot = st_ref[d, idx, B_TOT:B_TOT + 1, :]
        m_loc = st_ref[d, idx, M_LOC:M_LOC + 1, :]
        c_ref[d, idx] = c_prev
        st_ref[d, idx, M_PREV:M_PREV + 1, :] = m_prev
        m_new = jnp.maximum(b_tot + m_prev, m_loc)
        return (jnp.exp(b_tot + m_prev - m_new) * c_prev + jnp.exp(m_loc - m_new) * c_loc, m_new)

    s0 = (jnp.zeros((LANES, LANES), F32), jnp.zeros((1, LANES), F32))
    states = (s0, s0)
    for base, n, *_ in segs:
        def scan_step(i, states, base=base, n=n):
            return (advance(0, base + i, states[0]), advance(1, base + n - 1 - i, states[1]))
        states = lax.fori_loop(0, n, scan_step, states)

    for base, n, qq, kk, _, oo, zz, ggt, out_ref in segs:
        if out_ref is None:
            continue

        def outputs(cs, base=base, qq=qq, kk=kk, oo=oo, zz=zz, ggt=ggt, out_ref=out_ref):
            js = range(len(cs))
            jd = [(j, d) for j in js for d in range(2)]
            qb = [qq[0, rows_of(c), :].astype(BF16) for c in cs]
            sq = [_dot_nt(jnp.concatenate([kk[0, rows_of(c), :].astype(BF16),
                                           c_ref[0, base + c].astype(BF16),
                                           c_ref[1, base + c].astype(BF16)], axis=0), qb[j])
                  for j, c in enumerate(cs)]
            gates = [row_gates(ggt[0, :, rows_of(c)]) for c in cs]
            pieces = [jnp.concatenate(
                [p.astype(F32) for d in range(2) for p in _split3(gates[j][d][0] - gates[j][d][1])]
                + [jnp.zeros((10, L), F32)], axis=0).astype(BF16) for j in js]
            r_bc = [_dot_tn(x, ones_ref[...]) for x in pieces]
            valid = [rows_i <= cols_i, rows_i >= cols_i]
            dm = {(j, d): jnp.where(valid[d], gates[j][d][1] + r_bc[j][:, d * L:(d + 1) * L], NEG)
                  for j, d in jd}
            inter = {(j, d): gates[j][d][1] + st_ref[d, base + cs[j], M_PREV:M_PREV + 1, 0:1] for j, d in jd}
            m_t = {x: jnp.maximum(jnp.max(dm[x], axis=0, keepdims=True), inter[x]) for x in jd}
            p = {(j, d): sq[j][:L] * jnp.exp(dm[(j, d)] - m_t[(j, d)]) for j, d in jd}
            e_in = {x: jnp.exp(inter[x] - m_t[x]) for x in jd}
            vat = [vt_ref[:, rows_of(base + c)].astype(BF16) for c in cs]
            pv = {(j, d): _dot(vat[j], p[(j, d)].astype(BF16)) for j, d in jd}
            hts = []
            for j in js:
                ht = jnp.zeros((LANES, L), F32)
                for d in range(2):
                    x = (j, d)
                    qct = sq[j][(1 + d) * L:(2 + d) * L]
                    num = pv[x] + e_in[x] * qct
                    den = jnp.sum(p[x], axis=0, keepdims=True) + e_in[x] * qct[ONE_LANE:ONE_LANE + 1, :]
                    ht = ht + num / jnp.maximum(jnp.abs(den), jnp.exp(-m_t[x]))
                hts.append(jnp.where(sub < DH_B, ht, 0.0))
            ms = [jnp.sum(ht * ht, axis=0, keepdims=True) * (1.0 / DH_B) for ht in hts]
            h = [(ht * lax.rsqrt(m + EPS)).T for ht, m in zip(hts, ms)]
            for j, c in enumerate(cs):
                rows = rows_of(c)
                y = _sigmoid(oo[0, rows, :]) * (h[j] * gn) * _silu(zz[0, rows, :])
                out_ref[0, rows, :] = y.astype(out_ref.dtype)
        _grouped_loop(n, outputs, MLSTM_GROUP)


def _mlstm_mixer(pbqk, pbv, pgt, pbqkc, pbvc, pgtc, gn, tris, need_ctx):
    b, t, _ = pbv.shape
    tc = pbvc.shape[1]
    trit, spread = tris
    lat = lambda part: pl.BlockSpec((1, t, LANES), lambda i, h, part=part: (i, 0, part * H_B + h))
    ctx = lambda part: pl.BlockSpec((1, tc, LANES), lambda i, h, part=part: (i, 0, part * H_B + h))
    in_specs = ([lat(0), lat(1), lat(0), lat(1), lat(2), ctx(0), ctx(1), ctx(0), ctx(1), ctx(2)] +
                [pl.BlockSpec((1, 8, t), lambda i, h: (i, h, 0)),
                 pl.BlockSpec((1, 8, tc), lambda i, h: (i, h, 0)),
                 pl.BlockSpec((1, 1, LANES), lambda i, h: (h, 0, 0)),
                 pl.BlockSpec(trit.shape, lambda i, h: (0, 0)),
                 pl.BlockSpec(spread.shape, lambda i, h: (0, 0))])
    out_specs = [pl.BlockSpec((1, t, LANES), lambda i, h: (i, 0, h))]
    out_shape = [jax.ShapeDtypeStruct((b, t, H_B * LANES), BF16)]
    n_chunks = (t + tc) // CHUNK
    scratch = [pltpu.VMEM((2, n_chunks, LANES, LANES), F32),
               pltpu.VMEM((2, n_chunks, 8, LANES), F32),
               pltpu.VMEM((LANES, t + tc), F32)]
    if need_ctx:
        out_specs.append(pl.BlockSpec((1, tc, LANES), lambda i, h: (i, 0, h)))
        out_shape.append(jax.ShapeDtypeStruct((b, tc, H_B * LANES), BF16))
    res = pl.pallas_call(
        functools.partial(_mlstm_kernel, need_ctx=need_ctx, n_lat=t // CHUNK, n_ctx=tc // CHUNK),
        grid=(b, H_B),
        in_specs=in_specs,
        out_specs=out_specs,
        out_shape=out_shape,
        scratch_shapes=scratch,
        compiler_params=_cparams("arbitrary", "arbitrary"),
        name="mlstm_mixer_ctx" if need_ctx else "mlstm_mixer",
    )(pbqk, pbqk, pbv, pbv, pbv, pbqkc, pbqkc, pbvc, pbvc, pbvc, pgt, pgtc, gn, trit, spread)
    return (res[0], res[1]) if need_ctx else (res[0], None)


def _natten_kernel(*refs, need_ctx, n_rows):
    if need_ctx:
        (q_ref, k_ref, v_ref, z_ref, kc_ref, vc_ref, bias_ref, qc_ref, zc_ref, y_ref, yc_ref,
         vt_ref, vct_ref) = refs
    else:
        (q_ref, k_ref, v_ref, z_ref, kc_ref, vc_ref, bias_ref, y_ref, vt_ref, vct_ref) = refs
    lane = lax.broadcasted_iota(jnp.int32, (1, LANES), 1)
    hm0 = lane < DH_C
    scale = DH_C ** -0.5
    win = WIN_ROWS * GRID_W
    t = k_ref.shape[1]
    tc = kc_ref.shape[1]
    blk = LANES

    def transposed(tile):
        return tile.astype(F32).T.astype(BF16)

    nb = t // blk
    group = math.gcd(nb, NA_STAGE_GROUP)
    row_group_size = math.gcd(n_rows, NA_ROW_UNROLL)

    def blocks(i):
        return [pl.ds(pl.multiple_of((i * group + j) * blk, blk), blk) for j in range(group)]

    def stage(i, carry):
        tiles = [transposed(v_ref[0, rows, :]) for rows in blocks(i)]
        for rows, x in zip(blocks(i), tiles):
            vt_ref[0, :, rows] = x
        return carry

    lax.fori_loop(0, nb // group, stage, 0)

    def shifted(start):
        src = start + GRID_W
        if not isinstance(src, int):
            src = pl.multiple_of(src, GRID_W)
        return transposed(v_ref[0, pl.ds(src, blk), :])

    def stage_shifted(i, carry):
        tiles = [shifted(rows.start) for rows in blocks(i)]
        for rows, x in zip(blocks(i), tiles):
            vt_ref[1, :, rows] = x
        return carry

    lax.fori_loop(0, nb // group - 1, stage_shifted, 0)
    tail = [(nb - group + j) * blk for j in range(group - 1)]
    tiles = [shifted(start) for start in tail]
    for start, x in zip(tail, tiles):
        vt_ref[1, :, start:start + blk] = x
    last = v_ref[0, t - blk:t, :].astype(F32).T
    vt_ref[1, :, t - blk:t] = pltpu.roll(last, GRID_W, 1).astype(BF16)

    tiles = [transposed(vc_ref[0, i * blk:(i + 1) * blk, :]) for i in range(tc // blk)]
    for i, x in enumerate(tiles):
        vct_ref[:, i * blk:(i + 1) * blk] = x

    def attend(qs, keys, vals_t, biases):
        n = qs[0].shape[0]
        q2 = [jnp.concatenate([jnp.where(hm0, q, 0.0), jnp.where(hm0, 0.0, q)], axis=0).astype(BF16)
              for q in qs]
        lt = [_dot_nt(k, q) for k, q in zip(keys, q2)]
        lt = [x if b is None else jnp.concatenate([x[:b.shape[0]] + b, x[b.shape[0]:]], axis=0)
              for x, b in zip(lt, biases)]
        p = [jnp.exp(x - jnp.max(x, axis=0, keepdims=True)) for x in lt]
        den = [jnp.sum(x, axis=0, keepdims=True) for x in p]
        ot = [_dot(v, x.astype(BF16)) for v, x in zip(vals_t, p)]
        o = [(x / d).T for x, d in zip(ot, den)]
        return [jnp.where(hm0, x[:n], x[n:]) for x in o]

    def row_group(i, carry):
        rows = [i * row_group_size + j for j in range(row_group_size)]
        rs = [jnp.clip(r - WIN_ROWS // 2, 0, n_rows - WIN_ROWS) for r in rows]
        qrows = [pl.ds(pl.multiple_of(r * GRID_W, GRID_W), GRID_W) for r in rows]
        keys, vals_t, biases = [], [], []
        for r, s in zip(rows, rs):
            par = s % 2
            wrows = pl.ds(pl.multiple_of(s * GRID_W, GRID_W), win)
            wlanes = pl.ds(pl.multiple_of((s - par) * GRID_W, LANES), win)
            keys.append(jnp.concatenate([k_ref[0, wrows, :], kc_ref[0]], axis=0))
            vals_t.append(jnp.concatenate([vt_ref[par, :, wlanes], vct_ref[...]], axis=1))
            dr0 = s - r + (WIN_ROWS - 1)
            biases.append(jnp.concatenate([bias_ref[dr0 + j, 0] for j in range(WIN_ROWS)], axis=0))
        outs = attend([q_ref[0, qr, :].astype(F32) * scale for qr in qrows], keys, vals_t, biases)
        for qr, o in zip(qrows, outs):
            y_ref[0, qr, :] = (o * _silu(z_ref[0, qr, :])).astype(y_ref.dtype)
        return carry

    lax.fori_loop(0, n_rows // row_group_size, row_group, 0)

    if need_ctx:
        o, = attend([qc_ref[0].astype(F32) * scale], [kc_ref[0]], [vct_ref[...]], [None])
        yc_ref[0] = (o * _silu(zc_ref[0])).astype(yc_ref.dtype)


def _natten_mixer(pcb, pcz, pcbc, pczc, bias, layer, need_ctx):
    b, t, _ = pcb.shape
    tc = pcbc.shape[1]
    npair = W_C // LANES
    n_rows = t // GRID_W
    lat = lambda part: pl.BlockSpec((1, t, LANES), lambda i, p, part=part: (i, 0, part * npair + p))
    ctx = lambda part: pl.BlockSpec((1, tc, LANES), lambda i, p, part=part: (i, 0, part * npair + p))
    in_specs = [lat(0), lat(1), lat(2), lat(0), ctx(1), ctx(2),
                pl.BlockSpec((None, 2 * WIN_ROWS - 1, 1, GRID_W, LANES), lambda i, p: (layer, 0, p, 0, 0))]
    args = [pcb, pcb, pcb, pcz, pcbc, pcbc, bias]
    out_specs = [pl.BlockSpec((1, t, LANES), lambda i, p: (i, 0, p))]
    out_shape = [jax.ShapeDtypeStruct((b, t, W_C), BF16)]
    if need_ctx:
        in_specs += [ctx(0), ctx(0)]
        args += [pcbc, pczc]
        out_specs.append(pl.BlockSpec((1, tc, LANES), lambda i, p: (i, 0, p)))
        out_shape.append(jax.ShapeDtypeStruct((b, tc, W_C), BF16))
    scratch = [pltpu.VMEM((2, LANES, t), BF16), pltpu.VMEM((LANES, tc), BF16)]
    res = pl.pallas_call(
        functools.partial(_natten_kernel, need_ctx=need_ctx, n_rows=n_rows),
        grid=(b, npair),
        in_specs=in_specs,
        out_specs=out_specs,
        out_shape=out_shape,
        scratch_shapes=scratch,
        compiler_params=_cparams("arbitrary", "arbitrary"),
        name="natten_mixer_ctx" if need_ctx else "natten_mixer",
    )(*args)
    return (res[0], res[1]) if need_ctx else (res[0], None)


def _natten_bias(rpb):
    col = np.arange(GRID_W)
    cs = np.clip(col - WIN_COLS // 2, 0, GRID_W - WIN_COLS)
    band = (col[None, :] >= cs[:, None]) & (col[None, :] < cs[:, None] + WIN_COLS)
    dc = np.clip(col[None, :] - col[:, None] + WIN_COLS - 1, 0, 2 * WIN_COLS - 2)
    pick = (np.arange(2 * WIN_COLS - 1)[:, None, None] == dc.T[None]) & band.T[None]
    tz = jnp.einsum('lhrm,mkc->lhrkc', rpb.astype(F32), jnp.asarray(pick, F32),
                    precision=lax.Precision.HIGHEST)
    tz = jnp.where(band.T[None, None, None], tz, NEG)
    depth = rpb.shape[0]
    tab = tz.reshape(depth, H_C // 2, 2, 2 * WIN_ROWS - 1, GRID_W, GRID_W)
    tab = jnp.transpose(tab, (0, 3, 1, 4, 2, 5))
    return tab.reshape(depth, 2 * WIN_ROWS - 1, H_C // 2, GRID_W, 2 * GRID_W)


def _outproj_kernel(ya_ref, yb_ref, yc_ref, x_ref, gt_ref, g_ref, wa_ref, wb_ref, wc_ref, o_ref):
    u = (_dot(ya_ref[0].astype(BF16), wa_ref[...]) + _dot(yb_ref[0].astype(BF16), wb_ref[...])
         + _dot(yc_ref[0].astype(BF16), wc_ref[...]))
    ms = jnp.mean(u * u, axis=-1, keepdims=True)
    o_ref[0] = x_ref[0] + gt_ref[0] * (u * lax.rsqrt(ms + EPS) * g_ref[...])


def _out_projection(ya, yb, yc, x, gt, g, wts, layer, tm):
    b, t, d = x.shape
    wa, wb, wc = wts
    bm = gt.shape[0]
    mod_map = (lambda i, j: (i, 0, 0)) if bm > 1 else (lambda i, j: (0, 0, 0))
    tile = lambda n: pl.BlockSpec((1, tm, n), lambda i, j: (i, j, 0))
    return pl.pallas_call(
        _outproj_kernel,
        grid=(b, t // tm),
        in_specs=[tile(ya.shape[2]), tile(yb.shape[2]), tile(yc.shape[2]), tile(d),
                  pl.BlockSpec((1, 1, d), mod_map)] + [_layer_spec(a, layer) for a in (g, wa, wb, wc)],
        out_specs=tile(d),
        out_shape=jax.ShapeDtypeStruct((b, t, d), F32),
        compiler_params=_cparams("arbitrary", "arbitrary"),
        name="out_projection",
    )(ya, yb, yc, x, gt, g, wa, wb, wc)


def _mlstm_weights(w):
    lead = w.shape[:-1]
    q = DH_B // 4
    blk = w[..., OFF_B:OFF_G].astype(BF16).reshape(lead + (5, H_B, DH_B))
    gap = jnp.zeros(lead + (2, H_B, LANES // 2 - 2 * q), BF16)
    qk = blk[..., :2, :, :]
    qk = jnp.concatenate([qk[..., 0:q], qk[..., 2 * q:3 * q], gap, qk[..., q:2 * q], qk[..., 3 * q:], gap], axis=-1)
    rest = jnp.pad(blk[..., 2:, :, :], [(0, 0)] * (len(lead) + 2) + [(0, LANES - DH_B)])
    wb = jnp.concatenate([qk, rest], axis=-3).reshape(lead + (5 * H_B * LANES,))
    g = jnp.swapaxes(w[..., OFF_G:OFF_C].astype(BF16).reshape(lead + (4, H_B)), -1, -2)
    wgt = jnp.pad(g, [(0, 0)] * (len(lead) + 1) + [(0, 4)]).reshape(lead + (H_B * 8,))
    return wb, jnp.swapaxes(wgt, -1, -2)


def _rope_tables(t):
    pos = np.arange(t)
    q = DH_B // 4
    inv = ROPE_BASE ** (-jnp.arange(0, 2 * q, 2, dtype=F32) / (2 * q))
    ang_r = jnp.asarray(pos // GRID_W, F32)[:, None] * inv[None, :]
    ang_c = jnp.asarray(pos % GRID_W, F32)[:, None] * inv[None, :]
    zeros = jnp.zeros((t, LANES // 2 - 2 * q), F32)
    cos_half = jnp.concatenate([jnp.cos(ang_r), jnp.cos(ang_c), zeros], axis=1)
    sin_half = jnp.concatenate([jnp.sin(ang_r), jnp.sin(ang_c), zeros], axis=1)
    return (jnp.concatenate([cos_half, cos_half], axis=1),
            jnp.concatenate([-sin_half, sin_half], axis=1))


def _pad_heads(v):
    v = v.reshape(v.shape[:-1] + (H_B, DH_B))
    v = jnp.pad(v, [(0, 0)] * (v.ndim - 1) + [(0, LANES - DH_B)])
    return v.reshape(v.shape[:-2] + (H_B * LANES,))


def kernel(x, c, ctx, c_ctx, w_mod, b_mod, g_pre, g_post, w_in, w_out, hgrn_lb, hgrn_gn, mlstm_gate_b, mlstm_gn, na_rpb):
    depth = w_in.shape[0]
    b, t, d = x.shape
    tc = ctx.shape[1]
    assert t % CHUNK == 0 and tc % CHUNK == 0 and t // GRID_W >= WIN_ROWS and t % GRID_W == 0
    assert w_in.shape[2] == P_IN and (1 << N_LEVELS) == CHUNK

    lb_cum = jnp.cumsum(jax.nn.softmax(hgrn_lb.astype(F32), axis=0), axis=0)
    lb_all = lb_cum - lb_cum[0]
    hgrn_tabs = _hgrn_tables()
    tris = _tri_tables()
    rope = _rope_tables(t)
    mod_rows = 16
    cc = jnp.concatenate([c.astype(F32), c_ctx.astype(F32)[None],
                          jnp.zeros((mod_rows - b - 1, d), F32)], axis=0)
    tm = min(256, t)
    tmo = min(512, t)
    tmc = min(256, tc)

    wb_all, wgt_all = _mlstm_weights(w_in)
    gbc_all = jnp.pad(jnp.swapaxes(mlstm_gate_b.astype(F32), 1, 2), ((0, 0), (0, 0), (0, 4)))
    in_wts = (w_in[:, :, :OFF_B].astype(BF16), wb_all, w_in[:, :, OFF_C:].astype(BF16), wgt_all,
              gbc_all.reshape(depth, H_B * 8, 1))
    wo_b = jnp.pad(w_out[:, W_A:W_A + W_B].reshape(depth, H_B, DH_B, d),
                   ((0, 0), (0, 0), (0, LANES - DH_B), (0, 0)))
    out_wts = (w_out[:, :W_A].astype(BF16), wo_b.reshape(depth, H_B * LANES, d).astype(BF16),
               w_out[:, W_A + W_B:].astype(BF16))
    g_pre3 = g_pre.astype(F32).reshape(depth, 1, d)
    g_post3 = g_post.astype(F32).reshape(depth, 1, d)
    na_bias = _natten_bias(na_rpb)
    gnb_all = _pad_heads(mlstm_gn.astype(F32)).reshape(depth, H_B, 1, LANES)

    for l in range(depth):
        need_ctx = l < depth - 1
        mod = _modulation(cc, w_mod[l], b_mod[l])
        sh, sc, gt = (mod[:b, i * d:(i + 1) * d].reshape(b, 1, d) for i in range(3))
        shc, scc, gtc = (mod[b:b + 1, i * d:(i + 1) * d].reshape(1, 1, d) for i in range(3))

        pa, pai, pbqk, pbv, pcb, pcz, pgt = _in_projection(x, sh, sc, g_pre3, in_wts, l, rope, tm)
        pac, paic, pbqkc, pbvc, pcbc, pczc, pgtc = _in_projection(ctx, shc, scc, g_pre3, in_wts, l, None, tmc)

        ya, yac = _hgrn_mixer(pa, pai, pac, paic, lb_all[l], hgrn_gn[l], hgrn_tabs, need_ctx)
        yb, ybc = _mlstm_mixer(pbqk, pbv, pgt, pbqkc, pbvc, pgtc, gnb_all[l], tris, need_ctx)
        yc, ycc = _natten_mixer(pcb, pcz, pcbc, pczc, na_bias, l, need_ctx)

        x = _out_projection(ya, yb, yc, x, gt, g_post3, out_wts, l, tmo)
        if need_ctx:
            ctx = _out_projection(yac, ybc, ycc, ctx, gtc, g_post3, out_wts, l, tmc)
    return x
```

```python
import functools
import math

import numpy as np
import jax
import jax.numpy as jnp
from jax import lax
from jax.experimental import pallas as pl
from jax.experimental.pallas import tpu as pltpu

F32 = jnp.float32
BF16 = jnp.bfloat16

LANES = 128
GRID_W = 64
W_A, H_A, DH_A = 256, 4, 64
W_B, H_B, DH_B = 384, 4, 96
W_C, H_C, DH_C = 384, 6, 64
WIN_ROWS, WIN_COLS = 8, 16
ROPE_BASE = 10000.0
EPS = 1e-6
NEG = -1e30
LOG2_E = math.log2(math.e)
CHUNK = 128
N_LEVELS = 7
NA_ROW_UNROLL = 64
NA_STAGE_GROUP = 8
CHUNK_UNROLL = 8
MLSTM_GROUP = 32
OFF_B = 5 * W_A
OFF_G = OFF_B + 5 * W_B
OFF_C = OFF_G + 4 * H_B
P_IN = OFF_C + 4 * W_C
ONE_LANE = DH_B
VMEM_LIMIT = 56 * 1024 * 1024


def _cparams(*sem):
    return pltpu.CompilerParams(dimension_semantics=sem, vmem_limit_bytes=VMEM_LIMIT)


def _dot(a, b):
    return jnp.dot(a, b, preferred_element_type=F32)


def _dot_nt(a, b):
    return lax.dot_general(a, b, (((1,), (1,)), ((), ())), preferred_element_type=F32)


def _dot_tn(a, b):
    return lax.dot_general(a, b, (((0,), (0,)), ((), ())), preferred_element_type=F32)


def _chunk_loop(n, body):
    u = math.gcd(n, CHUNK_UNROLL)

    def step(i, carry):
        for j in range(u):
            body(i * u + j, carry)
        return carry

    return lax.fori_loop(0, n // u, step, 0)


def _grouped_loop(n, body, group=CHUNK_UNROLL):
    u = math.gcd(n, group)

    def step(i, carry):
        body([i * u + j for j in range(u)])
        return carry

    return lax.fori_loop(0, n // u, step, 0)


def _split3(x):
    hi = x.astype(BF16)
    r = x - hi.astype(F32)
    mid = r.astype(BF16)
    lo = (r - mid.astype(F32)).astype(BF16)
    return hi, mid, lo


def _exact_left(t01, x):
    n = x.shape[1]
    r = _dot(t01, jnp.concatenate(_split3(x), axis=1))
    return r[:, :n] + r[:, n:2 * n] + r[:, 2 * n:]


def _exact_right(x, t01):
    m = x.shape[0]
    pieces = jnp.concatenate([p.astype(F32) for p in _split3(x)], axis=0).astype(BF16)
    r = _dot(pieces, t01)
    return r[:m] + r[m:2 * m] + r[2 * m:]


def _sigmoid(x):
    return 1.0 / (1.0 + jnp.exp(-x))


def _silu(x):
    return x * _sigmoid(x)


def _log_sigmoid(x):
    return jnp.minimum(x, 0.0) - jnp.log(1.0 + jnp.exp(-jnp.abs(x)))


def _mod_kernel(c_ref, w_ref, b_ref, o_ref):
    s = _silu(c_ref[...])
    o_ref[...] = _dot(s.astype(BF16), w_ref[...].astype(BF16)) + b_ref[...]


def _modulation(cc, w, b):
    rows, d = cc.shape
    n = w.shape[1]
    tn = d
    assert n % tn == 0
    return pl.pallas_call(
        _mod_kernel,
        grid=(n // tn,),
        in_specs=[pl.BlockSpec((rows, d), lambda j: (0, 0)),
                  pl.BlockSpec((d, tn), lambda j: (0, j)),
                  pl.BlockSpec((1, tn), lambda j: (0, j))],
        out_specs=pl.BlockSpec((rows, tn), lambda j: (0, j)),
        out_shape=jax.ShapeDtypeStruct((rows, n), F32),
        compiler_params=_cparams("arbitrary"),
        name="modulation",
    )(cc, w, b.reshape(1, n))


def _inproj_kernel(*refs, rotary):
    if rotary:
        (x_ref, sh_ref, sc_ref, g_ref, wa_ref, wb_ref, wc_ref, wgt_ref, gbc_ref,
         cos_ref, sin_ref, pa_ref, pai_ref, pbqk_ref, pbv_ref, pcb_ref, pcz_ref, pgt_ref) = refs
    else:
        (x_ref, sh_ref, sc_ref, g_ref, wa_ref, wb_ref, wc_ref, wgt_ref, gbc_ref,
         pa_ref, pai_ref, pbqk_ref, pbv_ref, pcb_ref, pcz_ref, pgt_ref) = refs
    x = x_ref[0]
    ms = jnp.mean(x * x, axis=-1, keepdims=True)
    h = x * lax.rsqrt(ms + EPS) * g_ref[...]
    h = h * (1.0 + sc_ref[0]) + sh_ref[0]
    hb = h.astype(BF16)
    a = _dot(hb, wa_ref[...])
    pa_ref[0, :, :3 * W_A] = a[:, :3 * W_A]
    pai_ref[0] = a[:, 3 * W_A:4 * W_A].astype(BF16)
    pa_ref[0, :, 3 * W_A:] = a[:, 4 * W_A:]
    c = _dot(hb, wc_ref[...])
    pcb_ref[0] = c[:, :3 * W_C].astype(BF16)
    pcz_ref[0] = c[:, 3 * W_C:]
    pgt_ref[0] = _dot_nt(wgt_ref[...], hb) + gbc_ref[...]
    hw = H_B * LANES
    scale = DH_B ** -0.5
    for part in range(5):
        p = _dot(hb, wb_ref[:, part * hw:(part + 1) * hw])
        if part == 0:
            p = p * scale
        if part >= 2:
            pbv_ref[0, :, (part - 2) * hw:(part - 1) * hw] = p
        elif rotary:
            cos = cos_ref[...]
            sin = sin_ref[...]
            for hd in range(H_B):
                ph = p[:, hd * LANES:(hd + 1) * LANES]
                pbqk_ref[0, :, part * hw + hd * LANES:part * hw + (hd + 1) * LANES] = (
                    ph * cos + pltpu.roll(ph, LANES // 2, 1) * sin).astype(BF16)
        else:
            pbqk_ref[0, :, part * hw:(part + 1) * hw] = p.astype(BF16)


def _layer_spec(a, layer):
    return pl.BlockSpec((None,) + a.shape[1:], lambda i, j: (layer, 0, 0))


def _in_projection(x, sh, sc, g, wts, layer, rope, tm):
    b, t, d = x.shape
    wa, wb, wc, wgt, gbc = wts
    bm = sh.shape[0]
    mod_map = (lambda i, j: (i, 0, 0)) if bm > 1 else (lambda i, j: (0, 0, 0))
    in_specs = [pl.BlockSpec((1, tm, d), lambda i, j: (i, j, 0)),
                pl.BlockSpec((1, 1, d), mod_map),
                pl.BlockSpec((1, 1, d), mod_map)] + [_layer_spec(a, layer) for a in (g, wa, wb, wc, wgt, gbc)]
    args = [x, sh, sc, g, wa, wb, wc, wgt, gbc]
    if rope is not None:
        in_specs += [pl.BlockSpec((tm, LANES), lambda i, j: (j, 0))] * 2
        args += list(rope)
    ng = wgt.shape[1]
    hw = H_B * LANES
    widths = [(4 * W_A, F32), (W_A, BF16), (2 * hw, BF16), (3 * hw, F32), (3 * W_C, BF16), (W_C, F32)]
    out_specs = ([pl.BlockSpec((1, tm, n), lambda i, j: (i, j, 0)) for n, _ in widths]
                 + [pl.BlockSpec((1, ng, tm), lambda i, j: (i, 0, j))])
    out_shape = ([jax.ShapeDtypeStruct((b, t, n), dt) for n, dt in widths]
                 + [jax.ShapeDtypeStruct((b, ng, t), F32)])
    return pl.pallas_call(
        functools.partial(_inproj_kernel, rotary=rope is not None),
        grid=(b, t // tm),
        in_specs=in_specs,
        out_specs=out_specs,
        out_shape=out_shape,
        compiler_params=_cparams("arbitrary", "arbitrary"),
        name="in_projection_rope" if rope is not None else "in_projection",
    )(*args)


def _hgrn_tables():
    L, nl = CHUNK, N_LEVELS
    idx = np.arange(L)
    t = idx[:, None]
    u = idx[None, :]
    tabq, levels = [], []
    for rev in (False, True):
        tabq.append(((u <= t) if not rev else (u >= t)).astype(np.float32))
        x = t ^ u
        lvl = np.where(x > 0, np.floor(np.log2(np.maximum(x, 1))), float(nl))
        valid = (u <= t) if not rev else (u >= t)
        lvl = np.where(valid, lvl, -1.0).astype(np.float32)
        levels.append(np.concatenate([lvl, lvl], axis=1))
    return jnp.asarray(np.stack(tabq), dtype=BF16), jnp.asarray(np.stack(levels), dtype=BF16)


def _tri_tables():
    L = CHUNK
    idx = np.arange(L)
    t = idx[:, None]
    u = idx[None, :]
    fwd = (u <= t).astype(np.float32)
    bwd = (u >= t).astype(np.float32)
    spread = np.zeros((16, 2 * L), np.float32)
    spread[0:3, :L] = 1.0
    spread[3:6, L:] = 1.0
    return (jnp.asarray(np.concatenate([fwd.T, bwd.T], axis=1), dtype=BF16), jnp.asarray(spread, dtype=BF16))


def _hgrn_kernel(*refs, need_ctx, n_lat, n_ctx):
    (q_ref, ff_ref, fb_ref, i_ref, z_ref, qc_ref, ffc_ref, fbc_ref, ic_ref, zc_ref,
     lb_ref, gn_ref, tabq_ref, lvl_ref) = refs[:14]
    if need_ctx:
        y_ref, yc_ref, u_ref, dec_ref, gate_ref = refs[14:]
    else:
        y_ref, u_ref, dec_ref, gate_ref = refs[14:]
        yc_ref = None
    L, nl = CHUNK, N_LEVELS
    lane = lax.broadcasted_iota(jnp.int32, (1, LANES), 1)
    hm0 = lane < DH_A
    row_h = lax.broadcasted_iota(jnp.int32, (LANES, LANES), 0) // DH_A
    col_h = lax.broadcasted_iota(jnp.int32, (LANES, LANES), 1) // DH_A
    same_head = row_h == col_h
    gn = gn_ref[0]
    segs = ((0, n_ctx, qc_ref, (ffc_ref, fbc_ref), ic_ref, zc_ref, yc_ref),
            (n_ctx, n_lat, q_ref, (ff_ref, fb_ref), i_ref, z_ref, y_ref))

    def decay(fl, lbd):
        a0 = jnp.log(lbd)
        b0 = jnp.log(1.0 - lbd) + _log_sigmoid(fl)
        logf = jnp.maximum(a0, b0) + jnp.log(1.0 + jnp.exp(-jnp.abs(a0 - b0)))
        return logf * LOG2_E, (1.0 - lbd) * _sigmoid(-fl)

    def rows_of(c):
        return pl.ds(pl.multiple_of(c * L, L), L)

    tok = lax.broadcasted_iota(jnp.int32, (L, 1), 0)

    def level_exponents(logf, bcum, d):
        prev, nxt = pltpu.roll(logf, 1, 0), pltpu.roll(logf, L - 1, 0)
        if d == 0:
            eqs = [logf, logf + jnp.where((tok % 4) == 3, prev, 0.0)]
            eks = [None, jnp.where((tok % 4) == 0, nxt, 0.0)]
        else:
            eqs = [logf, logf + jnp.where((tok % 4) == 0, nxt, 0.0)]
            eks = [None, jnp.where((tok % 4) == 3, prev, 0.0)]
        for lv in range(2, nl):
            h = 1 << lv
            r = h - 1 if d == 0 else h
            g = jnp.concatenate([jnp.broadcast_to(bcum[blk + r:blk + r + 1], (2 * h, LANES))
                                 for blk in range(0, L, 2 * h)], axis=0)
            eqs.append(bcum - g)
            eks.append(g - bcum)
        return eqs, eks

    def chunk_total(bcum, d):
        return bcum[L - 1:L] if d == 0 else bcum[0:1]

    for base, n, _, f_refs, ai_ref, _, _ in segs:
        def increments(cs, base=base, f_refs=f_refs, ai_ref=ai_ref):
            jd = [(j, d) for j in range(len(cs)) for d in range(2)]
            lk = {(j, d): decay(f_refs[d][0, rows_of(cs[j]), :], lb_ref[d, 0]) for j, d in jd}
            bcum = {(j, d): _exact_left(tabq_ref[d], lk[(j, d)][0]) for j, d in jd}
            b_tot = {(j, d): chunk_total(bcum[(j, d)], d) for j, d in jd}
            vb = [ai_ref[0, rows_of(c), :].astype(BF16) for c in cs]
            kr = {x: (lk[x][1] * jnp.exp2(b_tot[x] - bcum[x])).astype(BF16) for x in jd}
            u = {(j, d): _dot_tn(kr[(j, d)], vb[j]) for j, d in jd}
            dec = {x: jnp.broadcast_to(jnp.exp2(b_tot[x]), (LANES, LANES)).T for x in jd}
            for j, d in jd:
                u_ref[d, base + cs[j]] = jnp.where(same_head, u[(j, d)], 0.0)
                dec_ref[d, base + cs[j]] = dec[(j, d)]
                tok_rows = rows_of(base + cs[j])
                gate_ref[d, 0, tok_rows, :] = lk[(j, d)][0]
                gate_ref[d, 1, tok_rows, :] = lk[(j, d)][1]
                gate_ref[d, 2, tok_rows, :] = bcum[(j, d)]
        _grouped_loop(n, increments)

    def advance(d, idx, s):
        u = u_ref[d, idx]
        u_ref[d, idx] = s
        return s * dec_ref[d, idx] + u

    states = (jnp.zeros((LANES, LANES), F32),) * 2
    for base, n, *_ in segs:
        def scan_step(i, states, base=base, n=n):
            return (advance(0, base + i, states[0]), advance(1, base + n - 1 - i, states[1]))
        states = lax.fori_loop(0, n, scan_step, states)

    def finish(o, z):
        sq = o * o
        ms0 = jnp.sum(jnp.where(hm0, sq, 0.0), axis=-1, keepdims=True)
        ms1 = jnp.sum(jnp.where(hm0, 0.0, sq), axis=-1, keepdims=True)
        ms = jnp.where(hm0, ms0, ms1) * (1.0 / DH_A)
        return o * lax.rsqrt(ms + EPS) * gn * _silu(z)

    for base, n, aq_ref, f_refs, ai_ref, zz_ref, out_ref in segs:
        if out_ref is None:
            continue

        def outputs(cs, base=base, aq_ref=aq_ref, f_refs=f_refs, ai_ref=ai_ref, zz_ref=zz_ref,
                    out_ref=out_ref):
            js = range(len(cs))
            jd = [(j, d) for j in js for d in range(2)]
            q = [_silu(aq_ref[0, rows_of(c), :]) * (DH_A ** -0.5) for c in cs]
            lk = {(j, d): (gate_ref[d, 0, rows_of(base + cs[j]), :], gate_ref[d, 1, rows_of(base + cs[j]), :])
                  for j, d in jd}
            bcum = {(j, d): gate_ref[d, 2, rows_of(base + cs[j]), :] for j, d in jd}
            ex = {(j, d): level_exponents(lk[(j, d)][0], bcum[(j, d)], d) for j, d in jd}
            a = {x: jnp.zeros((L, 2 * L), BF16) for x in jd}
            zero = jnp.zeros((DH_A, L), BF16)
            for lv in range(nl + 1):
                at_level = [lvl_ref[d] == jnp.asarray(lv, BF16) for d in range(2)]
                for j, d in jd:
                    k = lk[(j, d)][1]
                    eqs, eks = ex[(j, d)]
                    if lv < nl:
                        qt = q[j] * jnp.exp2(eqs[lv])
                        kl = k if eks[lv] is None else k * jnp.exp2(eks[lv])
                    else:
                        qt, kl = q[j], k
                    ktl = kl.T.astype(BF16)
                    w = jnp.concatenate([jnp.concatenate([ktl[:DH_A], zero], axis=0),
                                         jnp.concatenate([zero, ktl[DH_A:]], axis=0)], axis=1)
                    a[(j, d)] = jnp.where(at_level[d], _dot(qt.astype(BF16), w).astype(BF16), a[(j, d)])
            ai = [ai_ref[0, rows_of(c), :] for c in cs]
            vs = [jnp.concatenate([jnp.where(hm0, x, 0.0), jnp.where(hm0, 0.0, x)], axis=0).astype(BF16)
                  for x in ai]
            intra = {(j, d): _dot(a[(j, d)], vs[j]) for j, d in jd}
            inter = {(j, d): _dot((q[j] * jnp.exp2(bcum[(j, d)])).astype(BF16),
                                  u_ref[d, base + cs[j]].astype(BF16)) for j, d in jd}
            for j, c in enumerate(cs):
                o = intra[(j, 0)] + inter[(j, 0)] + intra[(j, 1)] + inter[(j, 1)]
                out_ref[0, rows_of(c), :] = finish(o, zz_ref[0, rows_of(c), :]).astype(out_ref.dtype)
        _grouped_loop(n, outputs)


def _hgrn_mixer(pa, pai, pac, paic, lb, gn, tabs, need_ctx):
    b, t, _ = pa.shape
    tc = pac.shape[1]
    npair = W_A // LANES
    n_chunks = (t + tc) // CHUNK
    lat = lambda part: pl.BlockSpec((1, t, LANES), lambda i, p, part=part: (i, 0, part * npair + p))
    ctx = lambda part: pl.BlockSpec((1, tc, LANES), lambda i, p, part=part: (i, 0, part * npair + p))
    whole = lambda a: pl.BlockSpec(a.shape, lambda i, p: (0, 0, 0))
    in_specs = ([lat(0), lat(1), lat(2), lat(0), lat(3), ctx(0), ctx(1), ctx(2), ctx(0), ctx(3),
                 pl.BlockSpec((2, 1, 1, LANES), lambda i, p: (0, p, 0, 0)),
                 pl.BlockSpec((1, 1, LANES), lambda i, p: (p, 0, 0))] + [whole(a) for a in tabs])
    out_specs = [pl.BlockSpec((1, t, LANES), lambda i, p: (i, 0, p))]
    out_shape = [jax.ShapeDtypeStruct((b, t, W_A), BF16)]
    scratch = [pltpu.VMEM((2, n_chunks, LANES, LANES), F32)] * 2
    scratch.append(pltpu.VMEM((2, 3, t + tc, LANES), F32))
    if need_ctx:
        out_specs.append(pl.BlockSpec((1, tc, LANES), lambda i, p: (i, 0, p)))
        out_shape.append(jax.ShapeDtypeStruct((b, tc, W_A), BF16))
    res = pl.pallas_call(
        functools.partial(_hgrn_kernel, need_ctx=need_ctx, n_lat=t // CHUNK, n_ctx=tc // CHUNK),
        grid=(b, npair),
        in_specs=in_specs,
        out_specs=out_specs,
        out_shape=out_shape,
        scratch_shapes=scratch,
        compiler_params=_cparams("arbitrary", "arbitrary"),
        name="hgrn2_mixer_ctx" if need_ctx else "hgrn2_mixer",
    )(pa, pa, pa, pai, pa, pac, pac, pac, paic, pac,
      lb.reshape(2, npair, 1, LANES), gn.reshape(npair, 1, LANES), *tabs)
    return (res[0], res[1]) if need_ctx else (res[0], None)


def _mlstm_kernel(*refs, need_ctx, n_lat, n_ctx):
    (q_ref, k_ref, v_ref, o_ref, z_ref, qc_ref, kc_ref, vc_ref, oc_ref, zc_ref,
     gt_ref, gtc_ref, gn_ref, trit_ref, ones_ref) = refs[:15]
    if need_ctx:
        y_ref, yc_ref, c_ref, st_ref, vt_ref = refs[15:]
    else:
        y_ref, c_ref, st_ref, vt_ref = refs[15:]
        yc_ref = None
    L = CHUNK
    sub = lax.broadcasted_iota(jnp.int32, (LANES, 1), 0)
    rows_i = lax.broadcasted_iota(jnp.int32, (L, L), 0)
    cols_i = lax.broadcasted_iota(jnp.int32, (L, L), 1)
    gn = gn_ref[0]
    segs = ((0, n_ctx, qc_ref, kc_ref, vc_ref, oc_ref, zc_ref, gtc_ref, yc_ref),
            (n_ctx, n_lat, q_ref, k_ref, v_ref, o_ref, z_ref, gt_ref, y_ref))
    B_TOT, M_LOC, M_PREV = 0, 1, 2

    def rows_of(c):
        return pl.ds(pl.multiple_of(c * L, L), L)

    loop = _chunk_loop

    def row_gates(grow):
        er = _exact_right(_log_sigmoid(grow), trit_ref[...])
        out = []
        for d in range(2):
            b_row = er[2 + d:3 + d, d * L:(d + 1) * L]
            out.append((grow[d:d + 1, :], b_row, b_row[:, L - 1:L] if d == 0 else b_row[:, 0:1]))
        return out

    for base, n, _, _, vv, *_ in segs:
        def transpose_values(cs, base=base, vv=vv):
            tiles = [jnp.where(sub == ONE_LANE, 1.0, vv[0, rows_of(c), :].T) for c in cs]
            for c, x in zip(cs, tiles):
                vt_ref[:, rows_of(base + c)] = x
        _grouped_loop(n, transpose_values, MLSTM_GROUP)

    for base, n, _, kk, _, _, _, ggt, _ in segs:
        def local(cs, base=base, kk=kk, ggt=ggt):
            jd = [(j, d) for j in range(len(cs)) for d in range(2)]
            gates = [row_gates(ggt[0, :, rows_of(c)]) for c in cs]
            w = {(j, d): gates[j][d][2] + gates[j][d][0] - gates[j][d][1] for j, d in jd}
            m_loc = {x: jnp.max(w[x], axis=1, keepdims=True) for x in jd}
            ew = {x: jnp.exp(w[x] - m_loc[x]) for x in jd}
            kb = [kk[0, rows_of(c), :].astype(BF16) for c in cs]
            vat = [vt_ref[:, rows_of(base + c)] for c in cs]
            lhs = {(j, d): (vat[j] * ew[(j, d)]).astype(BF16) for j, d in jd}
            out = {(j, d): _dot(lhs[(j, d)], kb[j]) for j, d in jd}
            for j, d in jd:
                c_ref[d, base + cs[j]] = out[(j, d)]
                st_ref[d, base + cs[j], B_TOT:B_TOT + 1, :] = jnp.broadcast_to(gates[j][d][2], (1, LANES))
                st_ref[d, base + cs[j], M_LOC:M_LOC + 1, :] = jnp.broadcast_to(m_loc[(j, d)], (1, LANES))
        _grouped_loop(n, local, MLSTM_GROUP)

    def advance(d, idx, state):
        c_prev, m_prev = state
        c_loc = c_ref[d, idx]
        b_tot = st_ref[d, idx, B_TOT:B_TOT + 1, :]
        m_loc = st_ref[d, idx, M_LOC:M_LOC + 1, :]
        c_ref[d, idx] = c_prev
        st_ref[d, idx, M_PREV:M_PREV + 1, :] = m_prev
        m_new = jnp.maximum(b_tot + m_prev, m_loc)
        return (jnp.exp(b_tot + m_prev - m_new) * c_prev + jnp.exp(m_loc - m_new) * c_loc, m_new)

    s0 = (jnp.zeros((LANES, LANES), F32), jnp.zeros((1, LANES), F32))
    states = (s0, s0)
    for base, n, *_ in segs:
        def scan_step(i, states, base=base, n=n):
            return (advance(0, base + i, states[0]), advance(1, base + n - 1 - i, states[1]))
        states = lax.fori_loop(0, n, scan_step, states)

    for base, n, qq, kk, _, oo, zz, ggt, out_ref in segs:
        if out_ref is None:
            continue

        def outputs(cs, base=base, qq=qq, kk=kk, oo=oo, zz=zz, ggt=ggt, out_ref=out_ref):
            js = range(len(cs))
            jd = [(j, d) for j in js for d in range(2)]
            qb = [qq[0, rows_of(c), :].astype(BF16) for c in cs]
            sq = [_dot_nt(jnp.concatenate([kk[0, rows_of(c), :].astype(BF16),
                                           c_ref[0, base + c].astype(BF16),
                                           c_ref[1, base + c].astype(BF16)], axis=0), qb[j])
                  for j, c in enumerate(cs)]
            gates = [row_gates(ggt[0, :, rows_of(c)]) for c in cs]
            pieces = [jnp.concatenate(
                [p.astype(F32) for d in range(2) for p in _split3(gates[j][d][0] - gates[j][d][1])]
                + [jnp.zeros((10, L), F32)], axis=0).astype(BF16) for j in js]
            r_bc = [_dot_tn(x, ones_ref[...]) for x in pieces]
            valid = [rows_i <= cols_i, rows_i >= cols_i]
            dm = {(j, d): jnp.where(valid[d], gates[j][d][1] + r_bc[j][:, d * L:(d + 1) * L], NEG)
                  for j, d in jd}
            inter = {(j, d): gates[j][d][1] + st_ref[d, base + cs[j], M_PREV:M_PREV + 1, 0:1] for j, d in jd}
            m_t = {x: jnp.maximum(jnp.max(dm[x], axis=0, keepdims=True), inter[x]) for x in jd}
            p = {(j, d): sq[j][:L] * jnp.exp(dm[(j, d)] - m_t[(j, d)]) for j, d in jd}
            e_in = {x: jnp.exp(inter[x] - m_t[x]) for x in jd}
            vat = [vt_ref[:, rows_of(base + c)].astype(BF16) for c in cs]
            pv = {(j, d): _dot(vat[j], p[(j, d)].astype(BF16)) for j, d in jd}
            hts = []
            for j in js:
                ht = jnp.zeros((LANES, L), F32)
                for d in range(2):
                    x = (j, d)
                    qct = sq[j][(1 + d) * L:(2 + d) * L]
                    num = pv[x] + e_in[x] * qct
                    den = jnp.sum(p[x], axis=0, keepdims=True) + e_in[x] * qct[ONE_LANE:ONE_LANE + 1, :]
                    ht = ht + num / jnp.maximum(jnp.abs(den), jnp.exp(-m_t[x]))
                hts.append(jnp.where(sub < DH_B, ht, 0.0))
            ms = [jnp.sum(ht * ht, axis=0, keepdims=True) * (1.0 / DH_B) for ht in hts]
            h = [(ht * lax.rsqrt(m + EPS)).T for ht, m in zip(hts, ms)]
            for j, c in enumerate(cs):
                rows = rows_of(c)
                y = _sigmoid(oo[0, rows, :]) * (h[j] * gn) * _silu(zz[0, rows, :])
                out_ref[0, rows, :] = y.astype(out_ref.dtype)
        _grouped_loop(n, outputs, MLSTM_GROUP)


def _mlstm_mixer(pbqk, pbv, pgt, pbqkc, pbvc, pgtc, gn, tris, need_ctx):
    b, t, _ = pbv.shape
    tc = pbvc.shape[1]
    trit, spread = tris
    lat = lambda part: pl.BlockSpec((1, t, LANES), lambda i, h, part=part: (i, 0, part * H_B + h))
    ctx = lambda part: pl.BlockSpec((1, tc, LANES), lambda i, h, part=part: (i, 0, part * H_B + h))
    in_specs = ([lat(0), lat(1), lat(0), lat(1), lat(2), ctx(0), ctx(1), ctx(0), ctx(1), ctx(2)] +
                [pl.BlockSpec((1, 8, t), lambda i, h: (i, h, 0)),
                 pl.BlockSpec((1, 8, tc), lambda i, h: (i, h, 0)),
                 pl.BlockSpec((1, 1, LANES), lambda i, h: (h, 0, 0)),
                 pl.BlockSpec(trit.shape, lambda i, h: (0, 0)),
                 pl.BlockSpec(spread.shape, lambda i, h: (0, 0))])
    out_specs = [pl.BlockSpec((1, t, LANES), lambda i, h: (i, 0, h))]
    out_shape = [jax.ShapeDtypeStruct((b, t, H_B * LANES), BF16)]
    n_chunks = (t + tc) // CHUNK
    scratch = [pltpu.VMEM((2, n_chunks, LANES, LANES), F32),
               pltpu.VMEM((2, n_chunks, 8, LANES), F32),
               pltpu.VMEM((LANES, t + tc), F32)]
    if need_ctx:
        out_specs.append(pl.BlockSpec((1, tc, LANES), lambda i, h: (i, 0, h)))
        out_shape.append(jax.ShapeDtypeStruct((b, tc, H_B * LANES), BF16))
    res = pl.pallas_call(
        functools.partial(_mlstm_kernel, need_ctx=need_ctx, n_lat=t // CHUNK, n_ctx=tc // CHUNK),
        grid=(b, H_B),
        in_specs=in_specs,
        out_specs=out_specs,
        out_shape=out_shape,
        scratch_shapes=scratch,
        compiler_params=_cparams("arbitrary", "arbitrary"),
        name="mlstm_mixer_ctx" if need_ctx else "mlstm_mixer",
    )(pbqk, pbqk, pbv, pbv, pbv, pbqkc, pbqkc, pbvc, pbvc, pbvc, pgt, pgtc, gn, trit, spread)
    return (res[0], res[1]) if need_ctx else (res[0], None)


def _natten_kernel(*refs, need_ctx, n_rows):
    if need_ctx:
        (q_ref, k_ref, v_ref, z_ref, kc_ref, vc_ref, bias_ref, qc_ref, zc_ref, y_ref, yc_ref,
         vt_ref, vct_ref) = refs
    else:
        (q_ref, k_ref, v_ref, z_ref, kc_ref, vc_ref, bias_ref, y_ref, vt_ref, vct_ref) = refs
    lane = lax.broadcasted_iota(jnp.int32, (1, LANES), 1)
    hm0 = lane < DH_C
    scale = DH_C ** -0.5
    win = WIN_ROWS * GRID_W
    t = k_ref.shape[1]
    tc = kc_ref.shape[1]
    blk = LANES

    def transposed(tile):
        return tile.astype(F32).T.astype(BF16)

    nb = t // blk
    group = math.gcd(nb, NA_STAGE_GROUP)
    row_group_size = math.gcd(n_rows, NA_ROW_UNROLL)

    def blocks(i):
        return [pl.ds(pl.multiple_of((i * group + j) * blk, blk), blk) for j in range(group)]

    def stage(i, carry):
        tiles = [transposed(v_ref[0, rows, :]) for rows in blocks(i)]
        for rows, x in zip(blocks(i), tiles):
            vt_ref[0, :, rows] = x
        return carry

    lax.fori_loop(0, nb // group, stage, 0)

    def shifted(start):
        src = start + GRID_W
        if not isinstance(src, int):
            src = pl.multiple_of(src, GRID_W)
        return transposed(v_ref[0, pl.ds(src, blk), :])

    def stage_shifted(i, carry):
        tiles = [shifted(rows.start) for rows in blocks(i)]
        for rows, x in zip(blocks(i), tiles):
            vt_ref[1, :, rows] = x
        return carry

    lax.fori_loop(0, nb // group - 1, stage_shifted, 0)
    tail = [(nb - group + j) * blk for j in range(group - 1)]
    tiles = [shifted(start) for start in tail]
    for start, x in zip(tail, tiles):
        vt_ref[1, :, start:start + blk] = x
    last = v_ref[0, t - blk:t, :].astype(F32).T
    vt_ref[1, :, t - blk:t] = pltpu.roll(last, GRID_W, 1).astype(BF16)

    tiles = [transposed(vc_ref[0, i * blk:(i + 1) * blk, :]) for i in range(tc // blk)]
    for i, x in enumerate(tiles):
        vct_ref[:, i * blk:(i + 1) * blk] = x

    def attend(qs, keys, vals_t, biases):
        n = qs[0].shape[0]
        q2 = [jnp.concatenate([jnp.where(hm0, q, 0.0), jnp.where(hm0, 0.0, q)], axis=0).astype(BF16)
              for q in qs]
        lt = [_dot_nt(k, q) for k, q in zip(keys, q2)]
        lt = [x if b is None else jnp.concatenate([x[:b.shape[0]] + b, x[b.shape[0]:]], axis=0)
              for x, b in zip(lt, biases)]
        p = [jnp.exp(x - jnp.max(x, axis=0, keepdims=True)) for x in lt]
        den = [jnp.sum(x, axis=0, keepdims=True) for x in p]
        ot = [_dot(v, x.astype(BF16)) for v, x in zip(vals_t, p)]
        o = [(x / d).T for x, d in zip(ot, den)]
        return [jnp.where(hm0, x[:n], x[n:]) for x in o]

    def row_group(i, carry):
        rows = [i * row_group_size + j for j in range(row_group_size)]
        rs = [jnp.clip(r - WIN_ROWS // 2, 0, n_rows - WIN_ROWS) for r in rows]
        qrows = [pl.ds(pl.multiple_of(r * GRID_W, GRID_W), GRID_W) for r in rows]
        keys, vals_t, biases = [], [], []
        for r, s in zip(rows, rs):
            par = s % 2
            wrows = pl.ds(pl.multiple_of(s * GRID_W, GRID_W), win)
            wlanes = pl.ds(pl.multiple_of((s - par) * GRID_W, LANES), win)
            keys.append(jnp.concatenate([k_ref[0, wrows, :], kc_ref[0]], axis=0))
            vals_t.append(jnp.concatenate([vt_ref[par, :, wlanes], vct_ref[...]], axis=1))
            dr0 = s - r + (WIN_ROWS - 1)
            biases.append(jnp.concatenate([bias_ref[dr0 + j, 0] for j in range(WIN_ROWS)], axis=0))
        outs = attend([q_ref[0, qr, :].astype(F32) * scale for qr in qrows], keys, vals_t, biases)
        for qr, o in zip(qrows, outs):
            y_ref[0, qr, :] = (o * _silu(z_ref[0, qr, :])).astype(y_ref.dtype)
        return carry

    lax.fori_loop(0, n_rows // row_group_size, row_group, 0)

    if need_ctx:
        o, = attend([qc_ref[0].astype(F32) * scale], [kc_ref[0]], [vct_ref[...]], [None])
        yc_ref[0] = (o * _silu(zc_ref[0])).astype(yc_ref.dtype)


def _natten_mixer(pcb, pcz, pcbc, pczc, bias, layer, need_ctx):
    b, t, _ = pcb.shape
    tc = pcbc.shape[1]
    npair = W_C // LANES
    n_rows = t // GRID_W
    lat = lambda part: pl.BlockSpec((1, t, LANES), lambda i, p, part=part: (i, 0, part * npair + p))
    ctx = lambda part: pl.BlockSpec((1, tc, LANES), lambda i, p, part=part: (i, 0, part * npair + p))
    in_specs = [lat(0), lat(1), lat(2), lat(0), ctx(1), ctx(2),
                pl.BlockSpec((None, 2 * WIN_ROWS - 1, 1, GRID_W, LANES), lambda i, p: (layer, 0, p, 0, 0))]
    args = [pcb, pcb, pcb, pcz, pcbc, pcbc, bias]
    out_specs = [pl.BlockSpec((1, t, LANES), lambda i, p: (i, 0, p))]
    out_shape = [jax.ShapeDtypeStruct((b, t, W_C), BF16)]
    if need_ctx:
        in_specs += [ctx(0), ctx(0)]
        args += [pcbc, pczc]
        out_specs.append(pl.BlockSpec((1, tc, LANES), lambda i, p: (i, 0, p)))
        out_shape.append(jax.ShapeDtypeStruct((b, tc, W_C), BF16))
    scratch = [pltpu.VMEM((2, LANES, t), BF16), pltpu.VMEM((LANES, tc), BF16)]
    res = pl.pallas_call(
        functools.partial(_natten_kernel, need_ctx=need_ctx, n_rows=n_rows),
        grid=(b, npair),
        in_specs=in_specs,
        out_specs=out_specs,
        out_shape=out_shape,
        scratch_shapes=scratch,
        compiler_params=_cparams("arbitrary", "arbitrary"),
        name="natten_mixer_ctx" if need_ctx else "natten_mixer",
    )(*args)
    return (res[0], res[1]) if need_ctx else (res[0], None)


def _natten_bias(rpb):
    col = np.arange(GRID_W)
    cs = np.clip(col - WIN_COLS // 2, 0, GRID_W - WIN_COLS)
    band = (col[None, :] >= cs[:, None]) & (col[None, :] < cs[:, None] + WIN_COLS)
    dc = np.clip(col[None, :] - col[:, None] + WIN_COLS - 1, 0, 2 * WIN_COLS - 2)
    pick = (np.arange(2 * WIN_COLS - 1)[:, None, None] == dc.T[None]) & band.T[None]
    tz = jnp.einsum('lhrm,mkc->lhrkc', rpb.astype(F32), jnp.asarray(pick, F32),
                    precision=lax.Precision.HIGHEST)
    tz = jnp.where(band.T[None, None, None], tz, NEG)
    depth = rpb.shape[0]
    tab = tz.reshape(depth, H_C // 2, 2, 2 * WIN_ROWS - 1, GRID_W, GRID_W)
    tab = jnp.transpose(tab, (0, 3, 1, 4, 2, 5))
    return tab.reshape(depth, 2 * WIN_ROWS - 1, H_C // 2, GRID_W, 2 * GRID_W)


def _outproj_kernel(ya_ref, yb_ref, yc_ref, x_ref, gt_ref, g_ref, wa_ref, wb_ref, wc_ref, o_ref):
    u = (_dot(ya_ref[0].astype(BF16), wa_ref[...]) + _dot(yb_ref[0].astype(BF16), wb_ref[...])
         + _dot(yc_ref[0].astype(BF16), wc_ref[...]))
    ms = jnp.mean(u * u, axis=-1, keepdims=True)
    o_ref[0] = x_ref[0] + gt_ref[0] * (u * lax.rsqrt(ms + EPS) * g_ref[...])


def _out_projection(ya, yb, yc, x, gt, g, wts, layer, tm):
    b, t, d = x.shape
    wa, wb, wc = wts
    bm = gt.shape[0]
    mod_map = (lambda i, j: (i, 0, 0)) if bm > 1 else (lambda i, j: (0, 0, 0))
    tile = lambda n: pl.BlockSpec((1, tm, n), lambda i, j: (i, j, 0))
    return pl.pallas_call(
        _outproj_kernel,
        grid=(b, t // tm),
        in_specs=[tile(ya.shape[2]), tile(yb.shape[2]), tile(yc.shape[2]), tile(d),
                  pl.BlockSpec((1, 1, d), mod_map)] + [_layer_spec(a, layer) for a in (g, wa, wb, wc)],
        out_specs=tile(d),
        out_shape=jax.ShapeDtypeStruct((b, t, d), F32),
        compiler_params=_cparams("arbitrary", "arbitrary"),
        name="out_projection",
    )(ya, yb, yc, x, gt, g, wa, wb, wc)


def _mlstm_weights(w):
    lead = w.shape[:-1]
    q = DH_B // 4
    blk = w[..., OFF_B:OFF_G].astype(BF16).reshape(lead + (5, H_B, DH_B))
    gap = jnp.zeros(lead + (2, H_B, LANES // 2 - 2 * q), BF16)
    qk = blk[..., :2, :, :]
    qk = jnp.concatenate([qk[..., 0:q], qk[..., 2 * q:3 * q], gap, qk[..., q:2 * q], qk[..., 3 * q:], gap], axis=-1)
    rest = jnp.pad(blk[..., 2:, :, :], [(0, 0)] * (len(lead) + 2) + [(0, LANES - DH_B)])
    wb = jnp.concatenate([qk, rest], axis=-3).reshape(lead + (5 * H_B * LANES,))
    g = jnp.swapaxes(w[..., OFF_G:OFF_C].astype(BF16).reshape(lead + (4, H_B)), -1, -2)
    wgt = jnp.pad(g, [(0, 0)] * (len(lead) + 1) + [(0, 4)]).reshape(lead + (H_B * 8,))
    return wb, jnp.swapaxes(wgt, -1, -2)


def _rope_tables(t):
    pos = np.arange(t)
    q = DH_B // 4
    inv = ROPE_BASE ** (-jnp.arange(0, 2 * q, 2, dtype=F32) / (2 * q))
    ang_r = jnp.asarray(pos // GRID_W, F32)[:, None] * inv[None, :]
    ang_c = jnp.asarray(pos % GRID_W, F32)[:, None] * inv[None, :]
    zeros = jnp.zeros((t, LANES // 2 - 2 * q), F32)
    cos_half = jnp.concatenate([jnp.cos(ang_r), jnp.cos(ang_c), zeros], axis=1)
    sin_half = jnp.concatenate([jnp.sin(ang_r), jnp.sin(ang_c), zeros], axis=1)
    return (jnp.concatenate([cos_half, cos_half], axis=1),
            jnp.concatenate([-sin_half, sin_half], axis=1))


def _pad_heads(v):
    v = v.reshape(v.shape[:-1] + (H_B, DH_B))
    v = jnp.pad(v, [(0, 0)] * (v.ndim - 1) + [(0, LANES - DH_B)])
    return v.reshape(v.shape[:-2] + (H_B * LANES,))


def kernel(x, c, ctx, c_ctx, w_mod, b_mod, g_pre, g_post, w_in, w_out, hgrn_lb, hgrn_gn, mlstm_gate_b, mlstm_gn, na_rpb):
    depth = w_in.shape[0]
    b, t, d = x.shape
    tc = ctx.shape[1]
    assert t % CHUNK == 0 and tc % CHUNK == 0 and t // GRID_W >= WIN_ROWS and t % GRID_W == 0
    assert w_in.shape[2] == P_IN and (1 << N_LEVELS) == CHUNK

    lb_cum = jnp.cumsum(jax.nn.softmax(hgrn_lb.astype(F32), axis=0), axis=0)
    lb_all = lb_cum - lb_cum[0]
    hgrn_tabs = _hgrn_tables()
    tris = _tri_tables()
    rope = _rope_tables(t)
    mod_rows = 16
    cc = jnp.concatenate([c.astype(F32), c_ctx.astype(F32)[None],
                          jnp.zeros((mod_rows - b - 1, d), F32)], axis=0)
    tm = min(512, t)
    tmo = min(1024, t)
    tmc = min(256, tc)

    wb_all, wgt_all = _mlstm_weights(w_in)
    gbc_all = jnp.pad(jnp.swapaxes(mlstm_gate_b.astype(F32), 1, 2), ((0, 0), (0, 0), (0, 4)))
    in_wts = (w_in[:, :, :OFF_B].astype(BF16), wb_all, w_in[:, :, OFF_C:].astype(BF16), wgt_all,
              gbc_all.reshape(depth, H_B * 8, 1))
    wo_b = jnp.pad(w_out[:, W_A:W_A + W_B].reshape(depth, H_B, DH_B, d),
                   ((0, 0), (0, 0), (0, LANES - DH_B), (0, 0)))
    out_wts = (w_out[:, :W_A].astype(BF16), wo_b.reshape(depth, H_B * LANES, d).astype(BF16),
               w_out[:, W_A + W_B:].astype(BF16))
    g_pre3 = g_pre.astype(F32).reshape(depth, 1, d)
    g_post3 = g_post.astype(F32).reshape(depth, 1, d)
    na_bias = _natten_bias(na_rpb)
    gnb_all = _pad_heads(mlstm_gn.astype(F32)).reshape(depth, H_B, 1, LANES)

    for l in range(depth):
        need_ctx = l < depth - 1
        mod = _modulation(cc, w_mod[l], b_mod[l])
        sh, sc, gt = (mod[:b, i * d:(i + 1) * d].reshape(b, 1, d) for i in range(3))
        shc, scc, gtc = (mod[b:b + 1, i * d:(i + 1) * d].reshape(1, 1, d) for i in range(3))

        pa, pai, pbqk, pbv, pcb, pcz, pgt = _in_projection(x, sh, sc, g_pre3, in_wts, l, rope, tm)
        pac, paic, pbqkc, pbvc, pcbc, pczc, pgtc = _in_projection(ctx, shc, scc, g_pre3, in_wts, l, None, tmc)

        ya, yac = _hgrn_mixer(pa, pai, pac, paic, lb_all[l], hgrn_gn[l], hgrn_tabs, need_ctx)
        yb, ybc = _mlstm_mixer(pbqk, pbv, pgt, pbqkc, pbvc, pgtc, gnb_all[l], tris, need_ctx)
        yc, ycc = _natten_mixer(pcb, pcz, pcbc, pczc, na_bias, l, need_ctx)

        x = _out_projection(ya, yb, yc, x, gt, g_post3, out_wts, l, tmo)
        if need_ctx:
            ctx = _out_projection(yac, ybc, ycc, ctx, gtc, g_post3, out_wts, l, tmc)
    return x
```
